```python
import math
import numpy as np
import jax
import jax.numpy as jnp
from jax import lax

D_MODEL = 1024
BATCH = 32
SEQ = 2048
DEPTH = 2
DEC_BATCH = 16
DEC_SEQ = 2048
PAST_LEN = 128

GRID_W = 64

NA_HEADS = 4
NA_HEAD_DIM = 64
NA_ROWS = 8
NA_COLS = 16
NA_QCOLS = 16
NA_WIDTH = NA_HEADS * NA_HEAD_DIM

RW_HEADS = 4
RW_HEAD_DIM = 64
RW_WIDTH = RW_HEADS * RW_HEAD_DIM
RW_DECAY_LORA = 64
RW_ICLR_LORA = 64
RW_LNX_EPS = 64e-5

DF_HEADS = 4
DF_HEAD_DIM = 32
DF_WIDTH = 2 * DF_HEADS * DF_HEAD_DIM
DF_QBLOCK = 128
DF_EPS = 1e-5

DL_HEADS = 4
DL_HEAD_DIM = 64
DL_WIDTH = DL_HEADS * DL_HEAD_DIM
DL_GROUPS = ((128, 1), (512, 4), (2048, 16))
DL_NGROUPS = 3

N_BRANCH = 4
BRANCH_WIDTH = 256
A_COLS = 4 * NA_WIDTH
B_COLS = 4 * RW_WIDTH + 2 * RW_DECAY_LORA + 2 * RW_ICLR_LORA
C_COLS = 4 * DF_WIDTH
D_COLS = 3 * DL_NGROUPS * DL_WIDTH + DL_WIDTH
MERGE_COLS = N_BRANCH * D_MODEL
IN_COLS = A_COLS + B_COLS + C_COLS + D_COLS + MERGE_COLS

ROPE_THETA = 10000.0
LN_EPS = 1e-5
DEEPNORM_ALPHA = (2 * DEPTH) ** 0.25
DEEPNORM_BETA = (8 * DEPTH) ** -0.25
NEG_INF = -1e30

kernel_name = 'hybrid_bidir_encoder_gated_branches'


def split_cols(x, sizes):
    return jnp.split(x, np.cumsum(sizes)[:-1].tolist(), axis=-1)


def layer_norm(x, g, b):
    xf = x.astype(jnp.float32)
    mu = jnp.mean(xf, -1, keepdims=True)
    var = jnp.mean(jnp.square(xf - mu), -1, keepdims=True)
    return (xf - mu) * lax.rsqrt(var + LN_EPS) * g + b


def rotary(x):
    T, d = x.shape[1], x.shape[-1]
    half = d // 2
    inv_freq = jnp.power(ROPE_THETA, -jnp.arange(half, dtype=jnp.float32) / half)
    ang = jnp.arange(T, dtype=jnp.float32)[:, None] * inv_freq[None, :]
    cos = jnp.cos(ang)[None, :, None, :]
    sin = jnp.sin(ang)[None, :, None, :]
    xf = x.astype(jnp.float32)
    x1, x2 = xf[..., :half], xf[..., half:]
    return jnp.concatenate([x1 * cos - x2 * sin, x2 * cos + x1 * sin], axis=-1)


def neighbourhood_attention(q, k, v, rpb):
    B, T, H, dh = q.shape
    rows = T // GRID_W
    kr = min(NA_ROWS, rows)
    n_cb = GRID_W // NA_QCOLS
    span = NA_QCOLS + NA_COLS
    cb_start = np.clip(np.arange(n_cb) * NA_QCOLS - NA_COLS // 2, 0, GRID_W - span)
    key_cols = cb_start[:, None] + np.arange(span)[None, :]
    q_cols = np.arange(GRID_W).reshape(n_cb, NA_QCOLS)
    c_start = np.clip(q_cols - NA_COLS // 2, 0, GRID_W - NA_COLS)
    kc = key_cols[:, None, :]
    col_ok = (kc >= c_start[..., None]) & (kc < c_start[..., None] + NA_COLS)
    dc_idx = np.clip(kc - q_cols[..., None] + NA_COLS - 1, 0, 2 * NA_COLS - 2)
    rpb_cols = rpb[:, :, dc_idx].astype(jnp.float32)
    qg = (q.astype(jnp.float32) * dh ** -0.5).reshape(B, rows, n_cb, NA_QCOLS, H, dh)
    kg = k.astype(jnp.float32).reshape(B, rows, GRID_W, H, dh)[:, :, key_cols]
    vg = v.astype(jnp.float32).reshape(B, rows, GRID_W, H, dh)[:, :, key_cols]
    mask = col_ok[:, :, None, :]

    def one_row(r):
        rs = jnp.clip(r - kr // 2, 0, rows - kr)
        k_r = lax.dynamic_slice_in_dim(kg, rs, kr, axis=1)
        v_r = lax.dynamic_slice_in_dim(vg, rs, kr, axis=1)
        q_r = lax.dynamic_index_in_dim(qg, r, axis=1, keepdims=False)
        s = jnp.einsum('bjqhd,brjchd->bhjqrc', q_r, k_r)
        dr_idx = rs + jnp.arange(kr) - r + NA_ROWS - 1
        bias = jnp.take(rpb_cols, dr_idx, axis=1).transpose(0, 2, 3, 1, 4)
        s = jnp.where(mask, s + bias, NEG_INF)
        p = jax.nn.softmax(s, axis=(-2, -1))
        return jnp.einsum('bhjqrc,brjchd->bjqhd', p, v_r)

    out = lax.map(one_row, jnp.arange(rows))
    return out.transpose(1, 0, 2, 3, 4, 5).reshape(B, T, H * dh)


def centred_shift(u, mu):
    up = jnp.pad(u, ((0, 0), (1, 1), (0, 0)))
    return u + mu * (0.5 * (up[:, :-2] + up[:, 2:]) - u)


def rwkv7_bidirectional(u, mu, w0, w2, a0, a2, k_k, k_a, r_k, lnx_g, lnx_b):
    B, T, _ = u.shape
    H, N, C = RW_HEADS, RW_HEAD_DIM, RW_WIDTH
    u = centred_shift(u, mu).astype(jnp.float32)
    r, k, v, g, wl, al = split_cols(u, [C, C, C, C, 2 * RW_DECAY_LORA, 2 * RW_ICLR_LORA])
    wl = wl.reshape(B, T, 2, RW_DECAY_LORA)
    al = al.reshape(B, T, 2, RW_ICLR_LORA)
    w_raw = w0 + jnp.einsum('btdl,dlc->btdc', jnp.tanh(wl), w2)
    decay = jnp.exp(-jnp.exp(-jax.nn.softplus(-w_raw) - 0.5))
    a = jax.nn.sigmoid(a0 + jnp.einsum('btdl,dlc->btdc', al, a2))
    kk = (k * k_k).reshape(B, T, H, N)
    kk = (kk * lax.rsqrt(jnp.maximum(jnp.sum(kk * kk, -1, keepdims=True), 1e-24))).reshape(B, T, C)
    kd = k[:, :, None, :] * (1.0 + (a - 1.0) * k_a)

    def per_dir(z):
        z = jnp.stack([z[:, :, 0], jnp.flip(z[:, :, 1], axis=1)], axis=0)
        return z.reshape(2, B, T, H, N).transpose(2, 0, 1, 3, 4)

    def shared(z):
        return per_dir(jnp.stack([z, z], axis=2))

    def step(S, inp):
        r_t, w_t, k_t, v_t, kk_t, a_t = inp
        sa = jnp.einsum('dbhvk,dbhk->dbhv', S, kk_t)
        S = S * w_t[..., None, :] - sa[..., :, None] * (kk_t * a_t)[..., None, :] + v_t[..., :, None] * k_t[..., None, :]
        return S, jnp.einsum('dbhvk,dbhk->dbhv', S, r_t)

    S0 = jnp.zeros((2, B, H, N, N), jnp.float32)
    _, ys = lax.scan(step, S0, (shared(r), per_dir(decay), per_dir(kd), shared(v), shared(kk), per_dir(a)))
    y = (ys[:, 0] + jnp.flip(ys[:, 1], axis=0)).transpose(1, 0, 2, 3)
    mu_y = jnp.mean(y, -1, keepdims=True)
    var_y = jnp.mean(jnp.square(y - mu_y), -1, keepdims=True)
    y = ((y - mu_y) * lax.rsqrt(var_y + RW_LNX_EPS)).reshape(B, T, C) * lnx_g + lnx_b
    bonus = jnp.sum((r[:, :, None, :] * kd * r_k.reshape(C)).reshape(B, T, 2, H, N), axis=(2, 4))
    y = y + (bonus[..., None] * v.reshape(B, T, H, N)).reshape(B, T, C)
    return y * jax.nn.silu(g)


def diff_attention(q, k, v, lam, subln_g, lam_init):
    B, T, _ = q.shape
    H, d = DF_HEADS, DF_HEAD_DIM
    q = rotary(q.reshape(B, T, 2 * H, d)).reshape(B, T, H, 2, d) * d ** -0.5
    k = rotary(k.reshape(B, T, 2 * H, d)).reshape(B, T, H, 2, d)
    v = v.reshape(B, T, H, 2 * d).astype(jnp.float32)
    lamf = lam.astype(jnp.float32)
    lam_full = jnp.exp(jnp.sum(lamf[0] * lamf[1])) - jnp.exp(jnp.sum(lamf[2] * lamf[3])) + lam_init
    nb = T // DF_QBLOCK
    q_blocks = q.reshape(B, nb, DF_QBLOCK, H, 2, d).transpose(1, 0, 2, 3, 4, 5)

    def attend_block(qb):
        s = jnp.einsum('bqhid,bkhid->bhiqk', qb, k)
        p = jax.nn.softmax(s, axis=-1)
        w = p[:, :, 0] - lam_full * p[:, :, 1]
        return jnp.einsum('bhqk,bkhe->bqhe', w, v)

    o = lax.map(attend_block, q_blocks)
    o = o.transpose(1, 0, 2, 3, 4).reshape(B, T, H, 2 * d)
    o = o * lax.rsqrt(jnp.mean(jnp.square(o), -1, keepdims=True) + DF_EPS) * subln_g * (1.0 - lam_init)
    return o.reshape(B, T, H * 2 * d)


def band_attention(q, k, v, half):
    lead = q.shape[:-2]
    L, dh = q.shape[-2], q.shape[-1]
    nb = -(-L // half)
    Lp = nb * half
    padw = [(0, 0)] * len(lead)
    qp = jnp.pad(q, padw + [(0, Lp - L), (0, 0)]).reshape(*lead, nb, half, dh)

    def windows(x):
        xp = jnp.pad(x, padw + [(half, Lp - L + half), (0, 0)]).reshape(*lead, nb + 2, half, dh)
        return jnp.concatenate([xp[..., :-2, :, :], xp[..., 1:-1, :, :], xp[..., 2:, :, :]], axis=-2)

    kw, vw = windows(k), windows(v)
    qpos = np.arange(nb)[:, None, None] * half + np.arange(half)[None, :, None]
    kpos = (np.arange(nb)[:, None, None] - 1) * half + np.arange(3 * half)[None, None, :]
    ok = (np.abs(kpos - qpos) <= half) & (kpos >= 0) & (kpos < L)
    s = jnp.where(ok, jnp.einsum('...nqd,...nkd->...nqk', qp, kw), NEG_INF)
    m = jnp.max(s, -1, keepdims=True)
    p = jnp.exp(s - m)
    den = jnp.sum(p, -1, keepdims=True)
    o = jnp.einsum('...nqk,...nkd->...nqd', p, vw) / den
    lse = (m + jnp.log(den))[..., 0]
    return o.reshape(*lead, Lp, dh)[..., :L, :], lse.reshape(*lead, Lp)[..., :L]


def dilated_attention(qkv):
    B, T, _ = qkv.shape
    H, dh = DL_HEADS, DL_HEAD_DIM
    qkv = qkv.reshape(B, T, DL_NGROUPS, 3, H, dh)
    outs, lses = [], []
    for gi, (window, dil) in enumerate(DL_GROUPS):
        half = window // (2 * dil)
        L = T // dil

        def residues(x):
            return x.astype(jnp.float32).reshape(B, L, dil, H, dh).transpose(0, 2, 3, 1, 4)

        q = residues(rotary(qkv[:, :, gi, 0])) * dh ** -0.5
        k = residues(rotary(qkv[:, :, gi, 1]))
        v = residues(qkv[:, :, gi, 2])
        o, lse = band_attention(q, k, v, half)
        outs.append(o.transpose(0, 3, 1, 2, 4).reshape(B, T, H, dh))
        lses.append(lse.transpose(0, 3, 1, 2).reshape(B, T, H))
    wts = jax.nn.softmax(jnp.stack(lses, 0), axis=0)
    out = jnp.sum(wts[..., None] * jnp.stack(outs, 0), axis=0)
    return out.reshape(B, T, H * dh)


def encode(x, ln0_g, ln0_b, w_in, b_in, na_rpb, rw_mu, rw_w0, rw_w2, rw_a0, rw_a2, rw_kk, rw_ka,
           rw_rk, rw_lnx_g, rw_lnx_b, df_lam, df_subln_g, w_branch, w_out, b_out, ln_g, ln_b):
    dtype = x.dtype
    x = layer_norm(x, ln0_g, ln0_b).astype(dtype)
    for l in range(DEPTH):
        B, T, _ = x.shape
        h = x @ w_in[l] + b_in[l]
        ua, ub, uc, ud, ug = split_cols(h, [A_COLS, B_COLS, C_COLS, D_COLS, MERGE_COLS])
        aq, ak, av, ag = split_cols(ua, [NA_WIDTH] * 4)
        heads = lambda z: z.reshape(B, T, NA_HEADS, NA_HEAD_DIM)
        y_a = neighbourhood_attention(heads(aq), heads(ak), heads(av), na_rpb[l]) * jax.nn.silu(ag)
        y_b = rwkv7_bidirectional(ub, rw_mu[l], rw_w0[l], rw_w2[l], rw_a0[l], rw_a2[l], rw_kk[l],
                                  rw_ka[l], rw_rk[l], rw_lnx_g[l], rw_lnx_b[l])
        cq, ck, cv, cg = split_cols(uc, [DF_WIDTH] * 4)
        lam_init = 0.8 - 0.6 * math.exp(-0.3 * l)
        y_c = diff_attention(cq, ck, cv, df_lam[l], df_subln_g[l], lam_init) * jax.nn.silu(cg)
        dqkv, dg = split_cols(ud, [3 * DL_NGROUPS * DL_WIDTH, DL_WIDTH])
        y_d = dilated_attention(dqkv) * jax.nn.silu(dg)
        gates = jax.nn.sigmoid(ug.astype(jnp.float32)).reshape(B, T, N_BRANCH, D_MODEL)
        merged = (gates[:, :, 0] * (y_a @ w_branch[l, 0]) + gates[:, :, 1] * (y_b @ w_branch[l, 1])
                  + gates[:, :, 2] * (y_c @ w_branch[l, 2]) + gates[:, :, 3] * (y_d @ w_branch[l, 3]))
        y = merged @ w_out[l] + b_out[l]
        x = layer_norm(DEEPNORM_ALPHA * x + y, ln_g[l], ln_b[l]).astype(dtype)
    return x


def setup_inputs(seed: int = 0) -> dict:
    key = jax.random.key(seed)
    ks = jax.random.split(key, 26)
    f32 = jnp.float32

    def nrm(k, shape, s):
        return jax.random.normal(k, shape, f32) * s

    return {
        'x_prompt': nrm(ks[0], (BATCH, SEQ, D_MODEL), 1.0),
        'x_sample': nrm(ks[1], (DEC_BATCH, DEC_SEQ, D_MODEL), 1.0),
        'ln0_g': 1.0 + nrm(ks[2], (D_MODEL,), 0.02),
        'ln0_b': nrm(ks[3], (D_MODEL,), 0.02),
        'w_in': nrm(ks[4], (DEPTH, D_MODEL, IN_COLS), D_MODEL ** -0.5),
        'b_in': nrm(ks[5], (DEPTH, IN_COLS), 0.02),
        'na_rpb': nrm(ks[6], (DEPTH, NA_HEADS, 2 * NA_ROWS - 1, 2 * NA_COLS - 1), 0.1),
        'rw_mu': jax.random.uniform(ks[7], (DEPTH, B_COLS), f32, 0.0, 1.0),
        'rw_w0': jax.random.uniform(ks[8], (DEPTH, 2, RW_WIDTH), f32, -5.0, 1.0),
        'rw_w2': nrm(ks[9], (DEPTH, 2, RW_DECAY_LORA, RW_WIDTH), 0.1 * RW_DECAY_LORA ** -0.5),
        'rw_a0': nrm(ks[10], (DEPTH, 2, RW_WIDTH), 0.1),
        'rw_a2': nrm(ks[11], (DEPTH, 2, RW_ICLR_LORA, RW_WIDTH), 0.1 * RW_ICLR_LORA ** -0.5),
        'rw_kk': 0.85 + nrm(ks[12], (DEPTH, RW_WIDTH), 0.02),
        'rw_ka': 1.0 + nrm(ks[13], (DEPTH, RW_WIDTH), 0.02),
        'rw_rk': nrm(ks[14], (DEPTH, RW_HEADS, RW_HEAD_DIM), 0.1),
        'rw_lnx_g': 1.0 + nrm(ks[15], (DEPTH, RW_WIDTH), 0.02),
        'rw_lnx_b': nrm(ks[16], (DEPTH, RW_WIDTH), 0.02),
        'df_lam': nrm(ks[17], (DEPTH, 4, DF_HEAD_DIM), 0.1),
        'df_subln_g': 1.0 + nrm(ks[18], (DEPTH, 2 * DF_HEAD_DIM), 0.02),
        'w_branch': nrm(ks[19], (DEPTH, N_BRANCH, BRANCH_WIDTH, D_MODEL), DEEPNORM_BETA * BRANCH_WIDTH ** -0.5),
        'w_out': nrm(ks[20], (DEPTH, D_MODEL, D_MODEL), DEEPNORM_BETA * D_MODEL ** -0.5),
        'b_out': nrm(ks[21], (DEPTH, D_MODEL), 0.02),
        'ln_g': 1.0 + nrm(ks[22], (DEPTH, D_MODEL), 0.02),
        'ln_b': nrm(ks[23], (DEPTH, D_MODEL), 0.02),
    }


def reference(x_prompt, x_sample, ln0_g, ln0_b, w_in, b_in, na_rpb, rw_mu, rw_w0, rw_w2, rw_a0, rw_a2,
              rw_kk, rw_ka, rw_rk, rw_lnx_g, rw_lnx_b, df_lam, df_subln_g, w_branch, w_out, b_out,
              ln_g, ln_b):
    weights = (ln0_g, ln0_b, w_in, b_in, na_rpb, rw_mu, rw_w0, rw_w2, rw_a0, rw_a2, rw_kk, rw_ka,
               rw_rk, rw_lnx_g, rw_lnx_b, df_lam, df_subln_g, w_branch, w_out, b_out, ln_g, ln_b)
    y_prompt = encode(x_prompt, *weights)
    y_sample = encode(x_sample, *weights)
    return (y_prompt, y_sample)
```

```python
import functools
import math

import numpy as np
import jax
import jax.numpy as jnp
from jax import lax
from jax.experimental import pallas as pl
from jax.experimental.pallas import tpu as pltpu

F32 = jnp.float32
BF16 = jnp.bfloat16

D_MODEL = 1024
DEPTH = 2
GRID_W = 64
NA_HEADS, NA_HEAD_DIM, NA_ROWS, NA_COLS = 4, 64, 8, 16
RW_HEADS, RW_HEAD_DIM, RW_LORA = 4, 64, 64
RW_LNX_EPS = 64e-5
DF_HEADS, DF_HEAD_DIM, DF_EPS = 4, 32, 1e-5
DL_HEADS, DL_HEAD_DIM = 4, 64
DL_DILATIONS = (1, 4, 16)
DL_WINDOWS = (128, 512, 2048)
DL_HALF = 64
BW = 256
ROPE_THETA = 10000.0
LN_EPS = 1e-5
DEEPNORM_ALPHA = (2 * DEPTH) ** 0.25
NEG_INF = -1e30

LANES = 128
VMEM_LIMIT = 56 * 1024 * 1024


def _cparams(sem):
    return pltpu.CompilerParams(dimension_semantics=sem, vmem_limit_bytes=VMEM_LIMIT)


def _dot_nt(a, b):
    return lax.dot_general(a, b, (((1,), (1,)), ((), ())), preferred_element_type=F32)


def _split3(x):
    hi = x.astype(BF16)
    r1 = x - hi.astype(F32)
    mid = r1.astype(BF16)
    lo = (r1 - mid.astype(F32)).astype(BF16)
    return hi, mid, lo


def _seg_sum(x, seg):
    n = x.shape[-1]
    r = lax.broadcasted_iota(jnp.int32, (n, n), 0) // seg
    c = lax.broadcasted_iota(jnp.int32, (n, n), 1) // seg
    ones = (r == c).astype(BF16)
    out = None
    for t in _split3(x):
        y = jnp.dot(t, ones, preferred_element_type=F32)
        out = y if out is None else out + y
    return out


def _ln0_kernel(xp_ref, xs_ref, g_ref, b_ref, of_ref, ob_ref, *, n_prompt):
    def norm(x):
        mu = jnp.mean(x, -1, keepdims=True)
        xc = x - mu
        var = jnp.mean(xc * xc, -1, keepdims=True)
        y = xc * lax.rsqrt(var + LN_EPS) * g_ref[...] + b_ref[...]
        of_ref[...] = y
        ob_ref[...] = y.astype(BF16)

    i = pl.program_id(0)

    @pl.when(i < n_prompt)
    def _():
        norm(xp_ref[...])

    @pl.when(i >= n_prompt)
    def _():
        norm(xs_ref[...])


def _ln0(xp, xs, g, b, tm=512):
    mp, ms = xp.shape[0], xs.shape[0]
    n_p, n_s = mp // tm, ms // tm
    out = jax.ShapeDtypeStruct((mp + ms, D_MODEL), F32)
    outb = jax.ShapeDtypeStruct((mp + ms, D_MODEL), BF16)
    return pl.pallas_call(
        functools.partial(_ln0_kernel, n_prompt=n_p),
        grid=(n_p + n_s,),
        in_specs=[
            pl.BlockSpec((tm, D_MODEL), lambda i: (jnp.minimum(i, n_p - 1), 0)),
            pl.BlockSpec((tm, D_MODEL), lambda i: (jnp.maximum(i - n_p, 0), 0)),
            pl.BlockSpec((1, D_MODEL), lambda i: (0, 0)),
            pl.BlockSpec((1, D_MODEL), lambda i: (0, 0)),
        ],
        out_specs=[pl.BlockSpec((tm, D_MODEL), lambda i: (i, 0)),
                   pl.BlockSpec((tm, D_MODEL), lambda i: (i, 0))],
        out_shape=[out, outb],
        compiler_params=_cparams(("parallel",)),
        name="ln0",
    )(xp, xs, g.reshape(1, -1), b.reshape(1, -1))


def _proj_kernel(x_ref, w_ref, b_ref, *rest, rot_hd, chunk_major):
    acc = jnp.dot(x_ref[...], w_ref[...], preferred_element_type=F32) + b_ref[...]
    if not rot_hd:
        (o_ref,) = rest
        o_ref[...] = acc.astype(o_ref.dtype)
        return
    cs_ref, sn_ref, o_ref = rest
    h2 = rot_hd // 2
    lane = lax.broadcasted_iota(jnp.int32, (1, LANES), 1)
    first = (lane % rot_hd) < h2
    cs, sn = cs_ref[...], sn_ref[...]
    for c in range(acc.shape[1] // LANES):
        ch = acc[:, c * LANES:(c + 1) * LANES]
        sw = jnp.where(first, pltpu.roll(ch, LANES - h2, 1), pltpu.roll(ch, h2, 1))
        val = (ch * cs + sw * sn).astype(o_ref.dtype)
        if chunk_major:
            o_ref[c] = val
        else:
            o_ref[:, c * LANES:(c + 1) * LANES] = val


def _proj(xb, w, b, out_dtype, *, seq, tm, tn, rot=None, chunk_major=False, name="proj"):
    m, n = xb.shape[0], w.shape[1]
    assert m % tm == 0 and n % tn == 0 and seq % tm == 0
    in_specs = [
        pl.BlockSpec((tm, D_MODEL), lambda i, j: (i, 0)),
        pl.BlockSpec((D_MODEL, tn), lambda i, j: (0, j)),
        pl.BlockSpec((1, tn), lambda i, j: (0, j)),
    ]
    args = [xb, w, b.reshape(1, -1)]
    rot_hd = 0
    if rot is not None:
        rot_hd, cos, sin, kind_of = rot
        assert tn % LANES == 0 and LANES % rot_hd == 0
        spt = seq // tm
        tab = pl.BlockSpec((None, tm, LANES), lambda i, j: (kind_of(j), i % spt, 0))
        in_specs += [tab, tab]
        args += [cos, sin]
    if chunk_major:
        assert rot is not None
        out_spec = pl.BlockSpec((tn // LANES, tm, LANES), lambda i, j: (j, i, 0))
        out_shape = jax.ShapeDtypeStruct((n // LANES, m, LANES), out_dtype)
    else:
        out_spec = pl.BlockSpec((tm, tn), lambda i, j: (i, j))
        out_shape = jax.ShapeDtypeStruct((m, n), out_dtype)
    return pl.pallas_call(
        functools.partial(_proj_kernel, rot_hd=rot_hd, chunk_major=chunk_major),
        grid=(m // tm, n // tn),
        in_specs=in_specs,
        out_specs=out_spec,
        out_shape=out_shape,
        compiler_params=_cparams(("parallel", "arbitrary")),
        name=name,
    )(*args)


def _rope_tables(seq, head_dim, scales):
    half = head_dim // 2
    inv_freq = jnp.power(ROPE_THETA, -jnp.arange(half, dtype=F32) / half)
    ang = jnp.arange(seq, dtype=F32)[:, None] * inv_freq[None, :]
    lane = np.arange(LANES)
    f_idx = lane % half
    sign = np.where((lane % head_dim) < half, -1.0, 1.0).astype(np.float32)
    cos = jnp.cos(ang)[:, f_idx]
    sin = jnp.sin(ang)[:, f_idx] * sign[None, :]
    cs, sn = [], []
    for s in scales:
        if s is None:
            cs.append(jnp.ones((seq, LANES), F32))
            sn.append(jnp.zeros((seq, LANES), F32))
        else:
            cs.append(cos * F32(s))
            sn.append(sin * F32(s))
    return jnp.stack(cs), jnp.stack(sn)


def _merge_kernel(ya_ref, yb_ref, yc_ref, yd_ref, g_ref, x_ref, wb_ref, wo_ref, bo_ref, lg_ref,
                  lb_ref, of_ref, ob_ref):
    merged = None
    for i, y_ref in enumerate((ya_ref, yb_ref, yc_ref, yd_ref)):
        p = jnp.dot(y_ref[...], wb_ref[i], preferred_element_type=F32)
        gate = jax.nn.sigmoid(g_ref[:, i * D_MODEL:(i + 1) * D_MODEL])
        merged = gate * p if merged is None else merged + gate * p
    y = jnp.dot(merged.astype(BF16), wo_ref[...], preferred_element_type=F32) + bo_ref[...]
    z = DEEPNORM_ALPHA * x_ref[...] + y
    mu = jnp.mean(z, -1, keepdims=True)
    zc = z - mu
    var = jnp.mean(zc * zc, -1, keepdims=True)
    out = zc * lax.rsqrt(var + LN_EPS) * lg_ref[...] + lb_ref[...]
    of_ref[...] = out
    ob_ref[...] = out.astype(BF16)


def _merge(ya, yb, yc, yd, gates, x, wb, wo, bo, lg, lb, *, row0, rows, tm=256):
    assert row0 % tm == 0 and rows % tm == 0
    o = row0 // tm
    ysp = pl.BlockSpec((tm, BW), lambda i: (i + o, 0))
    const2 = lambda i: (0, 0)
    return pl.pallas_call(
        _merge_kernel,
        grid=(rows // tm,),
        in_specs=[ysp, ysp, ysp, ysp,
                  pl.BlockSpec((tm, 4 * D_MODEL), lambda i: (i + o, 0)),
                  pl.BlockSpec((tm, D_MODEL), lambda i: (i + o, 0)),
                  pl.BlockSpec((4, BW, D_MODEL), lambda i: (0, 0, 0)),
                  pl.BlockSpec((D_MODEL, D_MODEL), const2),
                  pl.BlockSpec((1, D_MODEL), const2),
                  pl.BlockSpec((1, D_MODEL), const2),
                  pl.BlockSpec((1, D_MODEL), const2)],
        out_specs=[pl.BlockSpec((tm, D_MODEL), lambda i: (i, 0)),
                   pl.BlockSpec((tm, D_MODEL), lambda i: (i, 0))],
        out_shape=[jax.ShapeDtypeStruct((rows, D_MODEL), F32),
                   jax.ShapeDtypeStruct((rows, D_MODEL), BF16)],
        compiler_params=_cparams(("parallel",)),
        name="merge",
    )(ya, yb, yc, yd, gates, x, wb, wo, bo.reshape(1, -1), lg.reshape(1, -1), lb.reshape(1, -1))


def _na_bias_tables(rpb, rows):
    kr = min(NA_ROWS, rows)
    qc = np.arange(GRID_W)[:, None]
    kc = np.arange(GRID_W)[None, :]
    c_start = np.clip(qc - NA_COLS // 2, 0, GRID_W - NA_COLS)
    ok = (kc >= c_start) & (kc < c_start + NA_COLS)
    dc = np.clip(kc - qc + NA_COLS - 1, 0, 2 * NA_COLS - 2)
    n_off = 2 * NA_ROWS - kr
    dr = np.arange(n_off)[:, None] + np.arange(kr)[None, :]
    b = rpb.astype(F32)[:, dr][:, :, :, dc]
    b = jnp.where(ok[None, None, None], b, NEG_INF)
    return b.transpose(0, 1, 3, 2, 4).reshape(NA_HEADS, n_off, GRID_W, kr * GRID_W)


def _na_kernel(q_ref, k_ref, v_ref, g_ref, bias_ref, o_ref, *, rows):
    kr = min(NA_ROWS, rows)
    r = pl.program_id(1)
    rs = jnp.clip(r - kr // 2, 0, rows - kr)
    start = pl.multiple_of(rs * GRID_W, GRID_W)
    q = q_ref[...]
    k = k_ref[pl.ds(start, kr * GRID_W), :]
    v = v_ref[pl.ds(start, kr * GRID_W), :]
    lane = lax.broadcasted_iota(jnp.int32, (1, BW), 1)
    acc = jnp.zeros((GRID_W, BW), F32)
    for h in range(NA_HEADS):
        in_head = (lane >= h * NA_HEAD_DIM) & (lane < (h + 1) * NA_HEAD_DIM)
        qm = q * in_head.astype(BF16)
        s = _dot_nt(qm, k)
        bias = bias_ref[h]
        s = jnp.where(bias > 0.5 * NEG_INF, s + bias, NEG_INF)
        m = jnp.max(s, -1, keepdims=True)
        p = jnp.exp(s - m)
        l = jnp.sum(p, -1, keepdims=True)
        o = jnp.dot(p.astype(BF16), v, preferred_element_type=F32) / l
        acc = jnp.where(in_head, o, acc)
    o_ref[...] = (acc * jax.nn.silu(g_ref[...])).astype(o_ref.dtype)


def _na(hb, fb, bias_tab, *, batch, seq):
    rows = seq // GRID_W
    kr = min(NA_ROWS, rows)
    assert rows >= kr and kr == NA_ROWS
    n_off = bias_tab.shape[1]

    def off_of(r):
        return jnp.clip(r - kr // 2, 0, rows - kr) - r + NA_ROWS - 1

    return pl.pallas_call(
        functools.partial(_na_kernel, rows=rows),
        grid=(batch, rows),
        in_specs=[
            pl.BlockSpec((None, GRID_W, BW), lambda b, r: (b, r, 0)),
            pl.BlockSpec((None, seq, BW), lambda b, r: (b, 0, 1)),
            pl.BlockSpec((None, seq, BW), lambda b, r: (b, 0, 2)),
            pl.BlockSpec((None, GRID_W, BW), lambda b, r: (b, r, 5)),
            pl.BlockSpec((NA_HEADS, None, GRID_W, kr * GRID_W), lambda b, r: (0, off_of(r), 0, 0)),
        ],
        out_specs=pl.BlockSpec((None, GRID_W, BW), lambda b, r: (b, r, 0)),
        out_shape=jax.ShapeDtypeStruct((batch, seq, BW), BF16),
        compiler_params=_cparams(("parallel", "arbitrary")),
        name="na_attn",
    )(hb, hb, hb, fb, bias_tab)


RW_C = 64
RW_FIELDS = 9
RW_PASSES = 3


def _mm(a, b, mode="nn", passes=RW_PASSES):
    dims = {"nn": (((1,), (0,)), ((), ())), "nt": (((1,), (1,)), ((), ())),
            "tn": (((0,), (0,)), ((), ()))}[mode]
    dg = lambda x, y: lax.dot_general(x, y, dims, preferred_element_type=F32)
    ah, bh = a.astype(BF16), b.astype(BF16)
    out = dg(ah, bh)
    if passes == 3:
        al = (a - ah.astype(F32)).astype(BF16)
        bl = (b - bh.astype(F32)).astype(BF16)
        out = out + (dg(al, bh) + dg(ah, bl))
    return out


def _softplus(x):
    return jnp.maximum(x, 0.0) + jnp.log(1.0 + jnp.exp(-jnp.abs(x)))


def _rwprep_kernel(x_ref, prev_ref, next_ref, mu_ref, w0_ref, w2_ref, a0_ref, a2_ref, kk_ref, ka_ref,
                   rk_ref, f_ref, e_ref, *, rc):
    c = pl.program_id(1)
    lane = lax.broadcasted_iota(jnp.int32, (1, 2 * RW_LORA), 1)
    ri = lax.broadcasted_iota(jnp.int32, (rc, 1), 0)
    x = x_ref[...]
    prev_row = jnp.where(c > 0, prev_ref[7:8, :], 0.0)
    next_row = jnp.where(c < pl.num_programs(1) - 1, next_ref[0:1, :], 0.0)
    up = jnp.where(ri == 0, prev_row, pltpu.roll(x, 1, 0))
    dn = jnp.where(ri == rc - 1, next_row, pltpu.roll(x, rc - 1, 0))
    xs = x + mu_ref[...] * (0.5 * (up + dn) - x)
    r, k, v, g = (xs[:, i * BW:(i + 1) * BW] for i in range(4))
    wl = xs[:, 4 * BW:4 * BW + 2 * RW_LORA]
    al = xs[:, 4 * BW + 2 * RW_LORA:4 * BW + 4 * RW_LORA]
    kk = k * kk_ref[...]
    kap = kk * lax.rsqrt(jnp.maximum(_seg_sum(kk * kk, RW_HEAD_DIM), 1e-24))
    tw = jnp.tanh(wl)
    kd_sum = None
    for d in range(2):
        dm = ((lane >= d * RW_LORA) & (lane < (d + 1) * RW_LORA)).astype(F32)
        w_raw = w0_ref[d:d + 1, :] + _mm(tw * dm, w2_ref[...])
        lw = -jnp.exp(-_softplus(-w_raw) - 0.5)
        a = jax.nn.sigmoid(a0_ref[d:d + 1, :] + _mm(al * dm, a2_ref[...]))
        kd = k * (1.0 + (a - 1.0) * ka_ref[...])
        f_ref[:, (3 + 3 * d) * BW:(4 + 3 * d) * BW] = lw
        f_ref[:, (4 + 3 * d) * BW:(5 + 3 * d) * BW] = kd
        f_ref[:, (5 + 3 * d) * BW:(6 + 3 * d) * BW] = kap * a
        kd_sum = kd if kd_sum is None else kd_sum + kd
    f_ref[:, 0:BW] = r
    f_ref[:, BW:2 * BW] = v
    f_ref[:, 2 * BW:3 * BW] = kap
    bonus = _seg_sum(r * kd_sum * rk_ref[...], RW_HEAD_DIM)
    e_ref[:, 0:BW] = bonus * v
    e_ref[:, BW:2 * BW] = jax.nn.silu(g)


def _rwprep(fb, mu, w0, w2, a0, a2, k_k, k_a, r_k, *, batch, seq, rc=256):
    ucols = 4 * BW + 4 * RW_LORA
    row = lambda z: z.astype(F32).reshape(1, -1)
    c2 = lambda b, c: (0, 0)
    tpb = rc // 8
    return pl.pallas_call(
        functools.partial(_rwprep_kernel, rc=rc),
        grid=(batch, seq // rc),
        in_specs=[
            pl.BlockSpec((None, rc, ucols), lambda b, c: (b, c, 0)),
            pl.BlockSpec((None, 8, ucols), lambda b, c: (b, jnp.maximum(c * tpb - 1, 0), 0)),
            pl.BlockSpec((None, 8, ucols),
                         lambda b, c: (b, jnp.minimum((c + 1) * tpb, seq // 8 - 1), 0)),
            pl.BlockSpec((1, ucols), c2),
            pl.BlockSpec((2, BW), c2),
            pl.BlockSpec((2 * RW_LORA, BW), c2),
            pl.BlockSpec((2, BW), c2),
            pl.BlockSpec((2 * RW_LORA, BW), c2),
            pl.BlockSpec((1, BW), c2),
            pl.BlockSpec((1, BW), c2),
            pl.BlockSpec((1, BW), c2),
        ],
        out_specs=[pl.BlockSpec((None, rc, RW_FIELDS * BW), lambda b, c: (b, c, 0)),
                   pl.BlockSpec((None, rc, 2 * BW), lambda b, c: (b, c, 0))],
        out_shape=[jax.ShapeDtypeStruct((batch, seq, RW_FIELDS * BW), F32),
                   jax.ShapeDtypeStruct((batch, seq, 2 * BW), F32)],
        compiler_params=_cparams(("parallel", "arbitrary")),
        name="rwkv_prep",
    )(fb, fb, fb, row(mu), w0.astype(F32), w2.astype(F32).reshape(2 * RW_LORA, BW), a0.astype(F32),
      a2.astype(F32).reshape(2 * RW_LORA, BW), row(k_k), row(k_a), row(r_k))


def _rw_direction(f_ref, z_ref, d):
    C, H = RW_C, RW_HEADS
    r = f_ref[:, 0:BW]
    v = f_ref[:, BW:2 * BW]
    kap = f_ref[:, 2 * BW:3 * BW]
    lw = f_ref[:, (3 + 3 * d) * BW:(4 + 3 * d) * BW]
    kd = f_ref[:, (4 + 3 * d) * BW:(5 + 3 * d) * BW]
    beta = f_ref[:, (5 + 3 * d) * BW:(6 + 3 * d) * BW]
    upper = d == 1
    ri = lax.broadcasted_iota(jnp.int32, (C, C), 0)
    ci = lax.broadcasted_iota(jnp.int32, (C, C), 1)
    tri = ((ci >= ri) if upper else (ci <= ri)).astype(BF16)
    cs = None
    for t in _split3(lw):
        y = jnp.dot(tri, t, preferred_element_type=F32)
        cs = y if cs is None else cs + y
    tot = cs[0:1] if upper else cs[C - 1:C]
    ginv = jnp.exp(-cs)
    gto = jnp.exp(tot - cs)
    lane = lax.broadcasted_iota(jnp.int32, (1, BW), 1)
    hms = [((lane >= h * RW_HEAD_DIM) & (lane < (h + 1) * RW_HEAD_DIM)).astype(F32)
           for h in range(H)]
    stack = lambda x: jnp.concatenate([x * hm for hm in hms], axis=0)
    kt_s = stack(kap * jnp.exp(cs - lw))
    rt_s = stack(r * jnp.exp(cs))
    kg_s = stack(kd * ginv)
    bg_s = stack(beta * ginv)
    kh_s = stack(kd * gto)
    bh_s = stack(beta * gto)
    v_s = stack(v)
    rr = lax.broadcasted_iota(jnp.int32, (H * C, H * C), 0)
    cc = lax.broadcasted_iota(jnp.int32, (H * C, H * C), 1)
    same = (rr // C) == (cc // C)
    strict = same & ((cc > rr) if upper else (cc < rr))
    incl = same & ((cc >= rr) if upper else (cc <= rr))
    l_kk = jnp.where(strict, _mm(kt_s, kg_s, "nt"), 0.0)
    l_bk = jnp.where(strict, _mm(kt_s, bg_s, "nt"), 0.0)
    l_kr = jnp.where(incl, _mm(rt_s, kg_s, "nt"), 0.0)
    l_br = jnp.where(incl, _mm(rt_s, bg_s, "nt"), 0.0)
    eye = rr == cc
    npow = -l_bk
    w = jnp.where(eye, 1.0, 0.0) + npow
    for _ in range(int(math.log2(C)) - 1):
        npow = _mm(npow, npow)
        w = w + _mm(npow, w)
    p1 = _mm(w, kt_s)
    p2 = _mm(w, l_kk)
    g_s = rt_s - _mm(l_br, p1)
    hm = l_kr - _mm(l_br, p2)
    p2v = _mm(p2, v_s)
    hv_s = _mm(hm, v_s)
    mt = jnp.where(eye, jnp.exp(tot), 0.0) - _mm(bh_s, p1, "tn")
    nt = _mm(kh_s, v_s, "tn") - _mm(bh_s, p2v, "tn")
    z0 = z_ref[...]
    y_s = _mm(g_s, z0) + hv_s
    z_ref[...] = _mm(mt, z0) + nt
    return y_s[0:C] + y_s[C:2 * C] + y_s[2 * C:3 * C] + y_s[3 * C:4 * C]


def _rwscan_kernel(ff_ref, fr_ref, yf_ref, yr_ref, zf_ref, zr_ref):
    @pl.when(pl.program_id(1) == 0)
    def _():
        zf_ref[...] = jnp.zeros_like(zf_ref)
        zr_ref[...] = jnp.zeros_like(zr_ref)

    yf_ref[...] = _rw_direction(ff_ref, zf_ref, 0)
    yr_ref[...] = _rw_direction(fr_ref, zr_ref, 1)


def _rwscan(f, *, batch, seq):
    nc = seq // RW_C
    fsp = lambda imap: pl.BlockSpec((None, RW_C, RW_FIELDS * BW), imap)
    ysp = lambda imap: pl.BlockSpec((None, RW_C, BW), imap)
    fwd = lambda b, s: (b, s, 0)
    rev = lambda b, s: (b, nc - 1 - s, 0)
    y = jax.ShapeDtypeStruct((batch, seq, BW), F32)
    return pl.pallas_call(
        _rwscan_kernel,
        grid=(batch, nc),
        in_specs=[fsp(fwd), fsp(rev)],
        out_specs=[ysp(fwd), ysp(rev)],
        out_shape=[y, y],
        scratch_shapes=[pltpu.VMEM((BW, BW), F32), pltpu.VMEM((BW, BW), F32)],
        compiler_params=_cparams(("parallel", "arbitrary")),
        name="rwkv_scan",
    )(f, f)


def _rwpost_kernel(yf_ref, yr_ref, e_ref, lg_ref, lb_ref, o_ref):
    y = yf_ref[...] + yr_ref[...]
    inv = 1.0 / RW_HEAD_DIM
    yc = y - _seg_sum(y, RW_HEAD_DIM) * inv
    var = _seg_sum(yc * yc, RW_HEAD_DIM) * inv
    yn = yc * lax.rsqrt(var + RW_LNX_EPS) * lg_ref[...] + lb_ref[...]
    o_ref[...] = ((yn + e_ref[:, 0:BW]) * e_ref[:, BW:2 * BW]).astype(o_ref.dtype)


def _rwpost(yf, yr, e, lnx_g, lnx_b, *, tm=512):
    m = yf.shape[0]
    row = lambda z: z.astype(F32).reshape(1, -1)
    ysp = pl.BlockSpec((tm, BW), lambda i: (i, 0))
    return pl.pallas_call(
        _rwpost_kernel,
        grid=(m // tm,),
        in_specs=[ysp, ysp, pl.BlockSpec((tm, 2 * BW), lambda i: (i, 0)),
                  pl.BlockSpec((1, BW), lambda i: (0, 0)), pl.BlockSpec((1, BW), lambda i: (0, 0))],
        out_specs=ysp,
        out_shape=jax.ShapeDtypeStruct((m, BW), BF16),
        compiler_params=_cparams(("parallel",)),
        name="rwkv_post",
    )(yf, yr, e, row(lnx_g), row(lnx_b))


def _df_kernel(lam_ref, sg_ref, q_ref, k_ref, v_ref, g_ref, o_ref, *, lam_init):
    lam = lam_ref[...]
    e1 = jnp.exp(jnp.sum(lam[0:1] * lam[1:2], -1, keepdims=True))
    e2 = jnp.exp(jnp.sum(lam[2:3] * lam[3:4], -1, keepdims=True))
    lam_full = e1 - e2 + lam_init
    q, k, v = q_ref[...], k_ref[...], v_ref[...]
    lane = lax.broadcasted_iota(jnp.int32, (1, BW), 1)
    d = DF_HEAD_DIM
    acc = jnp.zeros((q.shape[0], BW), F32)
    for h in range(DF_HEADS):
        parts = []
        for i in range(2):
            lo = (2 * h + i) * d
            qm = q * ((lane >= lo) & (lane < lo + d)).astype(BF16)
            s = _dot_nt(qm, k)
            m = jnp.max(s, -1, keepdims=True)
            p = jnp.exp(s - m)
            l = jnp.sum(p, -1, keepdims=True)
            parts.append(jnp.dot(p.astype(BF16), v, preferred_element_type=F32) / l)
        in_head = (lane >= 2 * h * d) & (lane < 2 * (h + 1) * d)
        acc = jnp.where(in_head, parts[0] - lam_full * parts[1], acc)
    ms = _seg_sum(acc * acc, 2 * d) * (1.0 / (2 * d))
    o = acc * lax.rsqrt(ms + DF_EPS) * sg_ref[...] * (1.0 - lam_init)
    o_ref[...] = (o * jax.nn.silu(g_ref[...])).astype(o_ref.dtype)


def _df(cqk, hb, fb, lam, subln_g, lam_init, *, batch, seq, tq=128):
    sg = jnp.tile(subln_g.astype(F32), DF_HEADS).reshape(1, BW)
    return pl.pallas_call(
        functools.partial(_df_kernel, lam_init=lam_init),
        grid=(batch, seq // tq),
        in_specs=[
            pl.BlockSpec((4, DF_HEAD_DIM), lambda b, j: (0, 0)),
            pl.BlockSpec((1, BW), lambda b, j: (0, 0)),
            pl.BlockSpec((None, tq, BW), lambda b, j: (b, j, 0)),
            pl.BlockSpec((None, seq, BW), lambda b, j: (b, 0, 1)),
            pl.BlockSpec((None, seq, BW), lambda b, j: (b, 0, 3)),
            pl.BlockSpec((None, tq, BW), lambda b, j: (b, j, 6)),
        ],
        out_specs=pl.BlockSpec((None, tq, BW), lambda b, j: (b, j, 0)),
        out_shape=jax.ShapeDtypeStruct((batch, seq, BW), BF16),
        compiler_params=_cparams(("parallel", "arbitrary")),
        name="diff_attn",
    )(lam.astype(F32), sg, cqk, cqk, hb, fb)


DL_QB = 2 * DL_HALF


def _dl_group(qkv_ref, qs, kpad, vpad, oacc, lacc, gidx, dil, seq):
    L = seq // dil
    assert L % DL_QB == 0
    nblk = L // DL_QB
    zpad = jnp.zeros((DL_HALF, BW), BF16)
    for ref in (kpad, vpad):
        ref[0:DL_HALF, :] = zpad
        ref[L + DL_HALF:L + 2 * DL_HALF, :] = zpad
    lane = lax.broadcasted_iota(jnp.int32, (1, BW), 1)
    qi = lax.broadcasted_iota(jnp.int32, (DL_QB, 2 * DL_QB), 0)
    ji = lax.broadcasted_iota(jnp.int32, (DL_QB, 2 * DL_QB), 1)
    band = jnp.abs(ji - DL_HALF - qi) <= DL_HALF

    def rows_of(rho):
        return pl.ds(rho, L, stride=dil) if dil > 1 else pl.ds(0, L)

    def rho_body(rho, carry):
        for c in range(BW // LANES):
            cols = slice(c * LANES, (c + 1) * LANES)
            qs[0:L, cols] = qkv_ref[c, rows_of(rho), :].astype(BF16)
            kpad[DL_HALF:DL_HALF + L, cols] = qkv_ref[2 + c, rows_of(rho), :].astype(BF16)
            vpad[DL_HALF:DL_HALF + L, cols] = qkv_ref[4 + c, rows_of(rho), :].astype(BF16)

        def blk_body(n, carry2):
            base = pl.multiple_of(n * DL_QB, DL_QB)
            qb = qs[pl.ds(base, DL_QB), :]
            kw = kpad[pl.ds(base, 2 * DL_QB), :]
            vw = vpad[pl.ds(base, 2 * DL_QB), :]
            kpos = base - DL_HALF + ji
            ok = band & (kpos >= 0) & (kpos < L)
            acc_o = jnp.zeros((DL_QB, BW), F32)
            acc_l = jnp.zeros((DL_QB, BW), F32)
            for h in range(DL_HEADS):
                in_head = (lane >= h * DL_HEAD_DIM) & (lane < (h + 1) * DL_HEAD_DIM)
                s = _dot_nt(qb * in_head.astype(BF16), kw)
                s = jnp.where(ok, s, NEG_INF)
                m = jnp.max(s, -1, keepdims=True)
                p = jnp.exp(s - m)
                den = jnp.sum(p, -1, keepdims=True)
                o = jnp.dot(p.astype(BF16), vw, preferred_element_type=F32) / den
                acc_o = jnp.where(in_head, o, acc_o)
                acc_l = jnp.where(in_head, m + jnp.log(den), acc_l)
            if dil > 1:
                dst = pl.ds(rho + dil * base, DL_QB, stride=dil)
            else:
                dst = pl.ds(base, DL_QB)
            for c in range(BW // LANES):
                oacc[gidx, c, dst, :] = acc_o[:, c * LANES:(c + 1) * LANES]
                lacc[gidx, c, dst, :] = acc_l[:, c * LANES:(c + 1) * LANES]
            return carry2

        lax.fori_loop(0, nblk, blk_body, 0)
        return carry

    lax.fori_loop(0, dil, rho_body, 0)


def _dl_kernel(qkv_ref, g_ref, o_ref, qs, kpad, vpad, oacc, lacc, *, seq):
    gi = pl.program_id(1)
    for gidx, dil in enumerate(DL_DILATIONS):
        @pl.when(gi == gidx)
        def _(gidx=gidx, dil=dil):
            _dl_group(qkv_ref, qs, kpad, vpad, oacc, lacc, gidx, dil, seq)

    @pl.when(gi == len(DL_DILATIONS) - 1)
    def _():
        rc = 256

        def chunk(c, carry):
            rows = pl.ds(pl.multiple_of(c * rc, rc), rc)
            for lc in range(BW // LANES):
                cols = slice(lc * LANES, (lc + 1) * LANES)
                ls = [lacc[g, lc, rows, :] for g in range(len(DL_DILATIONS))]
                mx = functools.reduce(jnp.maximum, ls)
                ws = [jnp.exp(l - mx) for l in ls]
                num = sum(w * oacc[g, lc, rows, :] for g, w in enumerate(ws))
                out = num / sum(ws)
                o_ref[rows, cols] = (out * jax.nn.silu(g_ref[rows, cols])).astype(o_ref.dtype)
            return carry

        lax.fori_loop(0, seq // rc, chunk, 0)


def _dl(dqkv, fb, *, batch, seq):
    ng = len(DL_DILATIONS)
    nch = 3 * BW // LANES
    assert all(w // (2 * d) == DL_HALF for w, d in zip(DL_WINDOWS, DL_DILATIONS))
    return pl.pallas_call(
        functools.partial(_dl_kernel, seq=seq),
        grid=(batch, ng),
        in_specs=[
            pl.BlockSpec((nch, None, seq, LANES), lambda b, g: (g, b, 0, 0)),
            pl.BlockSpec((None, seq, BW), lambda b, g: (b, 0, 7)),
        ],
        out_specs=pl.BlockSpec((None, seq, BW), lambda b, g: (b, 0, 0)),
        out_shape=jax.ShapeDtypeStruct((batch, seq, BW), BF16),
        scratch_shapes=[
            pltpu.VMEM((seq, BW), BF16),
            pltpu.VMEM((seq + 2 * DL_HALF, BW), BF16),
            pltpu.VMEM((seq + 2 * DL_HALF, BW), BF16),
            pltpu.VMEM((ng, BW // LANES, seq, LANES), F32),
            pltpu.VMEM((ng, BW // LANES, seq, LANES), F32),
        ],
        compiler_params=_cparams(("parallel", "arbitrary")),
        name="dilated_attn",
    )(dqkv, fb)


_A0 = 0
_B0 = _A0 + 4 * BW
_C0 = _B0 + 4 * BW + 4 * RW_LORA
_D0 = _C0 + 4 * BW
_DG0 = _D0 + 9 * BW
_G0 = _DG0 + BW
PROJ_TM = 1024


def _layer(xf, xb, l, p, tabs, *, batch, seq, last_split):
    w, b = p["w_in"][l], p["b_in"][l]
    cols = lambda lo, n: (w[:, lo:lo + n], b[lo:lo + n])
    cat = lambda parts: (jnp.concatenate([q[0] for q in parts], 1).astype(BF16),
                         jnp.concatenate([q[1] for q in parts], 0).astype(F32))
    scaled = lambda part, s: (part[0] * s, part[1] * s)
    q_scale = NA_HEAD_DIM ** -0.5
    proj = functools.partial(_proj, xb, seq=seq, tm=PROJ_TM)

    gates = proj(*cat([cols(_G0, 4 * D_MODEL)]), F32, tn=1024, name="proj_gates")
    fb = proj(*cat([cols(_B0, _C0 - _B0), cols(_A0 + 3 * BW, BW), cols(_C0 + 3 * BW, BW),
                    cols(_DG0, BW)]), F32, tn=1024, name="proj_f32")
    hb = proj(*cat([scaled(cols(_A0, BW), q_scale), cols(_A0 + BW, 2 * BW), cols(_C0 + 2 * BW, BW)]),
              BF16, tn=1024, name="proj_bf16")
    cqk = proj(*cat([cols(_C0, 2 * BW)]), BF16, tn=BW, rot=(DF_HEAD_DIM,) + tabs["c"] + (lambda j: j,),
               name="proj_rot_c")
    dparts = []
    for g in range(len(DL_DILATIONS)):
        dparts += [scaled(cols(_D0 + 3 * g * BW, BW), q_scale), cols(_D0 + (3 * g + 1) * BW, 2 * BW)]
    dq = proj(*cat(dparts), F32, tn=BW, chunk_major=True,
              rot=(DL_HEAD_DIM,) + tabs["d"] + (lambda j: (j % 3 == 2).astype(jnp.int32),),
              name="proj_rot_d")

    r3 = lambda z: z.reshape(batch, seq, z.shape[-1])
    hb3, fb3 = r3(hb), r3(fb)
    ya = _na(hb3, fb3, _na_bias_tables(p["na_rpb"][l], seq // GRID_W), batch=batch, seq=seq)
    f, e = _rwprep(fb3, p["rw_mu"][l], p["rw_w0"][l], p["rw_w2"][l], p["rw_a0"][l], p["rw_a2"][l],
                   p["rw_kk"][l], p["rw_ka"][l], p["rw_rk"][l].reshape(-1), batch=batch, seq=seq)
    yf, yr = _rwscan(f, batch=batch, seq=seq)
    m = batch * seq
    yb = _rwpost(yf.reshape(m, BW), yr.reshape(m, BW), e.reshape(m, 2 * BW), p["rw_lnx_g"][l],
                 p["rw_lnx_b"][l])
    lam_init = 0.8 - 0.6 * math.exp(-0.3 * l)
    yc = _df(r3(cqk), hb3, fb3, p["df_lam"][l], p["df_subln_g"][l], lam_init, batch=batch, seq=seq)
    yd = _dl(dq.reshape(dq.shape[0], batch, seq, LANES), fb3, batch=batch, seq=seq)

    margs = (ya.reshape(m, BW), yb, yc.reshape(m, BW), yd.reshape(m, BW), gates, xf,
             p["w_branch"][l].astype(BF16), p["w_out"][l].astype(BF16), p["b_out"][l],
             p["ln_g"][l], p["ln_b"][l])
    if last_split is None:
        return _merge(*margs, row0=0, rows=m)
    return [_merge(*margs, row0=r0, rows=n) for r0, n in last_split]


def kernel(x_prompt, x_sample, ln0_g, ln0_b, w_in, b_in, na_rpb, rw_mu, rw_w0, rw_w2, rw_a0, rw_a2,
           rw_kk, rw_ka, rw_rk, rw_lnx_g, rw_lnx_b, df_lam, df_subln_g, w_branch, w_out, b_out,
           ln_g, ln_b):
    p = dict(w_in=w_in, b_in=b_in, na_rpb=na_rpb, rw_mu=rw_mu, rw_w0=rw_w0, rw_w2=rw_w2,
             rw_a0=rw_a0, rw_a2=rw_a2, rw_kk=rw_kk, rw_ka=rw_ka, rw_rk=rw_rk, rw_lnx_g=rw_lnx_g,
             rw_lnx_b=rw_lnx_b, df_lam=df_lam, df_subln_g=df_subln_g, w_branch=w_branch,
             w_out=w_out, b_out=b_out, ln_g=ln_g, ln_b=ln_b)
    bp, seq, _ = x_prompt.shape
    bs = x_sample.shape[0]
    assert x_sample.shape[1] == seq
    batch = bp + bs
    xf, xb = _ln0(x_prompt.reshape(bp * seq, D_MODEL), x_sample.reshape(bs * seq, D_MODEL),
                  ln0_g, ln0_b)
    tabs = {"c": _rope_tables(seq, DF_HEAD_DIM, (DF_HEAD_DIM ** -0.5, 1.0)),
            "d": _rope_tables(seq, DL_HEAD_DIM, (1.0, None))}
    for l in range(DEPTH - 1):
        xf, xb = _layer(xf, xb, l, p, tabs, batch=batch, seq=seq, last_split=None)
    split = [(0, bp * seq), (bp * seq, bs * seq)]
    (yp, _), (ys, _) = _layer(xf, xb, DEPTH - 1, p, tabs, batch=batch, seq=seq, last_split=split)
    return yp.reshape(bp, seq, D_MODEL), ys.reshape(bs, seq, D_MODEL)
```

```python
import functools
import math

import numpy as np
import jax
import jax.numpy as jnp
from jax import lax
from jax.experimental import pallas as pl
from jax.experimental.pallas import tpu as pltpu

F32 = jnp.float32
BF16 = jnp.bfloat16

D_MODEL = 1024
DEPTH = 2
GRID_W = 64
NA_HEADS, NA_HEAD_DIM, NA_ROWS, NA_COLS = 4, 64, 8, 16
RW_HEADS, RW_HEAD_DIM, RW_LORA = 4, 64, 64
RW_LNX_EPS = 64e-5
DF_HEADS, DF_HEAD_DIM, DF_EPS = 4, 32, 1e-5
DL_HEADS, DL_HEAD_DIM = 4, 64
DL_DILATIONS = (1, 4, 16)
DL_WINDOWS = (128, 512, 2048)
DL_HALF = 64
BW = 256
ROPE_THETA = 10000.0
LN_EPS = 1e-5
DEEPNORM_ALPHA = (2 * DEPTH) ** 0.25
NEG_INF = -1e30

LANES = 128
VMEM_LIMIT = 56 * 1024 * 1024


def _cparams(sem):
    return pltpu.CompilerParams(dimension_semantics=sem, vmem_limit_bytes=VMEM_LIMIT)


def _dot_nt(a, b):
    return lax.dot_general(a, b, (((1,), (1,)), ((), ())), preferred_element_type=F32)


def _split3(x):
    hi = x.astype(BF16)
    r1 = x - hi.astype(F32)
    mid = r1.astype(BF16)
    lo = (r1 - mid.astype(F32)).astype(BF16)
    return hi, mid, lo


def _seg_sum(x, seg):
    n = x.shape[-1]
    r = lax.broadcasted_iota(jnp.int32, (n, n), 0) // seg
    c = lax.broadcasted_iota(jnp.int32, (n, n), 1) // seg
    ones = (r == c).astype(BF16)
    out = None
    for t in _split3(x):
        y = jnp.dot(t, ones, preferred_element_type=F32)
        out = y if out is None else out + y
    return out


def _ln0_kernel(xp_ref, xs_ref, g_ref, b_ref, of_ref, ob_ref, *, n_prompt):
    def norm(x):
        mu = jnp.mean(x, -1, keepdims=True)
        xc = x - mu
        var = jnp.mean(xc * xc, -1, keepdims=True)
        y = xc * lax.rsqrt(var + LN_EPS) * g_ref[...] + b_ref[...]
        of_ref[...] = y
        ob_ref[...] = y.astype(BF16)

    i = pl.program_id(0)

    @pl.when(i < n_prompt)
    def _():
        norm(xp_ref[...])

    @pl.when(i >= n_prompt)
    def _():
        norm(xs_ref[...])


def _ln0(xp, xs, g, b, tm=512):
    mp, ms = xp.shape[0], xs.shape[0]
    n_p, n_s = mp // tm, ms // tm
    out = jax.ShapeDtypeStruct((mp + ms, D_MODEL), F32)
    outb = jax.ShapeDtypeStruct((mp + ms, D_MODEL), BF16)
    return pl.pallas_call(
        functools.partial(_ln0_kernel, n_prompt=n_p),
        grid=(n_p + n_s,),
        in_specs=[
            pl.BlockSpec((tm, D_MODEL), lambda i: (jnp.minimum(i, n_p - 1), 0)),
            pl.BlockSpec((tm, D_MODEL), lambda i: (jnp.maximum(i - n_p, 0), 0)),
            pl.BlockSpec((1, D_MODEL), lambda i: (0, 0)),
            pl.BlockSpec((1, D_MODEL), lambda i: (0, 0)),
        ],
        out_specs=[pl.BlockSpec((tm, D_MODEL), lambda i: (i, 0)),
                   pl.BlockSpec((tm, D_MODEL), lambda i: (i, 0))],
        out_shape=[out, outb],
        compiler_params=_cparams(("parallel",)),
        name="ln0",
    )(xp, xs, g.reshape(1, -1), b.reshape(1, -1))


def _proj_kernel(x_ref, w_ref, b_ref, *rest, rot_hd, chunk_major):
    acc = jnp.dot(x_ref[...], w_ref[...], preferred_element_type=F32) + b_ref[...]
    if not rot_hd:
        (o_ref,) = rest
        o_ref[...] = acc.astype(o_ref.dtype)
        return
    cs_ref, sn_ref, o_ref = rest
    h2 = rot_hd // 2
    lane = lax.broadcasted_iota(jnp.int32, (1, LANES), 1)
    first = (lane % rot_hd) < h2
    cs, sn = cs_ref[...], sn_ref[...]
    for c in range(acc.shape[1] // LANES):
        ch = acc[:, c * LANES:(c + 1) * LANES]
        sw = jnp.where(first, pltpu.roll(ch, LANES - h2, 1), pltpu.roll(ch, h2, 1))
        val = (ch * cs + sw * sn).astype(o_ref.dtype)
        if chunk_major:
            o_ref[c] = val
        else:
            o_ref[:, c * LANES:(c + 1) * LANES] = val


def _proj(xb, w, b, out_dtype, *, seq, tm, tn, rot=None, chunk_major=False, name="proj"):
    m, n = xb.shape[0], w.shape[1]
    assert m % tm == 0 and n % tn == 0 and seq % tm == 0
    in_specs = [
        pl.BlockSpec((tm, D_MODEL), lambda i, j: (i, 0)),
        pl.BlockSpec((D_MODEL, tn), lambda i, j: (0, j)),
        pl.BlockSpec((1, tn), lambda i, j: (0, j)),
    ]
    args = [xb, w, b.reshape(1, -1)]
    rot_hd = 0
    if rot is not None:
        rot_hd, cos, sin, kind_of = rot
        assert tn % LANES == 0 and LANES % rot_hd == 0
        spt = seq // tm
        tab = pl.BlockSpec((None, tm, LANES), lambda i, j: (kind_of(j), i % spt, 0))
        in_specs += [tab, tab]
        args += [cos, sin]
    if chunk_major:
        assert rot is not None
        out_spec = pl.BlockSpec((tn // LANES, tm, LANES), lambda i, j: (j, i, 0))
        out_shape = jax.ShapeDtypeStruct((n // LANES, m, LANES), out_dtype)
    else:
        out_spec = pl.BlockSpec((tm, tn), lambda i, j: (i, j))
        out_shape = jax.ShapeDtypeStruct((m, n), out_dtype)
    return pl.pallas_call(
        functools.partial(_proj_kernel, rot_hd=rot_hd, chunk_major=chunk_major),
        grid=(m // tm, n // tn),
        in_specs=in_specs,
        out_specs=out_spec,
        out_shape=out_shape,
        compiler_params=_cparams(("parallel", "arbitrary")),
        name=name,
    )(*args)


def _rope_tables(seq, head_dim, scales):
    half = head_dim // 2
    inv_freq = jnp.power(ROPE_THETA, -jnp.arange(half, dtype=F32) / half)
    ang = jnp.arange(seq, dtype=F32)[:, None] * inv_freq[None, :]
    lane = np.arange(LANES)
    f_idx = lane % half
    sign = np.where((lane % head_dim) < half, -1.0, 1.0).astype(np.float32)
    cos = jnp.cos(ang)[:, f_idx]
    sin = jnp.sin(ang)[:, f_idx] * sign[None, :]
    cs, sn = [], []
    for s in scales:
        if s is None:
            cs.append(jnp.ones((seq, LANES), F32))
            sn.append(jnp.zeros((seq, LANES), F32))
        else:
            cs.append(cos * F32(s))
            sn.append(sin * F32(s))
    return jnp.stack(cs), jnp.stack(sn)


def _merge_kernel(ya_ref, yb_ref, yc_ref, yd_ref, g_ref, x_ref, wb_ref, wo_ref, bo_ref, lg_ref,
                  lb_ref, of_ref, ob_ref):
    merged = None
    for i, y_ref in enumerate((ya_ref, yb_ref, yc_ref, yd_ref)):
        p = jnp.dot(y_ref[...], wb_ref[i], preferred_element_type=F32)
        gate = jax.nn.sigmoid(g_ref[:, i * D_MODEL:(i + 1) * D_MODEL])
        merged = gate * p if merged is None else merged + gate * p
    y = jnp.dot(merged.astype(BF16), wo_ref[...], preferred_element_type=F32) + bo_ref[...]
    z = DEEPNORM_ALPHA * x_ref[...] + y
    mu = jnp.mean(z, -1, keepdims=True)
    zc = z - mu
    var = jnp.mean(zc * zc, -1, keepdims=True)
    out = zc * lax.rsqrt(var + LN_EPS) * lg_ref[...] + lb_ref[...]
    of_ref[...] = out
    ob_ref[...] = out.astype(BF16)


def _merge(ya, yb, yc, yd, gates, x, wb, wo, bo, lg, lb, *, row0, rows, tm=256):
    assert row0 % tm == 0 and rows % tm == 0
    o = row0 // tm
    ysp = pl.BlockSpec((tm, BW), lambda i: (i + o, 0))
    const2 = lambda i: (0, 0)
    return pl.pallas_call(
        _merge_kernel,
        grid=(rows // tm,),
        in_specs=[ysp, ysp, ysp, ysp,
                  pl.BlockSpec((tm, 4 * D_MODEL), lambda i: (i + o, 0)),
                  pl.BlockSpec((tm, D_MODEL), lambda i: (i + o, 0)),
                  pl.BlockSpec((4, BW, D_MODEL), lambda i: (0, 0, 0)),
                  pl.BlockSpec((D_MODEL, D_MODEL), const2),
                  pl.BlockSpec((1, D_MODEL), const2),
                  pl.BlockSpec((1, D_MODEL), const2),
                  pl.BlockSpec((1, D_MODEL), const2)],
        out_specs=[pl.BlockSpec((tm, D_MODEL), lambda i: (i, 0)),
                   pl.BlockSpec((tm, D_MODEL), lambda i: (i, 0))],
        out_shape=[jax.ShapeDtypeStruct((rows, D_MODEL), F32),
                   jax.ShapeDtypeStruct((rows, D_MODEL), BF16)],
        compiler_params=_cparams(("parallel",)),
        name="merge",
    )(ya, yb, yc, yd, gates, x, wb, wo, bo.reshape(1, -1), lg.reshape(1, -1), lb.reshape(1, -1))


NA_QR = 4


def _na_geometry(rows):
    kr, wr = min(NA_ROWS, rows), min(NA_ROWS, rows) + NA_QR - 1
    assert rows % NA_QR == 0 and rows >= wr and kr == NA_ROWS
    steps = rows // NA_QR
    a = np.arange(NA_QR)[:, None]
    j = np.arange(wr)[None, :]
    pats = []
    for g in range(steps):
        r = NA_QR * g + a
        rs = np.clip(r - kr // 2, 0, rows - kr)
        ws = int(np.clip(NA_QR * g - kr // 2, 0, rows - wr))
        assert rs.min() >= ws and rs.max() + kr <= ws + wr
        valid = (ws + j >= rs) & (ws + j < rs + kr)
        dr = np.clip(ws + j - r + NA_ROWS - 1, 0, 2 * NA_ROWS - 2)
        pats.append((valid, np.where(valid, dr, 0)))
    same = lambda x, y: np.array_equal(x[0], y[0]) and np.array_equal(x[1], y[1])
    assert steps >= 3 and all(same(pats[g], pats[1]) for g in range(1, steps - 1))
    return kr, wr, steps, [pats[0], pats[1], pats[steps - 1]]


def _na_bias_tables(rpb, rows):
    kr, wr, steps, pats = _na_geometry(rows)
    qc = np.arange(GRID_W)[:, None]
    kc = np.arange(GRID_W)[None, :]
    c_start = np.clip(qc - NA_COLS // 2, 0, GRID_W - NA_COLS)
    col_ok = (kc >= c_start) & (kc < c_start + NA_COLS)
    dc = np.clip(kc - qc + NA_COLS - 1, 0, 2 * NA_COLS - 2)
    tabs = []
    for valid, dr in pats:
        b = rpb.astype(F32)[:, dr][:, :, :, dc]
        ok = valid[:, :, None, None] & col_ok[None, None]
        b = jnp.where(ok[None], b, NEG_INF)
        tabs.append(b.transpose(0, 1, 3, 2, 4).reshape(NA_HEADS, NA_QR * GRID_W, wr * GRID_W))
    return jnp.stack(tabs, axis=1)


def _stack_heads(q, n_heads, head_dim):
    lane = lax.broadcasted_iota(jnp.int32, (1, q.shape[1]), 1)
    zero = jnp.zeros_like(q)
    return jnp.concatenate(
        [jnp.where((lane >= h * head_dim) & (lane < (h + 1) * head_dim), q, zero)
         for h in range(n_heads)], axis=0)


def _unstack_heads(o, n_heads, head_dim):
    m = o.shape[0] // n_heads
    lane = lax.broadcasted_iota(jnp.int32, (1, o.shape[1]), 1)
    acc = o[0:m]
    for h in range(1, n_heads):
        in_head = (lane >= h * head_dim) & (lane < (h + 1) * head_dim)
        acc = jnp.where(in_head, o[h * m:(h + 1) * m], acc)
    return acc


def _na_kernel(q_ref, k_ref, v_ref, g_ref, bias_ref, o_ref, *, rows):
    kr, wr = min(NA_ROWS, rows), min(NA_ROWS, rows) + NA_QR - 1
    g = pl.program_id(1)
    ws = jnp.clip(NA_QR * g - kr // 2, 0, rows - wr)
    start = pl.multiple_of(ws * GRID_W, GRID_W)
    k = k_ref[pl.ds(start, wr * GRID_W), :]
    v = v_ref[pl.ds(start, wr * GRID_W), :]
    s = _dot_nt(_stack_heads(q_ref[...], NA_HEADS, NA_HEAD_DIM), k)
    bias = bias_ref[...].reshape(s.shape)
    s = jnp.where(bias > 0.5 * NEG_INF, s + bias, NEG_INF)
    m = jnp.max(s, -1, keepdims=True)
    p = jnp.exp(s - m)
    l = jnp.sum(p, -1, keepdims=True)
    o = jnp.dot(p.astype(BF16), v, preferred_element_type=F32) / l
    o = _unstack_heads(o, NA_HEADS, NA_HEAD_DIM)
    o_ref[...] = (o * jax.nn.silu(g_ref[...])).astype(o_ref.dtype)


def _na(hb, fb, bias_tab, *, batch, seq):
    rows = seq // GRID_W
    kr, wr, steps, _ = _na_geometry(rows)
    qb = NA_QR * GRID_W

    def pattern_of(g):
        return jnp.where(g == 0, 0, jnp.where(g == steps - 1, 2, 1))

    return pl.pallas_call(
        functools.partial(_na_kernel, rows=rows),
        grid=(batch, steps),
        in_specs=[
            pl.BlockSpec((None, qb, BW), lambda b, g: (b, g, 0)),
            pl.BlockSpec((None, seq, BW), lambda b, g: (b, 0, 1)),
            pl.BlockSpec((None, seq, BW), lambda b, g: (b, 0, 2)),
            pl.BlockSpec((None, qb, BW), lambda b, g: (b, g, 5)),
            pl.BlockSpec((NA_HEADS, None, qb, wr * GRID_W), lambda b, g: (0, pattern_of(g), 0, 0)),
        ],
        out_specs=pl.BlockSpec((None, qb, BW), lambda b, g: (b, g, 0)),
        out_shape=jax.ShapeDtypeStruct((batch, seq, BW), BF16),
        compiler_params=_cparams(("parallel", "arbitrary")),
        name="na_attn",
    )(hb, hb, hb, fb, bias_tab)


RW_C = 64
RW_FIELDS = 9


def _mm(a, b, mode="nn", passes=1):
    dims = {"nn": (((1,), (0,)), ((), ())), "nt": (((1,), (1,)), ((), ()))}[mode]
    dg = lambda x, y: lax.dot_general(x, y, dims, preferred_element_type=F32)
    ah, bh = a.astype(BF16), b.astype(BF16)
    out = dg(ah, bh)
    if passes == 3:
        al = (a - ah.astype(F32)).astype(BF16)
        bl = (b - bh.astype(F32)).astype(BF16)
        out = out + (dg(al, bh) + dg(ah, bl))
    return out


def _softplus(x):
    return jnp.maximum(x, 0.0) + jnp.log(1.0 + jnp.exp(-jnp.abs(x)))


def _rwprep_kernel(x_ref, prev_ref, next_ref, mu_ref, w0_ref, w2_ref, a0_ref, a2_ref, kk_ref, ka_ref,
                   rk_ref, f_ref, e_ref, *, rc):
    c = pl.program_id(1)
    lane = lax.broadcasted_iota(jnp.int32, (1, 2 * RW_LORA), 1)
    ri = lax.broadcasted_iota(jnp.int32, (rc, 1), 0)
    x = x_ref[...]
    prev_row = jnp.where(c > 0, prev_ref[7:8, :], 0.0)
    next_row = jnp.where(c < pl.num_programs(1) - 1, next_ref[0:1, :], 0.0)
    up = jnp.where(ri == 0, prev_row, pltpu.roll(x, 1, 0))
    dn = jnp.where(ri == rc - 1, next_row, pltpu.roll(x, rc - 1, 0))
    xs = x + mu_ref[...] * (0.5 * (up + dn) - x)
    r, k, v, g = (xs[:, i * BW:(i + 1) * BW] for i in range(4))
    wl = xs[:, 4 * BW:4 * BW + 2 * RW_LORA]
    al = xs[:, 4 * BW + 2 * RW_LORA:4 * BW + 4 * RW_LORA]
    kk = k * kk_ref[...]
    kap = kk * lax.rsqrt(jnp.maximum(_seg_sum(kk * kk, RW_HEAD_DIM), 1e-24))
    tw = jnp.tanh(wl)
    kd_sum = None
    for d in range(2):
        dm = ((lane >= d * RW_LORA) & (lane < (d + 1) * RW_LORA)).astype(F32)
        w_raw = w0_ref[d:d + 1, :] + _mm(tw * dm, w2_ref[...], passes=3)
        lw = -jnp.exp(-_softplus(-w_raw) - 0.5)
        a = jax.nn.sigmoid(a0_ref[d:d + 1, :] + _mm(al * dm, a2_ref[...], passes=3))
        kd = k * (1.0 + (a - 1.0) * ka_ref[...])
        f_ref[:, (3 + 3 * d) * BW:(4 + 3 * d) * BW] = lw
        f_ref[:, (4 + 3 * d) * BW:(5 + 3 * d) * BW] = kd
        f_ref[:, (5 + 3 * d) * BW:(6 + 3 * d) * BW] = kap * a
        kd_sum = kd if kd_sum is None else kd_sum + kd
    f_ref[:, 0:BW] = r
    f_ref[:, BW:2 * BW] = v
    f_ref[:, 2 * BW:3 * BW] = kap
    bonus = _seg_sum(r * kd_sum * rk_ref[...], RW_HEAD_DIM)
    e_ref[:, 0:BW] = bonus * v
    e_ref[:, BW:2 * BW] = jax.nn.silu(g)


def _rwprep(fb, mu, w0, w2, a0, a2, k_k, k_a, r_k, *, batch, seq, rc=256):
    ucols = 4 * BW + 4 * RW_LORA
    row = lambda z: z.astype(F32).reshape(1, -1)
    c2 = lambda b, c: (0, 0)
    tpb = rc // 8
    return pl.pallas_call(
        functools.partial(_rwprep_kernel, rc=rc),
        grid=(batch, seq // rc),
        in_specs=[
            pl.BlockSpec((None, rc, ucols), lambda b, c: (b, c, 0)),
            pl.BlockSpec((None, 8, ucols), lambda b, c: (b, jnp.maximum(c * tpb - 1, 0), 0)),
            pl.BlockSpec((None, 8, ucols),
                         lambda b, c: (b, jnp.minimum((c + 1) * tpb, seq // 8 - 1), 0)),
            pl.BlockSpec((1, ucols), c2),
            pl.BlockSpec((2, BW), c2),
            pl.BlockSpec((2 * RW_LORA, BW), c2),
            pl.BlockSpec((2, BW), c2),
            pl.BlockSpec((2 * RW_LORA, BW), c2),
            pl.BlockSpec((1, BW), c2),
            pl.BlockSpec((1, BW), c2),
            pl.BlockSpec((1, BW), c2),
        ],
        out_specs=[pl.BlockSpec((None, rc, RW_FIELDS * BW), lambda b, c: (b, c, 0)),
                   pl.BlockSpec((None, rc, 2 * BW), lambda b, c: (b, c, 0))],
        out_shape=[jax.ShapeDtypeStruct((batch, seq, RW_FIELDS * BW), F32),
                   jax.ShapeDtypeStruct((batch, seq, 2 * BW), F32)],
        compiler_params=_cparams(("parallel", "arbitrary")),
        name="rwkv_prep",
    )(fb, fb, fb, row(mu), w0.astype(F32), w2.astype(F32).reshape(2 * RW_LORA, BW), a0.astype(F32),
      a2.astype(F32).reshape(2 * RW_LORA, BW), row(k_k), row(k_a), row(r_k))


def _rw_bd(x):
    lane = lax.broadcasted_iota(jnp.int32, (1, BW), 1)
    xb = x.astype(BF16)
    zero = jnp.zeros_like(xb)
    return jnp.concatenate(
        [jnp.where((lane >= h * RW_C) & (lane < (h + 1) * RW_C), xb, zero) for h in range(RW_HEADS)],
        axis=0)


def _rw_advance(terms, z):
    gm, hv, nt = terms
    both = _mm(gm, z)
    return both[0:RW_C] + hv, _rw_bd(both[RW_C:2 * RW_C] + nt)


def _rw_prepare(specs):
    C, H = RW_C, RW_HEADS
    assert H * C == BW and RW_HEAD_DIM == C
    bd = _rw_bd
    cat = lambda a, b: jnp.concatenate([a, b], axis=0)
    ri = lax.broadcasted_iota(jnp.int32, (C, C), 0)
    ci = lax.broadcasted_iota(jnp.int32, (C, C), 1)
    tris = [(ci <= ri).astype(BF16), (ci >= ri).astype(BF16)]
    row = lax.broadcasted_iota(jnp.int32, (C, BW), 0)
    col = lax.broadcasted_iota(jnp.int32, (C, BW), 1) % C
    eye = col == row
    strict = [col < row, col > row]
    incl = [col <= row, col >= row]

    def lc_t(x):
        t = bd(x).astype(F32).T
        return t[0:C] + t[C:2 * C] + t[2 * C:3 * C] + t[3 * C:4 * C]

    def load(spec):
        f_ref, r0, d = spec
        fld = lambda i: f_ref[r0:r0 + C, i * BW:(i + 1) * BW]
        c = dict(d=d, r=fld(0), v=fld(1), kap=fld(2), lw=fld(3 + 3 * d), kd=fld(4 + 3 * d),
                 beta=fld(5 + 3 * d))
        cs = None
        for t in _split3(c["lw"]):
            y = jnp.dot(tris[d], t, preferred_element_type=F32)
            cs = y if cs is None else cs + y
        c["cs"] = cs
        return c

    def scale(c):
        cs, d = c["cs"], c["d"]
        tot = cs[0:1] if d == 1 else cs[C - 1:C]
        ginv = jnp.exp(-cs)
        gto = jnp.exp(tot - cs)
        kt = c["kap"] * jnp.exp(cs - c["lw"])
        rt = c["r"] * jnp.exp(cs)
        c.update(gc=jnp.exp(tot), rt=rt, ktrt=cat(kt, rt), kt_b=bd(kt), v_b=bd(c["v"]),
                 kg_b=bd(c["kd"] * ginv), bg_b=bd(c["beta"] * ginv),
                 kh=c["kd"] * gto, bh=c["beta"] * gto)
        return c

    def gram(c):
        d = c["d"]
        gk = _mm(c["ktrt"], c["kg_b"], "nt")
        gb = _mm(c["ktrt"], c["bg_b"], "nt")
        c.update(l_kk=jnp.where(strict[d], gk[0:C], 0.0), l_kr=jnp.where(incl[d], gk[C:2 * C], 0.0),
                 l_br=jnp.where(incl[d], gb[C:2 * C], 0.0),
                 npow=-jnp.where(strict[d], gb[0:C], 0.0), w=jnp.where(eye, 1.0, 0.0))
        return c

    def level(c, last):
        if last:
            c["w"] = c["w"] + _mm(c["w"], bd(c["npow"]))
        else:
            both = _mm(cat(c["npow"], c["w"]), bd(c["npow"]))
            c["npow"], c["w"] = both[0:C], c["w"] + both[C:2 * C]
        return c

    def solve(c):
        c["p1_b"] = bd(_mm(c["w"], c["kt_b"]))
        c["p2"] = _mm(c["w"], bd(c["l_kk"]))
        c["p2_b"] = bd(c["p2"])
        return c

    def outputs(c):
        c["g"] = c["rt"] - _mm(c["l_br"], c["p1_b"])
        hm = c["l_kr"] - _mm(c["l_br"], c["p2_b"])
        both = _mm(cat(hm, c["p2"]), c["v_b"])
        c["hv"], c["p2v_b"] = both[0:C], bd(both[C:2 * C])
        return c

    def state(c):
        bh_t = lc_t(c["bh"])
        mt = jnp.where(eye, c["gc"], 0.0) - _mm(bh_t, c["p1_b"])
        nt = _mm(lc_t(c["kh"]), c["v_b"]) - _mm(bh_t, c["p2v_b"])
        return cat(c["g"], mt).astype(BF16), c["hv"], nt

    cs = [load(s) for s in specs]
    cs = [scale(c) for c in cs]
    cs = [gram(c) for c in cs]
    levels = int(math.log2(C))
    for lvl in range(levels):
        cs = [level(c, lvl == levels - 1) for c in cs]
    cs = [solve(c) for c in cs]
    cs = [outputs(c) for c in cs]
    return [state(c) for c in cs]


def _rwscan_kernel(ff_ref, fr_ref, yf_ref, yr_ref, zf_ref, zr_ref, *, nch):
    @pl.when(pl.program_id(1) == 0)
    def _():
        zf_ref[...] = jnp.zeros_like(zf_ref)
        zr_ref[...] = jnp.zeros_like(zr_ref)

    C = RW_C
    pre = _rw_prepare([(ff_ref, j * C, 0) for j in range(nch)]
                      + [(fr_ref, j * C, 1) for j in range(nch)])
    pre_f, pre_r = pre[:nch], pre[nch:]
    zf, zr = zf_ref[...], zr_ref[...]
    for j in range(nch):
        yf_ref[j * C:(j + 1) * C, :], zf = _rw_advance(pre_f[j], zf)
        jr = nch - 1 - j
        yr_ref[jr * C:(jr + 1) * C, :], zr = _rw_advance(pre_r[jr], zr)
    zf_ref[...] = zf
    zr_ref[...] = zr


def _rwscan(f, *, batch, seq, nch=4):
    rb = nch * RW_C
    nc = seq // rb
    fsp = lambda imap: pl.BlockSpec((None, rb, RW_FIELDS * BW), imap)
    ysp = lambda imap: pl.BlockSpec((None, rb, BW), imap)
    fwd = lambda b, s: (b, s, 0)
    rev = lambda b, s: (b, nc - 1 - s, 0)
    y = jax.ShapeDtypeStruct((batch, seq, BW), F32)
    return pl.pallas_call(
        functools.partial(_rwscan_kernel, nch=nch),
        grid=(batch, nc),
        in_specs=[fsp(fwd), fsp(rev)],
        out_specs=[ysp(fwd), ysp(rev)],
        out_shape=[y, y],
        scratch_shapes=[pltpu.VMEM((BW, BW), BF16), pltpu.VMEM((BW, BW), BF16)],
        compiler_params=_cparams(("parallel", "arbitrary")),
        name="rwkv_scan",
    )(f, f)


def _rwpost_kernel(yf_ref, yr_ref, e_ref, lg_ref, lb_ref, o_ref):
    y = yf_ref[...] + yr_ref[...]
    inv = 1.0 / RW_HEAD_DIM
    yc = y - _seg_sum(y, RW_HEAD_DIM) * inv
    var = _seg_sum(yc * yc, RW_HEAD_DIM) * inv
    yn = yc * lax.rsqrt(var + RW_LNX_EPS) * lg_ref[...] + lb_ref[...]
    o_ref[...] = ((yn + e_ref[:, 0:BW]) * e_ref[:, BW:2 * BW]).astype(o_ref.dtype)


def _rwpost(yf, yr, e, lnx_g, lnx_b, *, tm=512):
    m = yf.shape[0]
    row = lambda z: z.astype(F32).reshape(1, -1)
    ysp = pl.BlockSpec((tm, BW), lambda i: (i, 0))
    return pl.pallas_call(
        _rwpost_kernel,
        grid=(m // tm,),
        in_specs=[ysp, ysp, pl.BlockSpec((tm, 2 * BW), lambda i: (i, 0)),
                  pl.BlockSpec((1, BW), lambda i: (0, 0)), pl.BlockSpec((1, BW), lambda i: (0, 0))],
        out_specs=ysp,
        out_shape=jax.ShapeDtypeStruct((m, BW), BF16),
        compiler_params=_cparams(("parallel",)),
        name="rwkv_post",
    )(yf, yr, e, row(lnx_g), row(lnx_b))


def _df_kernel(lam_ref, sg_ref, q_ref, k_ref, v_ref, g_ref, o_ref, *, lam_init):
    lam = lam_ref[...]
    e1 = jnp.exp(jnp.sum(lam[0:1] * lam[1:2], -1, keepdims=True))
    e2 = jnp.exp(jnp.sum(lam[2:3] * lam[3:4], -1, keepdims=True))
    lam_full = e1 - e2 + lam_init
    d = DF_HEAD_DIM
    tq = q_ref.shape[0]
    qs = _stack_heads(q_ref[...], 2 * DF_HEADS, d)
    k, v = k_ref[...], v_ref[...]
    ss = [_dot_nt(qs[2 * h * tq:2 * (h + 1) * tq], k) for h in range(DF_HEADS)]
    ps, ls = [], []
    for s in ss:
        p = jnp.exp(s - jnp.max(s, -1, keepdims=True))
        ls.append(jnp.sum(p, -1, keepdims=True))
        ps.append(p.astype(BF16))
    os_ = [jnp.dot(p, v, preferred_element_type=F32) / l for p, l in zip(ps, ls)]
    lane = lax.broadcasted_iota(jnp.int32, (1, BW), 1)
    acc = None
    for h, o in enumerate(os_):
        comb = o[0:tq] - lam_full * o[tq:2 * tq]
        in_head = (lane >= 2 * h * d) & (lane < 2 * (h + 1) * d)
        acc = comb if acc is None else jnp.where(in_head, comb, acc)
    ms = _seg_sum(acc * acc, 2 * d) * (1.0 / (2 * d))
    o = acc * lax.rsqrt(ms + DF_EPS) * sg_ref[...] * (1.0 - lam_init)
    o_ref[...] = (o * jax.nn.silu(g_ref[...])).astype(o_ref.dtype)


def _df(cqk, hb, fb, lam, subln_g, lam_init, *, batch, seq, tq=256):
    sg = jnp.tile(subln_g.astype(F32), DF_HEADS).reshape(1, BW)
    return pl.pallas_call(
        functools.partial(_df_kernel, lam_init=lam_init),
        grid=(batch, seq // tq),
        in_specs=[
            pl.BlockSpec((4, DF_HEAD_DIM), lambda b, j: (0, 0)),
            pl.BlockSpec((1, BW), lambda b, j: (0, 0)),
            pl.BlockSpec((None, tq, BW), lambda b, j: (b, j, 0)),
            pl.BlockSpec((None, seq, BW), lambda b, j: (b, 0, 1)),
            pl.BlockSpec((None, seq, BW), lambda b, j: (b, 0, 3)),
            pl.BlockSpec((None, tq, BW), lambda b, j: (b, j, 6)),
        ],
        out_specs=pl.BlockSpec((None, tq, BW), lambda b, j: (b, j, 0)),
        out_shape=jax.ShapeDtypeStruct((batch, seq, BW), BF16),
        compiler_params=_cparams(("parallel", "arbitrary")),
        name="diff_attn",
    )(lam.astype(F32), sg, cqk, cqk, hb, fb)


DL_QB = 2 * DL_HALF
DL_UNROLL = 2


def _dl_group(qkv_ref, qs, kpad, vpad, oacc, lacc, gidx, dil, seq):
    L = seq // dil
    assert L % DL_QB == 0 and (seq // DL_QB) % DL_UNROLL == 0
    nblk = L // DL_QB
    seg = L + 2 * DL_HALF
    zpad = jnp.zeros((DL_HALF, BW), BF16)

    def rows_of(rho):
        return pl.ds(rho, L, stride=dil) if dil > 1 else pl.ds(0, L)

    def stage(rho, carry):
        qrow = pl.multiple_of(rho * L, DL_QB)
        krow = pl.multiple_of(rho * seg, DL_HALF)
        for ref in (kpad, vpad):
            ref[pl.ds(krow, DL_HALF), :] = zpad
            ref[pl.ds(krow + DL_HALF + L, DL_HALF), :] = zpad
        for c in range(BW // LANES):
            cols = slice(c * LANES, (c + 1) * LANES)
            qs[pl.ds(qrow, L), cols] = qkv_ref[c, rows_of(rho), :].astype(BF16)
            kpad[pl.ds(krow + DL_HALF, L), cols] = qkv_ref[2 + c, rows_of(rho), :].astype(BF16)
            vpad[pl.ds(krow + DL_HALF, L), cols] = qkv_ref[4 + c, rows_of(rho), :].astype(BF16)
        return carry

    lax.fori_loop(0, dil, stage, 0)

    hq = DL_HEADS * DL_QB
    qi = lax.broadcasted_iota(jnp.int32, (hq, 2 * DL_QB), 0) % DL_QB
    ji = lax.broadcasted_iota(jnp.int32, (hq, 2 * DL_QB), 1)
    band = jnp.abs(ji - DL_HALF - qi) <= DL_HALF
    lane = lax.broadcasted_iota(jnp.int32, (1, BW), 1)

    def blocks(it, carry):
        fs = [it * DL_UNROLL + u for u in range(DL_UNROLL)]
        rhos = [f // nblk for f in fs]
        ns = [f % nblk for f in fs]
        qst = [_stack_heads(qs[pl.ds(pl.multiple_of(f * DL_QB, DL_QB), DL_QB), :], DL_HEADS,
                            DL_HEAD_DIM) for f in fs]
        krows = [pl.multiple_of(rho * seg + n * DL_QB, DL_HALF) for rho, n in zip(rhos, ns)]
        ss = [_dot_nt(q, kpad[pl.ds(kr, 2 * DL_QB), :]) for q, kr in zip(qst, krows)]
        ps, lses, dens = [], [], []
        for s, n in zip(ss, ns):
            kpos = n * DL_QB - DL_HALF + ji
            s = jnp.where(band & (kpos >= 0) & (kpos < L), s, NEG_INF)
            m = jnp.max(s, -1, keepdims=True)
            p = jnp.exp(s - m)
            den = jnp.sum(p, -1, keepdims=True)
            ps.append(p.astype(BF16))
            dens.append(den)
            lses.append(m + jnp.log(den))
        os_ = [jnp.dot(p, vpad[pl.ds(kr, 2 * DL_QB), :], preferred_element_type=F32) / den
               for p, kr, den in zip(ps, krows, dens)]
        for o, lse, rho, n in zip(os_, lses, rhos, ns):
            acc_o = _unstack_heads(o, DL_HEADS, DL_HEAD_DIM)
            acc_l = _unstack_heads(jnp.broadcast_to(lse, (hq, BW)), DL_HEADS, DL_HEAD_DIM)
            if dil > 1:
                dst = pl.ds(rho + dil * n * DL_QB, DL_QB, stride=dil)
            else:
                dst = pl.ds(pl.multiple_of(n * DL_QB, DL_QB), DL_QB)
            for c in range(BW // LANES):
                oacc[gidx, c, dst, :] = acc_o[:, c * LANES:(c + 1) * LANES]
                lacc[gidx, c, dst, :] = acc_l[:, c * LANES:(c + 1) * LANES]
        return carry

    lax.fori_loop(0, seq // DL_QB // DL_UNROLL, blocks, 0)


def _dl_kernel(qkv_ref, g_ref, o_ref, qs, kpad, vpad, oacc, lacc, *, seq):
    gi = pl.program_id(1)
    for gidx, dil in enumerate(DL_DILATIONS):
        @pl.when(gi == gidx)
        def _(gidx=gidx, dil=dil):
            _dl_group(qkv_ref, qs, kpad, vpad, oacc, lacc, gidx, dil, seq)

    @pl.when(gi == len(DL_DILATIONS) - 1)
    def _():
        rc = 256

        def chunk(c, carry):
            rows = pl.ds(pl.multiple_of(c * rc, rc), rc)
            for lc in range(BW // LANES):
                cols = slice(lc * LANES, (lc + 1) * LANES)
                ls = [lacc[g, lc, rows, :] for g in range(len(DL_DILATIONS))]
                mx = functools.reduce(jnp.maximum, ls)
                ws = [jnp.exp(l - mx) for l in ls]
                num = sum(w * oacc[g, lc, rows, :] for g, w in enumerate(ws))
                out = num / sum(ws)
                o_ref[rows, cols] = (out * jax.nn.silu(g_ref[rows, cols])).astype(o_ref.dtype)
            return carry

        lax.fori_loop(0, seq // rc, chunk, 0)


def _dl(dqkv, fb, *, batch, seq):
    ng = len(DL_DILATIONS)
    nch = 3 * BW // LANES
    assert all(w // (2 * d) == DL_HALF for w, d in zip(DL_WINDOWS, DL_DILATIONS))
    return pl.pallas_call(
        functools.partial(_dl_kernel, seq=seq),
        grid=(batch, ng),
        in_specs=[
            pl.BlockSpec((nch, None, seq, LANES), lambda b, g: (g, b, 0, 0)),
            pl.BlockSpec((None, seq, BW), lambda b, g: (b, 0, 7)),
        ],
        out_specs=pl.BlockSpec((None, seq, BW), lambda b, g: (b, 0, 0)),
        out_shape=jax.ShapeDtypeStruct((batch, seq, BW), BF16),
        scratch_shapes=[
            pltpu.VMEM((seq, BW), BF16),
            pltpu.VMEM((seq + 2 * DL_HALF * max(DL_DILATIONS), BW), BF16),
            pltpu.VMEM((seq + 2 * DL_HALF * max(DL_DILATIONS), BW), BF16),
            pltpu.VMEM((ng, BW // LANES, seq, LANES), F32),
            pltpu.VMEM((ng, BW // LANES, seq, LANES), F32),
        ],
        compiler_params=_cparams(("parallel", "arbitrary")),
        name="dilated_attn",
    )(dqkv, fb)


_A0 = 0
_B0 = _A0 + 4 * BW
_C0 = _B0 + 4 * BW + 4 * RW_LORA
_D0 = _C0 + 4 * BW
_DG0 = _D0 + 9 * BW
_G0 = _DG0 + BW
PROJ_TM = 1024


def _layer(xf, xb, l, p, tabs, *, batch, seq, last_split):
    w, b = p["w_in"][l], p["b_in"][l]
    cols = lambda lo, n: (w[:, lo:lo + n], b[lo:lo + n])
    cat = lambda parts: (jnp.concatenate([q[0] for q in parts], 1).astype(BF16),
                         jnp.concatenate([q[1] for q in parts], 0).astype(F32))
    scaled = lambda part, s: (part[0] * s, part[1] * s)
    q_scale = NA_HEAD_DIM ** -0.5
    proj = functools.partial(_proj, xb, seq=seq, tm=PROJ_TM)

    gates = proj(*cat([cols(_G0, 4 * D_MODEL)]), F32, tn=1024, name="proj_gates")
    fb = proj(*cat([cols(_B0, _C0 - _B0), cols(_A0 + 3 * BW, BW), cols(_C0 + 3 * BW, BW),
                    cols(_DG0, BW)]), F32, tn=1024, name="proj_f32")
    hb = proj(*cat([scaled(cols(_A0, BW), q_scale), cols(_A0 + BW, 2 * BW), cols(_C0 + 2 * BW, BW)]),
              BF16, tn=1024, name="proj_bf16")
    cqk = proj(*cat([cols(_C0, 2 * BW)]), BF16, tn=BW, rot=(DF_HEAD_DIM,) + tabs["c"] + (lambda j: j,),
               name="proj_rot_c")
    dparts = []
    for g in range(len(DL_DILATIONS)):
        dparts += [scaled(cols(_D0 + 3 * g * BW, BW), q_scale), cols(_D0 + (3 * g + 1) * BW, 2 * BW)]
    dq = proj(*cat(dparts), F32, tn=BW, chunk_major=True,
              rot=(DL_HEAD_DIM,) + tabs["d"] + (lambda j: (j % 3 == 2).astype(jnp.int32),),
              name="proj_rot_d")

    r3 = lambda z: z.reshape(batch, seq, z.shape[-1])
    hb3, fb3 = r3(hb), r3(fb)
    ya = _na(hb3, fb3, _na_bias_tables(p["na_rpb"][l], seq // GRID_W), batch=batch, seq=seq)
    f, e = _rwprep(fb3, p["rw_mu"][l], p["rw_w0"][l], p["rw_w2"][l], p["rw_a0"][l], p["rw_a2"][l],
                   p["rw_kk"][l], p["rw_ka"][l], p["rw_rk"][l].reshape(-1), batch=batch, seq=seq)
    yf, yr = _rwscan(f, batch=batch, seq=seq)
    m = batch * seq
    yb = _rwpost(yf.reshape(m, BW), yr.reshape(m, BW), e.reshape(m, 2 * BW), p["rw_lnx_g"][l],
                 p["rw_lnx_b"][l])
    lam_init = 0.8 - 0.6 * math.exp(-0.3 * l)
    yc = _df(r3(cqk), hb3, fb3, p["df_lam"][l], p["df_subln_g"][l], lam_init, batch=batch, seq=seq)
    yd = _dl(dq.reshape(dq.shape[0], batch, seq, LANES), fb3, batch=batch, seq=seq)

    margs = (ya.reshape(m, BW), yb, yc.reshape(m, BW), yd.reshape(m, BW), gates, xf,
             p["w_branch"][l].astype(BF16), p["w_out"][l].astype(BF16), p["b_out"][l],
             p["ln_g"][l], p["ln_b"][l])
    if last_split is None:
        return _merge(*margs, row0=0, rows=m)
    return [_merge(*margs, row0=r0, rows=n) for r0, n in last_split]


def kernel(x_prompt, x_sample, ln0_g, ln0_b, w_in, b_in, na_rpb, rw_mu, rw_w0, rw_w2, rw_a0, rw_a2,
           rw_kk, rw_ka, rw_rk, rw_lnx_g, rw_lnx_b, df_lam, df_subln_g, w_branch, w_out, b_out,
           ln_g, ln_b):
    p = dict(w_in=w_in, b_in=b_in, na_rpb=na_rpb, rw_mu=rw_mu, rw_w0=rw_w0, rw_w2=rw_w2,
             rw_a0=rw_a0, rw_a2=rw_a2, rw_kk=rw_kk, rw_ka=rw_ka, rw_rk=rw_rk, rw_lnx_g=rw_lnx_g,
             rw_lnx_b=rw_lnx_b, df_lam=df_lam, df_subln_g=df_subln_g, w_branch=w_branch,
             w_out=w_out, b_out=b_out, ln_g=ln_g, ln_b=ln_b)
    bp, seq, _ = x_prompt.shape
    bs = x_sample.shape[0]
    assert x_sample.shape[1] == seq
    batch = bp + bs
    xf, xb = _ln0(x_prompt.reshape(bp * seq, D_MODEL), x_sample.reshape(bs * seq, D_MODEL),
                  ln0_g, ln0_b)
    tabs = {"c": _rope_tables(seq, DF_HEAD_DIM, (DF_HEAD_DIM ** -0.5, 1.0)),
            "d": _rope_tables(seq, DL_HEAD_DIM, (1.0, None))}
    for l in range(DEPTH - 1):
        xf, xb = _layer(xf, xb, l, p, tabs, batch=batch, seq=seq, last_split=None)
    split = [(0, bp * seq), (bp * seq, bs * seq)]
    (yp, _), (ys, _) = _layer(xf, xb, DEPTH - 1, p, tabs, batch=batch, seq=seq, last_split=split)
    return yp.reshape(bp, seq, D_MODEL), ys.reshape(bs, seq, D_MODEL)
```

```python
import functools
import math

import numpy as np
import jax
import jax.numpy as jnp
from jax import lax
from jax.experimental import pallas as pl
from jax.experimental.pallas import tpu as pltpu

F32 = jnp.float32
BF16 = jnp.bfloat16

D_MODEL = 1024
DEPTH = 2
GRID_W = 64
NA_HEADS, NA_HEAD_DIM, NA_ROWS, NA_COLS = 4, 64, 8, 16
RW_HEADS, RW_HEAD_DIM, RW_LORA = 4, 64, 64
RW_LNX_EPS = 64e-5
DF_HEADS, DF_HEAD_DIM, DF_EPS = 4, 32, 1e-5
DL_HEADS, DL_HEAD_DIM = 4, 64
DL_DILATIONS = (1, 4, 16)
DL_WINDOWS = (128, 512, 2048)
DL_HALF = 64
BW = 256
ROPE_THETA = 10000.0
LN_EPS = 1e-5
DEEPNORM_ALPHA = (2 * DEPTH) ** 0.25
NEG_INF = -1e30

LANES = 128
VMEM_LIMIT = 56 * 1024 * 1024


def _cparams(sem):
    return pltpu.CompilerParams(dimension_semantics=sem, vmem_limit_bytes=VMEM_LIMIT)


def _dot_nt(a, b):
    return lax.dot_general(a, b, (((1,), (1,)), ((), ())), preferred_element_type=F32)


def _split3(x):
    hi = x.astype(BF16)
    r1 = x - hi.astype(F32)
    mid = r1.astype(BF16)
    lo = (r1 - mid.astype(F32)).astype(BF16)
    return hi, mid, lo


def _seg_sum(x, seg):
    n = x.shape[-1]
    r = lax.broadcasted_iota(jnp.int32, (n, n), 0) // seg
    c = lax.broadcasted_iota(jnp.int32, (n, n), 1) // seg
    ones = (r == c).astype(BF16)
    out = None
    for t in _split3(x):
        y = jnp.dot(t, ones, preferred_element_type=F32)
        out = y if out is None else out + y
    return out


def _ln0_kernel(xp_ref, xs_ref, g_ref, b_ref, of_ref, ob_ref, *, n_prompt):
    def norm(x):
        mu = jnp.mean(x, -1, keepdims=True)
        xc = x - mu
        var = jnp.mean(xc * xc, -1, keepdims=True)
        y = xc * lax.rsqrt(var + LN_EPS) * g_ref[...] + b_ref[...]
        of_ref[...] = y
        ob_ref[...] = y.astype(BF16)

    i = pl.program_id(0)

    @pl.when(i < n_prompt)
    def _():
        norm(xp_ref[...])

    @pl.when(i >= n_prompt)
    def _():
        norm(xs_ref[...])


def _ln0(xp, xs, g, b, tm=512):
    mp, ms = xp.shape[0], xs.shape[0]
    n_p, n_s = mp // tm, ms // tm
    out = jax.ShapeDtypeStruct((mp + ms, D_MODEL), F32)
    outb = jax.ShapeDtypeStruct((mp + ms, D_MODEL), BF16)
    return pl.pallas_call(
        functools.partial(_ln0_kernel, n_prompt=n_p),
        grid=(n_p + n_s,),
        in_specs=[
            pl.BlockSpec((tm, D_MODEL), lambda i: (jnp.minimum(i, n_p - 1), 0)),
            pl.BlockSpec((tm, D_MODEL), lambda i: (jnp.maximum(i - n_p, 0), 0)),
            pl.BlockSpec((1, D_MODEL), lambda i: (0, 0)),
            pl.BlockSpec((1, D_MODEL), lambda i: (0, 0)),
        ],
        out_specs=[pl.BlockSpec((tm, D_MODEL), lambda i: (i, 0)),
                   pl.BlockSpec((tm, D_MODEL), lambda i: (i, 0))],
        out_shape=[out, outb],
        compiler_params=_cparams(("parallel",)),
        name="ln0",
    )(xp, xs, g.reshape(1, -1), b.reshape(1, -1))


def _proj_kernel(x_ref, w_ref, b_ref, *rest, rot_hd, kinds, chunk_major, sigmoid):
    acc = jnp.dot(x_ref[...], w_ref[...], preferred_element_type=F32) + b_ref[...]
    if not rot_hd:
        (o_ref,) = rest
        o_ref[...] = (jax.nn.sigmoid(acc) if sigmoid else acc).astype(o_ref.dtype)
        return
    cs_ref, sn_ref, o_ref = rest
    h2 = rot_hd // 2
    lane = lax.broadcasted_iota(jnp.int32, (1, LANES), 1)
    first = (lane % rot_hd) < h2
    for c, kind in enumerate(kinds):
        val = acc[:, c * LANES:(c + 1) * LANES]
        if kind is not None:
            sw = jnp.where(first, pltpu.roll(val, LANES - h2, 1), pltpu.roll(val, h2, 1))
            val = val * cs_ref[kind] + sw * sn_ref[kind]
        val = val.astype(o_ref.dtype)
        if chunk_major:
            o_ref[c] = val
        else:
            o_ref[:, c * LANES:(c + 1) * LANES] = val


def _proj(xb, w, b, out_dtype, *, seq, tm, tn, rot=None, chunk_major=False, sigmoid=False,
          name="proj"):
    m, n = xb.shape[0], w.shape[1]
    assert m % tm == 0 and n % tn == 0 and seq % tm == 0
    in_specs = [
        pl.BlockSpec((tm, D_MODEL), lambda i, j: (i, 0)),
        pl.BlockSpec((D_MODEL, tn), lambda i, j: (0, j)),
        pl.BlockSpec((1, tn), lambda i, j: (0, j)),
    ]
    args = [xb, w, b.reshape(1, -1)]
    rot_hd, kinds = 0, None
    if rot is not None:
        rot_hd, cos, sin, kinds = rot
        assert len(kinds) == tn // LANES and LANES % rot_hd == 0
        spt = seq // tm
        tab = pl.BlockSpec((cos.shape[0], tm, LANES), lambda i, j: (0, i % spt, 0))
        in_specs += [tab, tab]
        args += [cos, sin]
    if chunk_major:
        assert rot is not None
        out_spec = pl.BlockSpec((tn // LANES, tm, LANES), lambda i, j: (j, i, 0))
        out_shape = jax.ShapeDtypeStruct((n // LANES, m, LANES), out_dtype)
    else:
        out_spec = pl.BlockSpec((tm, tn), lambda i, j: (i, j))
        out_shape = jax.ShapeDtypeStruct((m, n), out_dtype)
    return pl.pallas_call(
        functools.partial(_proj_kernel, rot_hd=rot_hd, kinds=kinds, chunk_major=chunk_major,
                          sigmoid=sigmoid),
        grid=(m // tm, n // tn),
        in_specs=in_specs,
        out_specs=out_spec,
        out_shape=out_shape,
        compiler_params=_cparams(("parallel", "arbitrary")),
        name=name,
    )(*args)


def _rope_tables(seq, head_dim, scales):
    half = head_dim // 2
    inv_freq = jnp.power(ROPE_THETA, -jnp.arange(half, dtype=F32) / half)
    ang = jnp.arange(seq, dtype=F32)[:, None] * inv_freq[None, :]
    lane = np.arange(LANES)
    f_idx = lane % half
    sign = np.where((lane % head_dim) < half, -1.0, 1.0).astype(np.float32)
    cos = jnp.cos(ang)[:, f_idx]
    sin = jnp.sin(ang)[:, f_idx] * sign[None, :]
    return (jnp.stack([cos * F32(s) for s in scales]), jnp.stack([sin * F32(s) for s in scales]))


def _merge_kernel(ya_ref, yb_ref, yc_ref, yd_ref, g_ref, x_ref, wb_ref, wo_ref, bo_ref, lg_ref,
                  lb_ref, of_ref, ob_ref):
    merged = None
    for i, y_ref in enumerate((ya_ref, yb_ref, yc_ref, yd_ref)):
        p = jnp.dot(y_ref[...], wb_ref[i], preferred_element_type=F32)
        gate = g_ref[:, i * D_MODEL:(i + 1) * D_MODEL].astype(F32)
        merged = gate * p if merged is None else merged + gate * p
    y = jnp.dot(merged.astype(BF16), wo_ref[...], preferred_element_type=F32) + bo_ref[...]
    z = DEEPNORM_ALPHA * x_ref[...] + y
    mu = jnp.mean(z, -1, keepdims=True)
    zc = z - mu
    var = jnp.mean(zc * zc, -1, keepdims=True)
    out = zc * lax.rsqrt(var + LN_EPS) * lg_ref[...] + lb_ref[...]
    of_ref[...] = out
    ob_ref[...] = out.astype(BF16)


def _merge(ya, yb, yc, yd, gates, x, wb, wo, bo, lg, lb, *, row0, rows, tm=512):
    assert row0 % tm == 0 and rows % tm == 0
    o = row0 // tm
    ysp = pl.BlockSpec((tm, BW), lambda i: (i + o, 0))
    const2 = lambda i: (0, 0)
    return pl.pallas_call(
        _merge_kernel,
        grid=(rows // tm,),
        in_specs=[ysp, ysp, ysp, ysp,
                  pl.BlockSpec((tm, 4 * D_MODEL), lambda i: (i + o, 0)),
                  pl.BlockSpec((tm, D_MODEL), lambda i: (i + o, 0)),
                  pl.BlockSpec((4, BW, D_MODEL), lambda i: (0, 0, 0)),
                  pl.BlockSpec((D_MODEL, D_MODEL), const2),
                  pl.BlockSpec((1, D_MODEL), const2),
                  pl.BlockSpec((1, D_MODEL), const2),
                  pl.BlockSpec((1, D_MODEL), const2)],
        out_specs=[pl.BlockSpec((tm, D_MODEL), lambda i: (i, 0)),
                   pl.BlockSpec((tm, D_MODEL), lambda i: (i, 0))],
        out_shape=[jax.ShapeDtypeStruct((rows, D_MODEL), F32),
                   jax.ShapeDtypeStruct((rows, D_MODEL), BF16)],
        compiler_params=_cparams(("parallel",)),
        name="merge",
    )(ya, yb, yc, yd, gates, x, wb, wo, bo.reshape(1, -1), lg.reshape(1, -1), lb.reshape(1, -1))


NA_QR = 4


def _na_geometry(rows):
    kr, wr = min(NA_ROWS, rows), min(NA_ROWS, rows) + NA_QR - 1
    assert rows % NA_QR == 0 and rows >= wr and kr == NA_ROWS
    steps = rows // NA_QR
    a = np.arange(NA_QR)[:, None]
    j = np.arange(wr)[None, :]
    pats = []
    for g in range(steps):
        r = NA_QR * g + a
        rs = np.clip(r - kr // 2, 0, rows - kr)
        ws = int(np.clip(NA_QR * g - kr // 2, 0, rows - wr))
        assert rs.min() >= ws and rs.max() + kr <= ws + wr
        valid = (ws + j >= rs) & (ws + j < rs + kr)
        dr = np.clip(ws + j - r + NA_ROWS - 1, 0, 2 * NA_ROWS - 2)
        pats.append((valid, np.where(valid, dr, 0)))
    same = lambda x, y: np.array_equal(x[0], y[0]) and np.array_equal(x[1], y[1])
    assert steps >= 3 and all(same(pats[g], pats[1]) for g in range(1, steps - 1))
    return kr, wr, steps, [pats[0], pats[1], pats[steps - 1]]


def _na_bias_tables(rpb, rows):
    kr, wr, steps, pats = _na_geometry(rows)
    qc = np.arange(GRID_W)[:, None]
    kc = np.arange(GRID_W)[None, :]
    c_start = np.clip(qc - NA_COLS // 2, 0, GRID_W - NA_COLS)
    col_ok = (kc >= c_start) & (kc < c_start + NA_COLS)
    dc = np.clip(kc - qc + NA_COLS - 1, 0, 2 * NA_COLS - 2)
    tabs = []
    for valid, dr in pats:
        b = rpb.astype(F32)[:, dr][:, :, :, dc]
        ok = valid[:, :, None, None] & col_ok[None, None]
        b = jnp.where(ok[None], b, NEG_INF)
        tabs.append(b.transpose(0, 1, 3, 2, 4).reshape(NA_HEADS, NA_QR * GRID_W, wr * GRID_W))
    return jnp.stack(tabs, axis=1)


def _stack_heads(q, n_heads, head_dim):
    lane = lax.broadcasted_iota(jnp.int32, (1, q.shape[1]), 1)
    zero = jnp.zeros_like(q)
    return jnp.concatenate(
        [jnp.where((lane >= h * head_dim) & (lane < (h + 1) * head_dim), q, zero)
         for h in range(n_heads)], axis=0)


def _unstack_heads(o, n_heads, head_dim):
    m = o.shape[0] // n_heads
    lane = lax.broadcasted_iota(jnp.int32, (1, o.shape[1]), 1)
    acc = o[0:m]
    for h in range(1, n_heads):
        in_head = (lane >= h * head_dim) & (lane < (h + 1) * head_dim)
        acc = jnp.where(in_head, o[h * m:(h + 1) * m], acc)
    return acc


def _na_kernel(q_ref, k_ref, v_ref, g_ref, bias_ref, o_ref, *, rows):
    kr, wr = min(NA_ROWS, rows), min(NA_ROWS, rows) + NA_QR - 1
    g = pl.program_id(1)
    ws = jnp.clip(NA_QR * g - kr // 2, 0, rows - wr)
    start = pl.multiple_of(ws * GRID_W, GRID_W)
    k = k_ref[pl.ds(start, wr * GRID_W), :]
    v = v_ref[pl.ds(start, wr * GRID_W), :]
    qs = _stack_heads(q_ref[...], NA_HEADS, NA_HEAD_DIM)
    m_rows = q_ref.shape[0]
    halves = [slice(0, 2 * m_rows), slice(2 * m_rows, 4 * m_rows)]
    ss = [_dot_nt(qs[h], k) for h in halves]
    ps, ls = [], []
    for i, s in enumerate(ss):
        bias = bias_ref[2 * i:2 * i + 2].reshape(s.shape)
        s = jnp.where(bias > 0.5 * NEG_INF, s + bias, NEG_INF)
        p = jnp.exp(s - jnp.max(s, -1, keepdims=True))
        ls.append(jnp.sum(p, -1, keepdims=True))
        ps.append(p.astype(BF16))
    o = jnp.concatenate([jnp.dot(p, v, preferred_element_type=F32) / l for p, l in zip(ps, ls)], 0)
    o = _unstack_heads(o, NA_HEADS, NA_HEAD_DIM)
    o_ref[...] = (o * jax.nn.silu(g_ref[...])).astype(o_ref.dtype)


def _na(hb, fb, bias_tab, *, batch, seq):
    rows = seq // GRID_W
    kr, wr, steps, _ = _na_geometry(rows)
    qb = NA_QR * GRID_W

    def pattern_of(g):
        return jnp.where(g == 0, 0, jnp.where(g == steps - 1, 2, 1))

    return pl.pallas_call(
        functools.partial(_na_kernel, rows=rows),
        grid=(batch, steps),
        in_specs=[
            pl.BlockSpec((None, qb, BW), lambda b, g: (b, g, 0)),
            pl.BlockSpec((None, seq, BW), lambda b, g: (b, 0, 1)),
            pl.BlockSpec((None, seq, BW), lambda b, g: (b, 0, 2)),
            pl.BlockSpec((None, qb, BW), lambda b, g: (b, g, 5)),
            pl.BlockSpec((NA_HEADS, None, qb, wr * GRID_W), lambda b, g: (0, pattern_of(g), 0, 0)),
        ],
        out_specs=pl.BlockSpec((None, qb, BW), lambda b, g: (b, g, 0)),
        out_shape=jax.ShapeDtypeStruct((batch, seq, BW), BF16),
        compiler_params=_cparams(("parallel", "arbitrary")),
        name="na_attn",
    )(hb, hb, hb, fb, bias_tab)


RW_C = 64
RW_FIELDS = 9


def _mm(a, b, mode="nn", passes=1):
    dims = {"nn": (((1,), (0,)), ((), ())), "nt": (((1,), (1,)), ((), ()))}[mode]
    dg = lambda x, y: lax.dot_general(x, y, dims, preferred_element_type=F32)
    ah, bh = a.astype(BF16), b.astype(BF16)
    out = dg(ah, bh)
    if passes == 3:
        al = (a - ah.astype(F32)).astype(BF16)
        bl = (b - bh.astype(F32)).astype(BF16)
        out = out + (dg(al, bh) + dg(ah, bl))
    return out


def _softplus(x):
    return jnp.maximum(x, 0.0) + jnp.log(1.0 + jnp.exp(-jnp.abs(x)))


def _rwprep_kernel(x_ref, prev_ref, next_ref, mu_ref, w0_ref, w2_ref, a0_ref, a2_ref, kk_ref, ka_ref,
                   rk_ref, f_ref, e_ref, *, rc):
    c = pl.program_id(1)
    lane = lax.broadcasted_iota(jnp.int32, (1, 2 * RW_LORA), 1)
    ri = lax.broadcasted_iota(jnp.int32, (rc, 1), 0)
    x = x_ref[...]
    prev_row = jnp.where(c > 0, prev_ref[7:8, :], 0.0)
    next_row = jnp.where(c < pl.num_programs(1) - 1, next_ref[0:1, :], 0.0)
    up = jnp.where(ri == 0, prev_row, pltpu.roll(x, 1, 0))
    dn = jnp.where(ri == rc - 1, next_row, pltpu.roll(x, rc - 1, 0))
    xs = x + mu_ref[...] * (0.5 * (up + dn) - x)
    r, k, v, g = (xs[:, i * BW:(i + 1) * BW] for i in range(4))
    wl = xs[:, 4 * BW:4 * BW + 2 * RW_LORA]
    al = xs[:, 4 * BW + 2 * RW_LORA:4 * BW + 4 * RW_LORA]
    kk = k * kk_ref[...]
    kap = kk * lax.rsqrt(jnp.maximum(_seg_sum(kk * kk, RW_HEAD_DIM), 1e-24))
    tw = jnp.tanh(wl)
    kd_sum = None
    for d in range(2):
        dm = ((lane >= d * RW_LORA) & (lane < (d + 1) * RW_LORA)).astype(F32)
        w_raw = w0_ref[d:d + 1, :] + _mm(tw * dm, w2_ref[...], passes=3)
        lw = -jnp.exp(-_softplus(-w_raw) - 0.5)
        a = jax.nn.sigmoid(a0_ref[d:d + 1, :] + _mm(al * dm, a2_ref[...], passes=3))
        kd = k * (1.0 + (a - 1.0) * ka_ref[...])
        f_ref[:, (3 + 3 * d) * BW:(4 + 3 * d) * BW] = lw
        f_ref[:, (4 + 3 * d) * BW:(5 + 3 * d) * BW] = kd
        f_ref[:, (5 + 3 * d) * BW:(6 + 3 * d) * BW] = kap * a
        kd_sum = kd if kd_sum is None else kd_sum + kd
    f_ref[:, 0:BW] = r
    f_ref[:, BW:2 * BW] = v
    f_ref[:, 2 * BW:3 * BW] = kap
    bonus = _seg_sum(r * kd_sum * rk_ref[...], RW_HEAD_DIM)
    e_ref[:, 0:BW] = bonus * v
    e_ref[:, BW:2 * BW] = jax.nn.silu(g)


def _rwprep(fb, mu, w0, w2, a0, a2, k_k, k_a, r_k, *, batch, seq, rc=256):
    ucols = 4 * BW + 4 * RW_LORA
    row = lambda z: z.astype(F32).reshape(1, -1)
    c2 = lambda b, c: (0, 0)
    tpb = rc // 8
    return pl.pallas_call(
        functools.partial(_rwprep_kernel, rc=rc),
        grid=(batch, seq // rc),
        in_specs=[
            pl.BlockSpec((None, rc, ucols), lambda b, c: (b, c, 0)),
            pl.BlockSpec((None, 8, ucols), lambda b, c: (b, jnp.maximum(c * tpb - 1, 0), 0)),
            pl.BlockSpec((None, 8, ucols),
                         lambda b, c: (b, jnp.minimum((c + 1) * tpb, seq // 8 - 1), 0)),
            pl.BlockSpec((1, ucols), c2),
            pl.BlockSpec((2, BW), c2),
            pl.BlockSpec((2 * RW_LORA, BW), c2),
            pl.BlockSpec((2, BW), c2),
            pl.BlockSpec((2 * RW_LORA, BW), c2),
            pl.BlockSpec((1, BW), c2),
            pl.BlockSpec((1, BW), c2),
            pl.BlockSpec((1, BW), c2),
        ],
        out_specs=[pl.BlockSpec((None, rc, RW_FIELDS * BW), lambda b, c: (b, c, 0)),
                   pl.BlockSpec((None, rc, 2 * BW), lambda b, c: (b, c, 0))],
        out_shape=[jax.ShapeDtypeStruct((batch, seq, RW_FIELDS * BW), F32),
                   jax.ShapeDtypeStruct((batch, seq, 2 * BW), F32)],
        compiler_params=_cparams(("parallel", "arbitrary")),
        name="rwkv_prep",
    )(fb, fb, fb, row(mu), w0.astype(F32), w2.astype(F32).reshape(2 * RW_LORA, BW), a0.astype(F32),
      a2.astype(F32).reshape(2 * RW_LORA, BW), row(k_k), row(k_a), row(r_k))


def _rw_bd(x):
    lane = lax.broadcasted_iota(jnp.int32, (1, BW), 1)
    xb = x.astype(BF16)
    zero = jnp.zeros_like(xb)
    return jnp.concatenate(
        [jnp.where((lane >= h * RW_C) & (lane < (h + 1) * RW_C), xb, zero) for h in range(RW_HEADS)],
        axis=0)


def _rw_advance(terms, z):
    gm, hv, nt = terms
    both = _mm(gm, z)
    return both[0:RW_C] + hv, _rw_bd(both[RW_C:2 * RW_C] + nt)


def _rw_prepare(specs):
    C, H = RW_C, RW_HEADS
    assert H * C == BW and RW_HEAD_DIM == C
    bd = _rw_bd
    cat = lambda a, b: jnp.concatenate([a, b], axis=0)
    ri = lax.broadcasted_iota(jnp.int32, (C, C), 0)
    ci = lax.broadcasted_iota(jnp.int32, (C, C), 1)
    tris = [(ci <= ri).astype(BF16), (ci >= ri).astype(BF16)]
    row = lax.broadcasted_iota(jnp.int32, (C, BW), 0)
    col = lax.broadcasted_iota(jnp.int32, (C, BW), 1) % C
    eye = col == row
    strict = [col < row, col > row]
    incl = [col <= row, col >= row]

    def lc_t(x):
        t = bd(x).astype(F32).T
        return t[0:C] + t[C:2 * C] + t[2 * C:3 * C] + t[3 * C:4 * C]

    def load(spec):
        f_ref, r0, d = spec
        fld = lambda i: f_ref[r0:r0 + C, i * BW:(i + 1) * BW]
        c = dict(d=d, r=fld(0), v=fld(1), kap=fld(2), lw=fld(3 + 3 * d), kd=fld(4 + 3 * d),
                 beta=fld(5 + 3 * d))
        cs = None
        for t in _split3(c["lw"]):
            y = jnp.dot(tris[d], t, preferred_element_type=F32)
            cs = y if cs is None else cs + y
        c["cs"] = cs
        return c

    def scale(c):
        cs, d = c["cs"], c["d"]
        tot = cs[0:1] if d == 1 else cs[C - 1:C]
        ginv = jnp.exp(-cs)
        gto = jnp.exp(tot - cs)
        kt = c["kap"] * jnp.exp(cs - c["lw"])
        rt = c["r"] * jnp.exp(cs)
        c.update(gc=jnp.exp(tot), rt=rt, ktrt=cat(kt, rt), kt_b=bd(kt), v_b=bd(c["v"]),
                 kg_b=bd(c["kd"] * ginv), bg_b=bd(c["beta"] * ginv),
                 kh=c["kd"] * gto, bh=c["beta"] * gto)
        return c

    def gram(c):
        d = c["d"]
        gk = _mm(c["ktrt"], c["kg_b"], "nt")
        gb = _mm(c["ktrt"], c["bg_b"], "nt")
        c.update(l_kk=jnp.where(strict[d], gk[0:C], 0.0), l_kr=jnp.where(incl[d], gk[C:2 * C], 0.0),
                 l_br=jnp.where(incl[d], gb[C:2 * C], 0.0),
                 npow=-jnp.where(strict[d], gb[0:C], 0.0), w=jnp.where(eye, 1.0, 0.0))
        return c

    def level(c, last):
        if last:
            c["w"] = c["w"] + _mm(c["w"], bd(c["npow"]))
        else:
            both = _mm(cat(c["npow"], c["w"]), bd(c["npow"]))
            c["npow"], c["w"] = both[0:C], c["w"] + both[C:2 * C]
        return c

    def solve(c):
        c["p1_b"] = bd(_mm(c["w"], c["kt_b"]))
        c["p2"] = _mm(c["w"], bd(c["l_kk"]))
        c["p2_b"] = bd(c["p2"])
        return c

    def outputs(c):
        c["g"] = c["rt"] - _mm(c["l_br"], c["p1_b"])
        hm = c["l_kr"] - _mm(c["l_br"], c["p2_b"])
        both = _mm(cat(hm, c["p2"]), c["v_b"])
        c["hv"], c["p2v_b"] = both[0:C], bd(both[C:2 * C])
        return c

    def state(c):
        bh_t = lc_t(c["bh"])
        mt = jnp.where(eye, c["gc"], 0.0) - _mm(bh_t, c["p1_b"])
        nt = _mm(lc_t(c["kh"]), c["v_b"]) - _mm(bh_t, c["p2v_b"])
        return cat(c["g"], mt).astype(BF16), c["hv"], nt

    cs = [load(s) for s in specs]
    cs = [scale(c) for c in cs]
    cs = [gram(c) for c in cs]
    levels = int(math.log2(C))
    for lvl in range(levels):
        cs = [level(c, lvl == levels - 1) for c in cs]
    cs = [solve(c) for c in cs]
    cs = [outputs(c) for c in cs]
    return [state(c) for c in cs]


def _rwscan_kernel(ff_ref, fr_ref, yf_ref, yr_ref, zf_ref, zr_ref, *, nch):
    @pl.when(pl.program_id(1) == 0)
    def _():
        zf_ref[...] = jnp.zeros_like(zf_ref)
        zr_ref[...] = jnp.zeros_like(zr_ref)

    C = RW_C
    pre = _rw_prepare([(ff_ref, j * C, 0) for j in range(nch)]
                      + [(fr_ref, j * C, 1) for j in range(nch)])
    pre_f, pre_r = pre[:nch], pre[nch:]
    zf, zr = zf_ref[...], zr_ref[...]
    for j in range(nch):
        yf_ref[j * C:(j + 1) * C, :], zf = _rw_advance(pre_f[j], zf)
        jr = nch - 1 - j
        yr_ref[jr * C:(jr + 1) * C, :], zr = _rw_advance(pre_r[jr], zr)
    zf_ref[...] = zf
    zr_ref[...] = zr


def _rwscan(f, *, batch, seq, nch=4):
    rb = nch * RW_C
    nc = seq // rb
    fsp = lambda imap: pl.BlockSpec((None, rb, RW_FIELDS * BW), imap)
    ysp = lambda imap: pl.BlockSpec((None, rb, BW), imap)
    fwd = lambda b, s: (b, s, 0)
    rev = lambda b, s: (b, nc - 1 - s, 0)
    y = jax.ShapeDtypeStruct((batch, seq, BW), F32)
    return pl.pallas_call(
        functools.partial(_rwscan_kernel, nch=nch),
        grid=(batch, nc),
        in_specs=[fsp(fwd), fsp(rev)],
        out_specs=[ysp(fwd), ysp(rev)],
        out_shape=[y, y],
        scratch_shapes=[pltpu.VMEM((BW, BW), BF16), pltpu.VMEM((BW, BW), BF16)],
        compiler_params=_cparams(("parallel", "arbitrary")),
        name="rwkv_scan",
    )(f, f)


def _rwpost_kernel(yf_ref, yr_ref, e_ref, lg_ref, lb_ref, o_ref):
    y = yf_ref[...] + yr_ref[...]
    inv = 1.0 / RW_HEAD_DIM
    yc = y - _seg_sum(y, RW_HEAD_DIM) * inv
    var = _seg_sum(yc * yc, RW_HEAD_DIM) * inv
    yn = yc * lax.rsqrt(var + RW_LNX_EPS) * lg_ref[...] + lb_ref[...]
    o_ref[...] = ((yn + e_ref[:, 0:BW]) * e_ref[:, BW:2 * BW]).astype(o_ref.dtype)


def _rwpost(yf, yr, e, lnx_g, lnx_b, *, tm=512):
    m = yf.shape[0]
    row = lambda z: z.astype(F32).reshape(1, -1)
    ysp = pl.BlockSpec((tm, BW), lambda i: (i, 0))
    return pl.pallas_call(
        _rwpost_kernel,
        grid=(m // tm,),
        in_specs=[ysp, ysp, pl.BlockSpec((tm, 2 * BW), lambda i: (i, 0)),
                  pl.BlockSpec((1, BW), lambda i: (0, 0)), pl.BlockSpec((1, BW), lambda i: (0, 0))],
        out_specs=ysp,
        out_shape=jax.ShapeDtypeStruct((m, BW), BF16),
        compiler_params=_cparams(("parallel",)),
        name="rwkv_post",
    )(yf, yr, e, row(lnx_g), row(lnx_b))


def _df_kernel(lam_ref, sg_ref, q_ref, k_ref, v_ref, g_ref, o_ref, *, lam_init):
    lam = lam_ref[...]
    e1 = jnp.exp(jnp.sum(lam[0:1] * lam[1:2], -1, keepdims=True))
    e2 = jnp.exp(jnp.sum(lam[2:3] * lam[3:4], -1, keepdims=True))
    lam_full = e1 - e2 + lam_init
    d = DF_HEAD_DIM
    tq = q_ref.shape[0]
    qs = _stack_heads(q_ref[...], 2 * DF_HEADS, d)
    k, v = k_ref[...], v_ref[...]
    ss = [_dot_nt(qs[2 * h * tq:2 * (h + 1) * tq], k) for h in range(DF_HEADS)]
    ps, ls = [], []
    for s in ss:
        p = jnp.exp(s - jnp.max(s, -1, keepdims=True))
        ls.append(jnp.sum(p, -1, keepdims=True))
        ps.append(p.astype(BF16))
    os_ = [jnp.dot(p, v, preferred_element_type=F32) / l for p, l in zip(ps, ls)]
    lane = lax.broadcasted_iota(jnp.int32, (1, BW), 1)
    acc = None
    for h, o in enumerate(os_):
        comb = o[0:tq] - lam_full * o[tq:2 * tq]
        in_head = (lane >= 2 * h * d) & (lane < 2 * (h + 1) * d)
        acc = comb if acc is None else jnp.where(in_head, comb, acc)
    ms = _seg_sum(acc * acc, 2 * d) * (1.0 / (2 * d))
    o = acc * lax.rsqrt(ms + DF_EPS) * sg_ref[...] * (1.0 - lam_init)
    o_ref[...] = (o * jax.nn.silu(g_ref[...])).astype(o_ref.dtype)


def _df(cqk, hb, fb, lam, subln_g, lam_init, *, batch, seq, tq=256):
    sg = jnp.tile(subln_g.astype(F32), DF_HEADS).reshape(1, BW)
    return pl.pallas_call(
        functools.partial(_df_kernel, lam_init=lam_init),
        grid=(batch, seq // tq),
        in_specs=[
            pl.BlockSpec((4, DF_HEAD_DIM), lambda b, j: (0, 0)),
            pl.BlockSpec((1, BW), lambda b, j: (0, 0)),
            pl.BlockSpec((None, tq, BW), lambda b, j: (b, j, 0)),
            pl.BlockSpec((None, seq, BW), lambda b, j: (b, 0, 1)),
            pl.BlockSpec((None, seq, BW), lambda b, j: (b, 0, 3)),
            pl.BlockSpec((None, tq, BW), lambda b, j: (b, j, 6)),
        ],
        out_specs=pl.BlockSpec((None, tq, BW), lambda b, j: (b, j, 0)),
        out_shape=jax.ShapeDtypeStruct((batch, seq, BW), BF16),
        compiler_params=_cparams(("parallel", "arbitrary")),
        name="diff_attn",
    )(lam.astype(F32), sg, cqk, cqk, hb, fb)


DL_QB = 2 * DL_HALF
DL_UNROLL = 4


def _dl_group(qkv_ref, qs, kpad, vpad, oacc, lacc, gidx, dil, seq):
    L = seq // dil
    assert L % DL_QB == 0 and (seq // DL_QB) % DL_UNROLL == 0
    nblk = L // DL_QB
    seg = L + 2 * DL_HALF
    zpad = jnp.zeros((DL_HALF, BW), BF16)

    def rows_of(rho):
        return pl.ds(rho, L, stride=dil) if dil > 1 else pl.ds(0, L)

    def stage(rho, carry):
        qrow = pl.multiple_of(rho * L, DL_QB)
        krow = pl.multiple_of(rho * seg, DL_HALF)
        for ref in (kpad, vpad):
            ref[pl.ds(krow, DL_HALF), :] = zpad
            ref[pl.ds(krow + DL_HALF + L, DL_HALF), :] = zpad
        for c in range(BW // LANES):
            cols = slice(c * LANES, (c + 1) * LANES)
            qs[pl.ds(qrow, L), cols] = qkv_ref[c, rows_of(rho), :].astype(BF16)
            kpad[pl.ds(krow + DL_HALF, L), cols] = qkv_ref[2 + c, rows_of(rho), :].astype(BF16)
            vpad[pl.ds(krow + DL_HALF, L), cols] = qkv_ref[4 + c, rows_of(rho), :].astype(BF16)
        return carry

    lax.fori_loop(0, dil, stage, 0)

    hq = DL_HEADS * DL_QB
    qi = lax.broadcasted_iota(jnp.int32, (hq, 2 * DL_QB), 0) % DL_QB
    ji = lax.broadcasted_iota(jnp.int32, (hq, 2 * DL_QB), 1)
    band = jnp.abs(ji - DL_HALF - qi) <= DL_HALF
    lane = lax.broadcasted_iota(jnp.int32, (1, BW), 1)

    def blocks(it, carry):
        fs = [it * DL_UNROLL + u for u in range(DL_UNROLL)]
        rhos = [f // nblk for f in fs]
        ns = [f % nblk for f in fs]
        qst = [_stack_heads(qs[pl.ds(pl.multiple_of(f * DL_QB, DL_QB), DL_QB), :], DL_HEADS,
                            DL_HEAD_DIM) for f in fs]
        krows = [pl.multiple_of(rho * seg + n * DL_QB, DL_HALF) for rho, n in zip(rhos, ns)]
        ss = [_dot_nt(q, kpad[pl.ds(kr, 2 * DL_QB), :]) for q, kr in zip(qst, krows)]
        ps, lses, dens = [], [], []
        for s, n in zip(ss, ns):
            kpos = n * DL_QB - DL_HALF + ji
            s = jnp.where(band & (kpos >= 0) & (kpos < L), s, NEG_INF)
            m = jnp.max(s, -1, keepdims=True)
            p = jnp.exp(s - m)
            den = jnp.sum(p, -1, keepdims=True)
            ps.append(p.astype(BF16))
            dens.append(den)
            lses.append(m + jnp.log(den))
        os_ = [jnp.dot(p, vpad[pl.ds(kr, 2 * DL_QB), :], preferred_element_type=F32) / den
               for p, kr, den in zip(ps, krows, dens)]
        for o, lse, rho, n in zip(os_, lses, rhos, ns):
            acc_o = _unstack_heads(o, DL_HEADS, DL_HEAD_DIM)
            acc_l = _unstack_heads(jnp.broadcast_to(lse, (hq, BW)), DL_HEADS, DL_HEAD_DIM)
            if dil > 1:
                dst = pl.ds(rho + dil * n * DL_QB, DL_QB, stride=dil)
            else:
                dst = pl.ds(pl.multiple_of(n * DL_QB, DL_QB), DL_QB)
            for c in range(BW // LANES):
                oacc[gidx, c, dst, :] = acc_o[:, c * LANES:(c + 1) * LANES]
                lacc[gidx, c, dst, :] = acc_l[:, c * LANES:(c + 1) * LANES]
        return carry

    lax.fori_loop(0, seq // DL_QB // DL_UNROLL, blocks, 0)


def _dl_kernel(qkv_ref, g_ref, o_ref, qs, kpad, vpad, oacc, lacc, *, seq):
    gi = pl.program_id(1)
    for gidx, dil in enumerate(DL_DILATIONS):
        @pl.when(gi == gidx)
        def _(gidx=gidx, dil=dil):
            _dl_group(qkv_ref, qs, kpad, vpad, oacc, lacc, gidx, dil, seq)

    @pl.when(gi == len(DL_DILATIONS) - 1)
    def _():
        rc = 256

        def chunk(c, carry):
            rows = pl.ds(pl.multiple_of(c * rc, rc), rc)
            for lc in range(BW // LANES):
                cols = slice(lc * LANES, (lc + 1) * LANES)
                ls = [lacc[g, lc, rows, :] for g in range(len(DL_DILATIONS))]
                mx = functools.reduce(jnp.maximum, ls)
                ws = [jnp.exp(l - mx) for l in ls]
                num = sum(w * oacc[g, lc, rows, :] for g, w in enumerate(ws))
                out = num / sum(ws)
                o_ref[rows, cols] = (out * jax.nn.silu(g_ref[rows, cols])).astype(o_ref.dtype)
            return carry

        lax.fori_loop(0, seq // rc, chunk, 0)


def _dl(dqkv, fb, *, batch, seq):
    ng = len(DL_DILATIONS)
    nch = 3 * BW // LANES
    assert all(w // (2 * d) == DL_HALF for w, d in zip(DL_WINDOWS, DL_DILATIONS))
    return pl.pallas_call(
        functools.partial(_dl_kernel, seq=seq),
        grid=(batch, ng),
        in_specs=[
            pl.BlockSpec((nch, None, seq, LANES), lambda b, g: (g, b, 0, 0)),
            pl.BlockSpec((None, seq, BW), lambda b, g: (b, 0, 7)),
        ],
        out_specs=pl.BlockSpec((None, seq, BW), lambda b, g: (b, 0, 0)),
        out_shape=jax.ShapeDtypeStruct((batch, seq, BW), BF16),
        scratch_shapes=[
            pltpu.VMEM((seq, BW), BF16),
            pltpu.VMEM((seq + 2 * DL_HALF * max(DL_DILATIONS), BW), BF16),
            pltpu.VMEM((seq + 2 * DL_HALF * max(DL_DILATIONS), BW), BF16),
            pltpu.VMEM((ng, BW // LANES, seq, LANES), F32),
            pltpu.VMEM((ng, BW // LANES, seq, LANES), F32),
        ],
        compiler_params=_cparams(("parallel", "arbitrary")),
        name="dilated_attn",
    )(dqkv, fb)


_A0 = 0
_B0 = _A0 + 4 * BW
_C0 = _B0 + 4 * BW + 4 * RW_LORA
_D0 = _C0 + 4 * BW
_DG0 = _D0 + 9 * BW
_G0 = _DG0 + BW
PROJ_TM = 1024


def _layer(xf, xb, l, p, tabs, *, batch, seq, last_split):
    w, b = p["w_in"][l], p["b_in"][l]
    cols = lambda lo, n: (w[:, lo:lo + n], b[lo:lo + n])
    cat = lambda parts: (jnp.concatenate([q[0] for q in parts], 1).astype(BF16),
                         jnp.concatenate([q[1] for q in parts], 0).astype(F32))
    scaled = lambda part, s: (part[0] * s, part[1] * s)
    q_scale = NA_HEAD_DIM ** -0.5
    proj = functools.partial(_proj, xb, seq=seq, tm=PROJ_TM)

    gates = proj(*cat([cols(_G0, 4 * D_MODEL)]), BF16, tn=1024, sigmoid=True, name="proj_gates")
    fb = proj(*cat([cols(_B0, _C0 - _B0), cols(_A0 + 3 * BW, BW), cols(_C0 + 3 * BW, BW),
                    cols(_DG0, BW)]), F32, tn=1024, name="proj_f32")
    hb = proj(*cat([scaled(cols(_A0, BW), q_scale), cols(_A0 + BW, 2 * BW), cols(_C0 + 2 * BW, BW)]),
              BF16, tn=1024, name="proj_bf16")
    cpb = BW // LANES
    cqk = proj(*cat([cols(_C0, 2 * BW)]), BF16, tn=2 * BW, name="proj_rot_c",
               rot=(DF_HEAD_DIM,) + tabs["c"] + ((0,) * cpb + (1,) * cpb,))
    dparts = []
    for g in range(len(DL_DILATIONS)):
        dparts += [scaled(cols(_D0 + 3 * g * BW, BW), q_scale), cols(_D0 + (3 * g + 1) * BW, 2 * BW)]
    dq = proj(*cat(dparts), F32, tn=3 * BW, chunk_major=True, name="proj_rot_d",
              rot=(DL_HEAD_DIM,) + tabs["d"] + ((0,) * (2 * cpb) + (None,) * cpb,))

    r3 = lambda z: z.reshape(batch, seq, z.shape[-1])
    hb3, fb3 = r3(hb), r3(fb)
    ya = _na(hb3, fb3, _na_bias_tables(p["na_rpb"][l], seq // GRID_W), batch=batch, seq=seq)
    f, e = _rwprep(fb3, p["rw_mu"][l], p["rw_w0"][l], p["rw_w2"][l], p["rw_a0"][l], p["rw_a2"][l],
                   p["rw_kk"][l], p["rw_ka"][l], p["rw_rk"][l].reshape(-1), batch=batch, seq=seq)
    yf, yr = _rwscan(f, batch=batch, seq=seq)
    m = batch * seq
    yb = _rwpost(yf.reshape(m, BW), yr.reshape(m, BW), e.reshape(m, 2 * BW), p["rw_lnx_g"][l],
                 p["rw_lnx_b"][l])
    lam_init = 0.8 - 0.6 * math.exp(-0.3 * l)
    yc = _df(r3(cqk), hb3, fb3, p["df_lam"][l], p["df_subln_g"][l], lam_init, batch=batch, seq=seq)
    yd = _dl(dq.reshape(dq.shape[0], batch, seq, LANES), fb3, batch=batch, seq=seq)

    margs = (ya.reshape(m, BW), yb, yc.reshape(m, BW), yd.reshape(m, BW), gates, xf,
             p["w_branch"][l].astype(BF16), p["w_out"][l].astype(BF16), p["b_out"][l],
             p["ln_g"][l], p["ln_b"][l])
    if last_split is None:
        return _merge(*margs, row0=0, rows=m)
    return [_merge(*margs, row0=r0, rows=n) for r0, n in last_split]


def kernel(x_prompt, x_sample, ln0_g, ln0_b, w_in, b_in, na_rpb, rw_mu, rw_w0, rw_w2, rw_a0, rw_a2,
           rw_kk, rw_ka, rw_rk, rw_lnx_g, rw_lnx_b, df_lam, df_subln_g, w_branch, w_out, b_out,
           ln_g, ln_b):
    p = dict(w_in=w_in, b_in=b_in, na_rpb=na_rpb, rw_mu=rw_mu, rw_w0=rw_w0, rw_w2=rw_w2,
             rw_a0=rw_a0, rw_a2=rw_a2, rw_kk=rw_kk, rw_ka=rw_ka, rw_rk=rw_rk, rw_lnx_g=rw_lnx_g,
             rw_lnx_b=rw_lnx_b, df_lam=df_lam, df_subln_g=df_subln_g, w_branch=w_branch,
             w_out=w_out, b_out=b_out, ln_g=ln_g, ln_b=ln_b)
    bp, seq, _ = x_prompt.shape
    bs = x_sample.shape[0]
    assert x_sample.shape[1] == seq
    batch = bp + bs
    xf, xb = _ln0(x_prompt.reshape(bp * seq, D_MODEL), x_sample.reshape(bs * seq, D_MODEL),
                  ln0_g, ln0_b)
    tabs = {"c": _rope_tables(seq, DF_HEAD_DIM, (DF_HEAD_DIM ** -0.5, 1.0)),
            "d": _rope_tables(seq, DL_HEAD_DIM, (1.0,))}
    for l in range(DEPTH - 1):
        xf, xb = _layer(xf, xb, l, p, tabs, batch=batch, seq=seq, last_split=None)
    split = [(0, bp * seq), (bp * seq, bs * seq)]
    (yp, _), (ys, _) = _layer(xf, xb, DEPTH - 1, p, tabs, batch=batch, seq=seq, last_split=split)
    return yp.reshape(bp, seq, D_MODEL), ys.reshape(bs, seq, D_MODEL)
```

```python
import functools
import math

import numpy as np
import jax
import jax.numpy as jnp
from jax import lax
from jax.experimental import pallas as pl
from jax.experimental.pallas import tpu as pltpu

F32 = jnp.float32
BF16 = jnp.bfloat16

D_MODEL = 1024
DEPTH = 2
GRID_W = 64
NA_HEADS, NA_HEAD_DIM, NA_ROWS, NA_COLS = 4, 64, 8, 16
RW_HEADS, RW_HEAD_DIM, RW_LORA = 4, 64, 64
RW_LNX_EPS = 64e-5
DF_HEADS, DF_HEAD_DIM, DF_EPS = 4, 32, 1e-5
DL_HEADS, DL_HEAD_DIM = 4, 64
DL_DILATIONS = (1, 4, 16)
DL_WINDOWS = (128, 512, 2048)
DL_HALF = 64
BW = 256
ROPE_THETA = 10000.0
LN_EPS = 1e-5
DEEPNORM_ALPHA = (2 * DEPTH) ** 0.25
MASKED = -float("inf")

LANES = 128
VMEM_LIMIT = 56 * 1024 * 1024


def _cparams(sem):
    return pltpu.CompilerParams(dimension_semantics=sem, vmem_limit_bytes=VMEM_LIMIT)


def _dot_nt(a, b):
    return lax.dot_general(a, b, (((1,), (1,)), ((), ())), preferred_element_type=F32)


def _split3(x):
    hi = x.astype(BF16)
    r1 = x - hi.astype(F32)
    mid = r1.astype(BF16)
    lo = (r1 - mid.astype(F32)).astype(BF16)
    return hi, mid, lo


def _seg_sum(x, seg):
    n = x.shape[-1]
    r = lax.broadcasted_iota(jnp.int32, (n, n), 0) // seg
    c = lax.broadcasted_iota(jnp.int32, (n, n), 1) // seg
    ones = (r == c).astype(BF16)
    out = None
    for t in _split3(x):
        y = jnp.dot(t, ones, preferred_element_type=F32)
        out = y if out is None else out + y
    return out


def _ln0_kernel(xp_ref, xs_ref, g_ref, b_ref, of_ref, ob_ref, *, n_prompt):
    def norm(x):
        mu = jnp.mean(x, -1, keepdims=True)
        xc = x - mu
        var = jnp.mean(xc * xc, -1, keepdims=True)
        y = xc * lax.rsqrt(var + LN_EPS) * g_ref[...] + b_ref[...]
        of_ref[...] = y
        ob_ref[...] = y.astype(BF16)

    i = pl.program_id(0)

    @pl.when(i < n_prompt)
    def _():
        norm(xp_ref[...])

    @pl.when(i >= n_prompt)
    def _():
        norm(xs_ref[...])


def _ln0(xp, xs, g, b, tm=512):
    mp, ms = xp.shape[0], xs.shape[0]
    n_p, n_s = mp // tm, ms // tm
    out = jax.ShapeDtypeStruct((mp + ms, D_MODEL), F32)
    outb = jax.ShapeDtypeStruct((mp + ms, D_MODEL), BF16)
    return pl.pallas_call(
        functools.partial(_ln0_kernel, n_prompt=n_p),
        grid=(n_p + n_s,),
        in_specs=[
            pl.BlockSpec((tm, D_MODEL), lambda i: (jnp.minimum(i, n_p - 1), 0)),
            pl.BlockSpec((tm, D_MODEL), lambda i: (jnp.maximum(i - n_p, 0), 0)),
            pl.BlockSpec((1, D_MODEL), lambda i: (0, 0)),
            pl.BlockSpec((1, D_MODEL), lambda i: (0, 0)),
        ],
        out_specs=[pl.BlockSpec((tm, D_MODEL), lambda i: (i, 0)),
                   pl.BlockSpec((tm, D_MODEL), lambda i: (i, 0))],
        out_shape=[out, outb],
        compiler_params=_cparams(("parallel",)),
        name="ln0",
    )(xp, xs, g.reshape(1, -1), b.reshape(1, -1))


def _proj_kernel(x_ref, w_ref, b_ref, *rest, rot_hd, kinds, chunk_major, sigmoid):
    acc = jnp.dot(x_ref[...], w_ref[...], preferred_element_type=F32) + b_ref[...]
    if not rot_hd:
        (o_ref,) = rest
        o_ref[...] = (jax.nn.sigmoid(acc) if sigmoid else acc).astype(o_ref.dtype)
        return
    cs_ref, sn_ref, o_ref = rest
    h2 = rot_hd // 2
    lane = lax.broadcasted_iota(jnp.int32, (1, LANES), 1)
    first = (lane % rot_hd) < h2
    for c, kind in enumerate(kinds):
        val = acc[:, c * LANES:(c + 1) * LANES]
        if kind is not None:
            sw = jnp.where(first, pltpu.roll(val, LANES - h2, 1), pltpu.roll(val, h2, 1))
            val = val * cs_ref[kind] + sw * sn_ref[kind]
        val = val.astype(o_ref.dtype)
        if chunk_major:
            o_ref[c] = val
        else:
            o_ref[:, c * LANES:(c + 1) * LANES] = val


def _proj(xb, w, b, out_dtype, *, seq, tm, tn, rot=None, chunk_major=False, sigmoid=False,
          name="proj"):
    m, n = xb.shape[0], w.shape[1]
    assert m % tm == 0 and n % tn == 0 and seq % tm == 0
    in_specs = [
        pl.BlockSpec((tm, D_MODEL), lambda i, j: (i, 0)),
        pl.BlockSpec((D_MODEL, tn), lambda i, j: (0, j)),
        pl.BlockSpec((1, tn), lambda i, j: (0, j)),
    ]
    args = [xb, w, b.reshape(1, -1)]
    rot_hd, kinds = 0, None
    if rot is not None:
        rot_hd, cos, sin, kinds = rot
        assert len(kinds) == tn // LANES and LANES % rot_hd == 0
        spt = seq // tm
        tab = pl.BlockSpec((cos.shape[0], tm, LANES), lambda i, j: (0, i % spt, 0))
        in_specs += [tab, tab]
        args += [cos, sin]
    if chunk_major:
        assert rot is not None
        out_spec = pl.BlockSpec((tn // LANES, tm, LANES), lambda i, j: (j, i, 0))
        out_shape = jax.ShapeDtypeStruct((n // LANES, m, LANES), out_dtype)
    else:
        out_spec = pl.BlockSpec((tm, tn), lambda i, j: (i, j))
        out_shape = jax.ShapeDtypeStruct((m, n), out_dtype)
    return pl.pallas_call(
        functools.partial(_proj_kernel, rot_hd=rot_hd, kinds=kinds, chunk_major=chunk_major,
                          sigmoid=sigmoid),
        grid=(m // tm, n // tn),
        in_specs=in_specs,
        out_specs=out_spec,
        out_shape=out_shape,
        compiler_params=_cparams(("parallel", "arbitrary")),
        name=name,
    )(*args)


def _rope_tables(seq, head_dim, scales):
    half = head_dim // 2
    inv_freq = jnp.power(ROPE_THETA, -jnp.arange(half, dtype=F32) / half)
    ang = jnp.arange(seq, dtype=F32)[:, None] * inv_freq[None, :]
    lane = np.arange(LANES)
    f_idx = lane % half
    sign = np.where((lane % head_dim) < half, -1.0, 1.0).astype(np.float32)
    cos = jnp.cos(ang)[:, f_idx]
    sin = jnp.sin(ang)[:, f_idx] * sign[None, :]
    return (jnp.stack([cos * F32(s) for s in scales]), jnp.stack([sin * F32(s) for s in scales]))


def _merge_kernel(ya_ref, yb_ref, yc_ref, yd_ref, g_ref, x_ref, wb_ref, wo_ref, bo_ref, lg_ref,
                  lb_ref, of_ref, ob_ref):
    merged = None
    for i, y_ref in enumerate((ya_ref, yb_ref, yc_ref, yd_ref)):
        p = jnp.dot(y_ref[...], wb_ref[i], preferred_element_type=F32)
        gate = g_ref[:, i * D_MODEL:(i + 1) * D_MODEL].astype(F32)
        merged = gate * p if merged is None else merged + gate * p
    y = jnp.dot(merged.astype(BF16), wo_ref[...], preferred_element_type=F32) + bo_ref[...]
    z = DEEPNORM_ALPHA * x_ref[...] + y
    mu = jnp.mean(z, -1, keepdims=True)
    zc = z - mu
    var = jnp.mean(zc * zc, -1, keepdims=True)
    out = zc * lax.rsqrt(var + LN_EPS) * lg_ref[...] + lb_ref[...]
    of_ref[...] = out
    ob_ref[...] = out.astype(BF16)


def _merge(ya, yb, yc, yd, gates, x, wb, wo, bo, lg, lb, *, row0, rows, tm=512):
    assert row0 % tm == 0 and rows % tm == 0
    o = row0 // tm
    ysp = pl.BlockSpec((tm, BW), lambda i: (i + o, 0))
    const2 = lambda i: (0, 0)
    return pl.pallas_call(
        _merge_kernel,
        grid=(rows // tm,),
        in_specs=[ysp, ysp, ysp, ysp,
                  pl.BlockSpec((tm, 4 * D_MODEL), lambda i: (i + o, 0)),
                  pl.BlockSpec((tm, D_MODEL), lambda i: (i + o, 0)),
                  pl.BlockSpec((4, BW, D_MODEL), lambda i: (0, 0, 0)),
                  pl.BlockSpec((D_MODEL, D_MODEL), const2),
                  pl.BlockSpec((1, D_MODEL), const2),
                  pl.BlockSpec((1, D_MODEL), const2),
                  pl.BlockSpec((1, D_MODEL), const2)],
        out_specs=[pl.BlockSpec((tm, D_MODEL), lambda i: (i, 0)),
                   pl.BlockSpec((tm, D_MODEL), lambda i: (i, 0))],
        out_shape=[jax.ShapeDtypeStruct((rows, D_MODEL), F32),
                   jax.ShapeDtypeStruct((rows, D_MODEL), BF16)],
        compiler_params=_cparams(("parallel",)),
        name="merge",
    )(ya, yb, yc, yd, gates, x, wb, wo, bo.reshape(1, -1), lg.reshape(1, -1), lb.reshape(1, -1))


NA_QR = 4


def _na_geometry(rows):
    kr, wr = min(NA_ROWS, rows), min(NA_ROWS, rows) + NA_QR - 1
    assert rows % NA_QR == 0 and rows >= wr and kr == NA_ROWS
    steps = rows // NA_QR
    a = np.arange(NA_QR)[:, None]
    j = np.arange(wr)[None, :]
    pats = []
    for g in range(steps):
        r = NA_QR * g + a
        rs = np.clip(r - kr // 2, 0, rows - kr)
        ws = int(np.clip(NA_QR * g - kr // 2, 0, rows - wr))
        assert rs.min() >= ws and rs.max() + kr <= ws + wr
        valid = (ws + j >= rs) & (ws + j < rs + kr)
        dr = np.clip(ws + j - r + NA_ROWS - 1, 0, 2 * NA_ROWS - 2)
        pats.append((valid, np.where(valid, dr, 0)))
    same = lambda x, y: np.array_equal(x[0], y[0]) and np.array_equal(x[1], y[1])
    assert steps >= 3 and all(same(pats[g], pats[1]) for g in range(1, steps - 1))
    return kr, wr, steps, [pats[0], pats[1], pats[steps - 1]]


def _na_bias_tables(rpb, rows):
    kr, wr, steps, pats = _na_geometry(rows)
    qc = np.arange(GRID_W)[:, None]
    kc = np.arange(GRID_W)[None, :]
    c_start = np.clip(qc - NA_COLS // 2, 0, GRID_W - NA_COLS)
    col_ok = (kc >= c_start) & (kc < c_start + NA_COLS)
    dc = np.clip(kc - qc + NA_COLS - 1, 0, 2 * NA_COLS - 2)
    tabs = []
    for valid, dr in pats:
        b = rpb.astype(F32)[:, dr][:, :, :, dc]
        ok = valid[:, :, None, None] & col_ok[None, None]
        b = jnp.where(ok[None], b, MASKED)
        tabs.append(b.transpose(0, 1, 3, 2, 4).reshape(NA_HEADS, NA_QR * GRID_W, wr * GRID_W))
    return jnp.stack(tabs, axis=1)


def _stack_heads(q, n_heads, head_dim):
    lane = lax.broadcasted_iota(jnp.int32, (1, q.shape[1]), 1)
    zero = jnp.zeros_like(q)
    return jnp.concatenate(
        [jnp.where((lane >= h * head_dim) & (lane < (h + 1) * head_dim), q, zero)
         for h in range(n_heads)], axis=0)


def _unstack_heads(o, n_heads, head_dim):
    m = o.shape[0] // n_heads
    lane = lax.broadcasted_iota(jnp.int32, (1, o.shape[1]), 1)
    acc = o[0:m]
    for h in range(1, n_heads):
        in_head = (lane >= h * head_dim) & (lane < (h + 1) * head_dim)
        acc = jnp.where(in_head, o[h * m:(h + 1) * m], acc)
    return acc


def _na_kernel(q_ref, k_ref, v_ref, g_ref, bias_ref, o_ref, *, rows):
    kr, wr = min(NA_ROWS, rows), min(NA_ROWS, rows) + NA_QR - 1
    g = pl.program_id(1)
    ws = jnp.clip(NA_QR * g - kr // 2, 0, rows - wr)
    start = pl.multiple_of(ws * GRID_W, GRID_W)
    k = k_ref[pl.ds(start, wr * GRID_W), :]
    v = v_ref[pl.ds(start, wr * GRID_W), :]
    qs = _stack_heads(q_ref[...], NA_HEADS, NA_HEAD_DIM)
    m_rows = q_ref.shape[0]
    halves = [slice(0, 2 * m_rows), slice(2 * m_rows, 4 * m_rows)]
    ss = [_dot_nt(qs[h], k) for h in halves]
    ps, ls = [], []
    for i, s in enumerate(ss):
        s = s + bias_ref[2 * i:2 * i + 2].reshape(s.shape)
        p = jnp.exp(s - jnp.max(s, -1, keepdims=True))
        ls.append(jnp.sum(p, -1, keepdims=True))
        ps.append(p.astype(BF16))
    o = jnp.concatenate([jnp.dot(p, v, preferred_element_type=F32) / l for p, l in zip(ps, ls)], 0)
    o = _unstack_heads(o, NA_HEADS, NA_HEAD_DIM)
    o_ref[...] = (o * jax.nn.silu(g_ref[...])).astype(o_ref.dtype)


def _na(hb, fb, bias_tab, *, batch, seq):
    rows = seq // GRID_W
    kr, wr, steps, _ = _na_geometry(rows)
    qb = NA_QR * GRID_W

    def pattern_of(g):
        return jnp.where(g == 0, 0, jnp.where(g == steps - 1, 2, 1))

    return pl.pallas_call(
        functools.partial(_na_kernel, rows=rows),
        grid=(batch, steps),
        in_specs=[
            pl.BlockSpec((None, qb, BW), lambda b, g: (b, g, 0)),
            pl.BlockSpec((None, seq, BW), lambda b, g: (b, 0, 1)),
            pl.BlockSpec((None, seq, BW), lambda b, g: (b, 0, 2)),
            pl.BlockSpec((None, qb, BW), lambda b, g: (b, g, 5)),
            pl.BlockSpec((NA_HEADS, None, qb, wr * GRID_W), lambda b, g: (0, pattern_of(g), 0, 0)),
        ],
        out_specs=pl.BlockSpec((None, qb, BW), lambda b, g: (b, g, 0)),
        out_shape=jax.ShapeDtypeStruct((batch, seq, BW), BF16),
        compiler_params=_cparams(("parallel", "arbitrary")),
        name="na_attn",
    )(hb, hb, hb, fb, bias_tab)


RW_C = 64
RW_FIELDS = 9


def _mm(a, b, mode="nn", passes=1):
    dims = {"nn": (((1,), (0,)), ((), ())), "nt": (((1,), (1,)), ((), ()))}[mode]
    dg = lambda x, y: lax.dot_general(x, y, dims, preferred_element_type=F32)
    ah, bh = a.astype(BF16), b.astype(BF16)
    out = dg(ah, bh)
    if passes == 3:
        al = (a - ah.astype(F32)).astype(BF16)
        bl = (b - bh.astype(F32)).astype(BF16)
        out = out + (dg(al, bh) + dg(ah, bl))
    return out


def _softplus(x):
    return jnp.maximum(x, 0.0) + jnp.log(1.0 + jnp.exp(-jnp.abs(x)))


def _rwprep_kernel(x_ref, prev_ref, next_ref, mu_ref, w0_ref, w2_ref, a0_ref, a2_ref, kk_ref, ka_ref,
                   rk_ref, f_ref, e_ref, *, rc):
    c = pl.program_id(1)
    lane = lax.broadcasted_iota(jnp.int32, (1, 2 * RW_LORA), 1)
    ri = lax.broadcasted_iota(jnp.int32, (rc, 1), 0)
    x = x_ref[...]
    prev_row = jnp.where(c > 0, prev_ref[7:8, :], 0.0)
    next_row = jnp.where(c < pl.num_programs(1) - 1, next_ref[0:1, :], 0.0)
    up = jnp.where(ri == 0, prev_row, pltpu.roll(x, 1, 0))
    dn = jnp.where(ri == rc - 1, next_row, pltpu.roll(x, rc - 1, 0))
    xs = x + mu_ref[...] * (0.5 * (up + dn) - x)
    r, k, v, g = (xs[:, i * BW:(i + 1) * BW] for i in range(4))
    wl = xs[:, 4 * BW:4 * BW + 2 * RW_LORA]
    al = xs[:, 4 * BW + 2 * RW_LORA:4 * BW + 4 * RW_LORA]
    kk = k * kk_ref[...]
    kap = kk * lax.rsqrt(jnp.maximum(_seg_sum(kk * kk, RW_HEAD_DIM), 1e-24))
    tw = jnp.tanh(wl)
    kd_sum = None
    for d in range(2):
        dm = ((lane >= d * RW_LORA) & (lane < (d + 1) * RW_LORA)).astype(F32)
        w_raw = w0_ref[d:d + 1, :] + _mm(tw * dm, w2_ref[...], passes=3)
        lw = -jnp.exp(-_softplus(-w_raw) - 0.5)
        a = jax.nn.sigmoid(a0_ref[d:d + 1, :] + _mm(al * dm, a2_ref[...], passes=3))
        kd = k * (1.0 + (a - 1.0) * ka_ref[...])
        f_ref[:, (3 + 3 * d) * BW:(4 + 3 * d) * BW] = lw
        f_ref[:, (4 + 3 * d) * BW:(5 + 3 * d) * BW] = kd
        f_ref[:, (5 + 3 * d) * BW:(6 + 3 * d) * BW] = kap * a
        kd_sum = kd if kd_sum is None else kd_sum + kd
    f_ref[:, 0:BW] = r
    f_ref[:, BW:2 * BW] = v
    f_ref[:, 2 * BW:3 * BW] = kap
    bonus = _seg_sum(r * kd_sum * rk_ref[...], RW_HEAD_DIM)
    e_ref[:, 0:BW] = bonus * v
    e_ref[:, BW:2 * BW] = jax.nn.silu(g)


def _rwprep(fb, mu, w0, w2, a0, a2, k_k, k_a, r_k, *, batch, seq, rc=256):
    ucols = 4 * BW + 4 * RW_LORA
    row = lambda z: z.astype(F32).reshape(1, -1)
    c2 = lambda b, c: (0, 0)
    tpb = rc // 8
    return pl.pallas_call(
        functools.partial(_rwprep_kernel, rc=rc),
        grid=(batch, seq // rc),
        in_specs=[
            pl.BlockSpec((None, rc, ucols), lambda b, c: (b, c, 0)),
            pl.BlockSpec((None, 8, ucols), lambda b, c: (b, jnp.maximum(c * tpb - 1, 0), 0)),
            pl.BlockSpec((None, 8, ucols),
                         lambda b, c: (b, jnp.minimum((c + 1) * tpb, seq // 8 - 1), 0)),
            pl.BlockSpec((1, ucols), c2),
            pl.BlockSpec((2, BW), c2),
            pl.BlockSpec((2 * RW_LORA, BW), c2),
            pl.BlockSpec((2, BW), c2),
            pl.BlockSpec((2 * RW_LORA, BW), c2),
            pl.BlockSpec((1, BW), c2),
            pl.BlockSpec((1, BW), c2),
            pl.BlockSpec((1, BW), c2),
        ],
        out_specs=[pl.BlockSpec((None, rc, RW_FIELDS * BW), lambda b, c: (b, c, 0)),
                   pl.BlockSpec((None, rc, 2 * BW), lambda b, c: (b, c, 0))],
        out_shape=[jax.ShapeDtypeStruct((batch, seq, RW_FIELDS * BW), F32),
                   jax.ShapeDtypeStruct((batch, seq, 2 * BW), F32)],
        compiler_params=_cparams(("parallel", "arbitrary")),
        name="rwkv_prep",
    )(fb, fb, fb, row(mu), w0.astype(F32), w2.astype(F32).reshape(2 * RW_LORA, BW), a0.astype(F32),
      a2.astype(F32).reshape(2 * RW_LORA, BW), row(k_k), row(k_a), row(r_k))


def _rw_bd(x):
    lane = lax.broadcasted_iota(jnp.int32, (1, BW), 1)
    xb = x.astype(BF16)
    zero = jnp.zeros_like(xb)
    return jnp.concatenate(
        [jnp.where((lane >= h * RW_C) & (lane < (h + 1) * RW_C), xb, zero) for h in range(RW_HEADS)],
        axis=0)


def _rw_advance(terms, z):
    gm, hv, nt = terms
    both = _mm(gm, z)
    return both[0:RW_C] + hv, _rw_bd(both[RW_C:2 * RW_C] + nt)


def _rw_prepare(specs):
    C, H = RW_C, RW_HEADS
    assert H * C == BW and RW_HEAD_DIM == C
    bd = _rw_bd
    cat = lambda a, b: jnp.concatenate([a, b], axis=0)
    ri = lax.broadcasted_iota(jnp.int32, (C, C), 0)
    ci = lax.broadcasted_iota(jnp.int32, (C, C), 1)
    tris = [(ci <= ri).astype(BF16), (ci >= ri).astype(BF16)]
    row = lax.broadcasted_iota(jnp.int32, (C, BW), 0)
    col = lax.broadcasted_iota(jnp.int32, (C, BW), 1) % C
    eye = col == row
    strict = [col < row, col > row]
    incl = [col <= row, col >= row]

    def lc_t(x):
        t = bd(x).astype(F32).T
        return t[0:C] + t[C:2 * C] + t[2 * C:3 * C] + t[3 * C:4 * C]

    def load(spec):
        f_ref, r0, d = spec
        fld = lambda i: f_ref[r0:r0 + C, i * BW:(i + 1) * BW]
        c = dict(d=d, r=fld(0), v=fld(1), kap=fld(2), lw=fld(3 + 3 * d), kd=fld(4 + 3 * d),
                 beta=fld(5 + 3 * d))
        cs = None
        for t in _split3(c["lw"]):
            y = jnp.dot(tris[d], t, preferred_element_type=F32)
            cs = y if cs is None else cs + y
        c["cs"] = cs
        return c

    def scale(c):
        cs, d = c["cs"], c["d"]
        tot = cs[0:1] if d == 1 else cs[C - 1:C]
        ginv = jnp.exp(-cs)
        gto = jnp.exp(tot - cs)
        kt = c["kap"] * jnp.exp(cs - c["lw"])
        rt = c["r"] * jnp.exp(cs)
        c.update(gc=jnp.exp(tot), rt=rt, ktrt=cat(kt, rt), kt_b=bd(kt), v_b=bd(c["v"]),
                 kg_b=bd(c["kd"] * ginv), bg_b=bd(c["beta"] * ginv),
                 kh=c["kd"] * gto, bh=c["beta"] * gto)
        return c

    def gram(c):
        d = c["d"]
        gk = _mm(c["ktrt"], c["kg_b"], "nt")
        gb = _mm(c["ktrt"], c["bg_b"], "nt")
        c.update(l_kk=jnp.where(strict[d], gk[0:C], 0.0), l_kr=jnp.where(incl[d], gk[C:2 * C], 0.0),
                 l_br=jnp.where(incl[d], gb[C:2 * C], 0.0),
                 npow=-jnp.where(strict[d], gb[0:C], 0.0), w=jnp.where(eye, 1.0, 0.0))
        return c

    def level(c, last):
        if last:
            c["w"] = c["w"] + _mm(c["w"], bd(c["npow"]))
        else:
            both = _mm(cat(c["npow"], c["w"]), bd(c["npow"]))
            c["npow"], c["w"] = both[0:C], c["w"] + both[C:2 * C]
        return c

    def solve(c):
        c["p1_b"] = bd(_mm(c["w"], c["kt_b"]))
        c["p2"] = _mm(c["w"], bd(c["l_kk"]))
        c["p2_b"] = bd(c["p2"])
        return c

    def outputs(c):
        c["g"] = c["rt"] - _mm(c["l_br"], c["p1_b"])
        hm = c["l_kr"] - _mm(c["l_br"], c["p2_b"])
        both = _mm(cat(hm, c["p2"]), c["v_b"])
        c["hv"], c["p2v_b"] = both[0:C], bd(both[C:2 * C])
        return c

    def state(c):
        bh_t = lc_t(c["bh"])
        mt = jnp.where(eye, c["gc"], 0.0) - _mm(bh_t, c["p1_b"])
        nt = _mm(lc_t(c["kh"]), c["v_b"]) - _mm(bh_t, c["p2v_b"])
        return cat(c["g"], mt).astype(BF16), c["hv"], nt

    cs = [load(s) for s in specs]
    cs = [scale(c) for c in cs]
    cs = [gram(c) for c in cs]
    levels = int(math.log2(C))
    for lvl in range(levels):
        cs = [level(c, lvl == levels - 1) for c in cs]
    cs = [solve(c) for c in cs]
    cs = [outputs(c) for c in cs]
    return [state(c) for c in cs]


def _rwscan_kernel(ff_ref, fr_ref, yf_ref, yr_ref, zf_ref, zr_ref, *, nch):
    @pl.when(pl.program_id(1) == 0)
    def _():
        zf_ref[...] = jnp.zeros_like(zf_ref)
        zr_ref[...] = jnp.zeros_like(zr_ref)

    C = RW_C
    pre = _rw_prepare([(ff_ref, j * C, 0) for j in range(nch)]
                      + [(fr_ref, j * C, 1) for j in range(nch)])
    pre_f, pre_r = pre[:nch], pre[nch:]
    zf, zr = zf_ref[...], zr_ref[...]
    for j in range(nch):
        yf_ref[j * C:(j + 1) * C, :], zf = _rw_advance(pre_f[j], zf)
        jr = nch - 1 - j
        yr_ref[jr * C:(jr + 1) * C, :], zr = _rw_advance(pre_r[jr], zr)
    zf_ref[...] = zf
    zr_ref[...] = zr


def _rwscan(f, *, batch, seq, nch=4):
    rb = nch * RW_C
    nc = seq // rb
    fsp = lambda imap: pl.BlockSpec((None, rb, RW_FIELDS * BW), imap)
    ysp = lambda imap: pl.BlockSpec((None, rb, BW), imap)
    fwd = lambda b, s: (b, s, 0)
    rev = lambda b, s: (b, nc - 1 - s, 0)
    y = jax.ShapeDtypeStruct((batch, seq, BW), F32)
    return pl.pallas_call(
        functools.partial(_rwscan_kernel, nch=nch),
        grid=(batch, nc),
        in_specs=[fsp(fwd), fsp(rev)],
        out_specs=[ysp(fwd), ysp(rev)],
        out_shape=[y, y],
        scratch_shapes=[pltpu.VMEM((BW, BW), BF16), pltpu.VMEM((BW, BW), BF16)],
        compiler_params=_cparams(("parallel", "arbitrary")),
        name="rwkv_scan",
    )(f, f)


def _rwpost_kernel(yf_ref, yr_ref, e_ref, lg_ref, lb_ref, o_ref):
    y = yf_ref[...] + yr_ref[...]
    inv = 1.0 / RW_HEAD_DIM
    yc = y - _seg_sum(y, RW_HEAD_DIM) * inv
    var = _seg_sum(yc * yc, RW_HEAD_DIM) * inv
    yn = yc * lax.rsqrt(var + RW_LNX_EPS) * lg_ref[...] + lb_ref[...]
    o_ref[...] = ((yn + e_ref[:, 0:BW]) * e_ref[:, BW:2 * BW]).astype(o_ref.dtype)


def _rwpost(yf, yr, e, lnx_g, lnx_b, *, tm=512):
    m = yf.shape[0]
    row = lambda z: z.astype(F32).reshape(1, -1)
    ysp = pl.BlockSpec((tm, BW), lambda i: (i, 0))
    return pl.pallas_call(
        _rwpost_kernel,
        grid=(m // tm,),
        in_specs=[ysp, ysp, pl.BlockSpec((tm, 2 * BW), lambda i: (i, 0)),
                  pl.BlockSpec((1, BW), lambda i: (0, 0)), pl.BlockSpec((1, BW), lambda i: (0, 0))],
        out_specs=ysp,
        out_shape=jax.ShapeDtypeStruct((m, BW), BF16),
        compiler_params=_cparams(("parallel",)),
        name="rwkv_post",
    )(yf, yr, e, row(lnx_g), row(lnx_b))


def _df_kernel(lam_ref, sg_ref, q_ref, k_ref, v_ref, g_ref, o_ref, *, lam_init):
    lam = lam_ref[...]
    e1 = jnp.exp(jnp.sum(lam[0:1] * lam[1:2], -1, keepdims=True))
    e2 = jnp.exp(jnp.sum(lam[2:3] * lam[3:4], -1, keepdims=True))
    lam_full = e1 - e2 + lam_init
    d = DF_HEAD_DIM
    tq = q_ref.shape[0]
    qs = _stack_heads(q_ref[...], 2 * DF_HEADS, d)
    k, v = k_ref[...], v_ref[...]
    ss = [_dot_nt(qs[2 * h * tq:2 * (h + 1) * tq], k) for h in range(DF_HEADS)]
    ws, r1s = [], []
    for s in ss:
        e = jnp.exp2(s - jnp.max(s, -1, keepdims=True))
        l = jnp.sum(e, -1, keepdims=True)
        r1 = 1.0 / l[0:tq]
        ws.append((e[0:tq] - e[tq:2 * tq] * (lam_full * l[0:tq] / l[tq:2 * tq])).astype(BF16))
        r1s.append(r1)
    os_ = [jnp.dot(w, v, preferred_element_type=F32) * r1 for w, r1 in zip(ws, r1s)]
    lane = lax.broadcasted_iota(jnp.int32, (1, BW), 1)
    acc = None
    for h, o in enumerate(os_):
        in_head = (lane >= 2 * h * d) & (lane < 2 * (h + 1) * d)
        acc = o if acc is None else jnp.where(in_head, o, acc)
    ms = _seg_sum(acc * acc, 2 * d) * (1.0 / (2 * d))
    o = acc * lax.rsqrt(ms + DF_EPS) * sg_ref[...] * (1.0 - lam_init)
    o_ref[...] = (o * jax.nn.silu(g_ref[...])).astype(o_ref.dtype)


def _df(cqk, hb, fb, lam, subln_g, lam_init, *, batch, seq, tq=256):
    sg = jnp.tile(subln_g.astype(F32), DF_HEADS).reshape(1, BW)
    return pl.pallas_call(
        functools.partial(_df_kernel, lam_init=lam_init),
        grid=(batch, seq // tq),
        in_specs=[
            pl.BlockSpec((4, DF_HEAD_DIM), lambda b, j: (0, 0)),
            pl.BlockSpec((1, BW), lambda b, j: (0, 0)),
            pl.BlockSpec((None, tq, BW), lambda b, j: (b, j, 0)),
            pl.BlockSpec((None, seq, BW), lambda b, j: (b, 0, 1)),
            pl.BlockSpec((None, seq, BW), lambda b, j: (b, 0, 3)),
            pl.BlockSpec((None, tq, BW), lambda b, j: (b, j, 6)),
        ],
        out_specs=pl.BlockSpec((None, tq, BW), lambda b, j: (b, j, 0)),
        out_shape=jax.ShapeDtypeStruct((batch, seq, BW), BF16),
        compiler_params=_cparams(("parallel", "arbitrary")),
        name="diff_attn",
    )(lam.astype(F32), sg, cqk, cqk, hb, fb)


DL_QB = 2 * DL_HALF
DL_UNROLL = 4


def _dl_group(qkv_ref, qs, kpad, vpad, oacc, lacc, gidx, dil, seq):
    L = seq // dil
    assert L % DL_QB == 0 and (seq // DL_QB) % DL_UNROLL == 0
    nblk = L // DL_QB
    seg = L + 2 * DL_HALF
    zpad = jnp.zeros((DL_HALF, BW), BF16)

    def rows_of(rho):
        return pl.ds(rho, L, stride=dil) if dil > 1 else pl.ds(0, L)

    def stage(rho, carry):
        qrow = pl.multiple_of(rho * L, DL_QB)
        krow = pl.multiple_of(rho * seg, DL_HALF)
        for ref in (kpad, vpad):
            ref[pl.ds(krow, DL_HALF), :] = zpad
            ref[pl.ds(krow + DL_HALF + L, DL_HALF), :] = zpad
        for c in range(BW // LANES):
            cols = slice(c * LANES, (c + 1) * LANES)
            qs[pl.ds(qrow, L), cols] = qkv_ref[c, rows_of(rho), :].astype(BF16)
            kpad[pl.ds(krow + DL_HALF, L), cols] = qkv_ref[2 + c, rows_of(rho), :].astype(BF16)
            vpad[pl.ds(krow + DL_HALF, L), cols] = qkv_ref[4 + c, rows_of(rho), :].astype(BF16)
        return carry

    lax.fori_loop(0, dil, stage, 0)

    hq = DL_HEADS * DL_QB
    qi = lax.broadcasted_iota(jnp.int32, (hq, 2 * DL_QB), 0) % DL_QB
    ji = lax.broadcasted_iota(jnp.int32, (hq, 2 * DL_QB), 1)
    band = jnp.where(jnp.abs(ji - DL_HALF - qi) <= DL_HALF, 0.0, MASKED)
    jrow = lax.broadcasted_iota(jnp.int32, (1, 2 * DL_QB), 1)
    lane = lax.broadcasted_iota(jnp.int32, (1, BW), 1)

    def blocks(it, carry):
        fs = [it * DL_UNROLL + u for u in range(DL_UNROLL)]
        rhos = [f // nblk for f in fs]
        ns = [f % nblk for f in fs]
        qst = [_stack_heads(qs[pl.ds(pl.multiple_of(f * DL_QB, DL_QB), DL_QB), :], DL_HEADS,
                            DL_HEAD_DIM) for f in fs]
        krows = [pl.multiple_of(rho * seg + n * DL_QB, DL_HALF) for rho, n in zip(rhos, ns)]
        ss = [_dot_nt(q, kpad[pl.ds(kr, 2 * DL_QB), :]) for q, kr in zip(qst, krows)]
        ps, lses, dens = [], [], []
        for s, n in zip(ss, ns):
            kpos = n * DL_QB - DL_HALF + jrow
            s = s + band + jnp.where((kpos >= 0) & (kpos < L), 0.0, MASKED)
            m = jnp.max(s, -1, keepdims=True)
            p = jnp.exp(s - m)
            den = jnp.sum(p, -1, keepdims=True)
            ps.append(p.astype(BF16))
            dens.append(den)
            lses.append(m + jnp.log(den))
        os_ = [jnp.dot(p, vpad[pl.ds(kr, 2 * DL_QB), :], preferred_element_type=F32) / den
               for p, kr, den in zip(ps, krows, dens)]
        for o, lse, rho, n in zip(os_, lses, rhos, ns):
            acc_o = _unstack_heads(o, DL_HEADS, DL_HEAD_DIM)
            acc_l = _unstack_heads(jnp.broadcast_to(lse, (hq, BW)), DL_HEADS, DL_HEAD_DIM)
            if dil > 1:
                dst = pl.ds(rho + dil * n * DL_QB, DL_QB, stride=dil)
            else:
                dst = pl.ds(pl.multiple_of(n * DL_QB, DL_QB), DL_QB)
            for c in range(BW // LANES):
                oacc[gidx, c, dst, :] = acc_o[:, c * LANES:(c + 1) * LANES]
                lacc[gidx, c, dst, :] = acc_l[:, c * LANES:(c + 1) * LANES]
        return carry

    lax.fori_loop(0, seq // DL_QB // DL_UNROLL, blocks, 0)


def _dl_kernel(qkv_ref, g_ref, o_ref, qs, kpad, vpad, oacc, lacc, *, seq):
    gi = pl.program_id(1)
    for gidx, dil in enumerate(DL_DILATIONS):
        @pl.when(gi == gidx)
        def _(gidx=gidx, dil=dil):
            _dl_group(qkv_ref, qs, kpad, vpad, oacc, lacc, gidx, dil, seq)

    @pl.when(gi == len(DL_DILATIONS) - 1)
    def _():
        rc = 256

        def chunk(c, carry):
            rows = pl.ds(pl.multiple_of(c * rc, rc), rc)
            for lc in range(BW // LANES):
                cols = slice(lc * LANES, (lc + 1) * LANES)
                ls = [lacc[g, lc, rows, :] for g in range(len(DL_DILATIONS))]
                mx = functools.reduce(jnp.maximum, ls)
                ws = [jnp.exp(l - mx) for l in ls]
                num = sum(w * oacc[g, lc, rows, :] for g, w in enumerate(ws))
                out = num / sum(ws)
                o_ref[rows, cols] = (out * jax.nn.silu(g_ref[rows, cols])).astype(o_ref.dtype)
            return carry

        lax.fori_loop(0, seq // rc, chunk, 0)


def _dl(dqkv, fb, *, batch, seq):
    ng = len(DL_DILATIONS)
    nch = 3 * BW // LANES
    assert all(w // (2 * d) == DL_HALF for w, d in zip(DL_WINDOWS, DL_DILATIONS))
    return pl.pallas_call(
        functools.partial(_dl_kernel, seq=seq),
        grid=(batch, ng),
        in_specs=[
            pl.BlockSpec((nch, None, seq, LANES), lambda b, g: (g, b, 0, 0)),
            pl.BlockSpec((None, seq, BW), lambda b, g: (b, 0, 7)),
        ],
        out_specs=pl.BlockSpec((None, seq, BW), lambda b, g: (b, 0, 0)),
        out_shape=jax.ShapeDtypeStruct((batch, seq, BW), BF16),
        scratch_shapes=[
            pltpu.VMEM((seq, BW), BF16),
            pltpu.VMEM((seq + 2 * DL_HALF * max(DL_DILATIONS), BW), BF16),
            pltpu.VMEM((seq + 2 * DL_HALF * max(DL_DILATIONS), BW), BF16),
            pltpu.VMEM((ng, BW // LANES, seq, LANES), F32),
            pltpu.VMEM((ng, BW // LANES, seq, LANES), F32),
        ],
        compiler_params=_cparams(("parallel", "arbitrary")),
        name="dilated_attn",
    )(dqkv, fb)


_A0 = 0
_B0 = _A0 + 4 * BW
_C0 = _B0 + 4 * BW + 4 * RW_LORA
_D0 = _C0 + 4 * BW
_DG0 = _D0 + 9 * BW
_G0 = _DG0 + BW
PROJ_TM = 2048


def _layer(xf, xb, l, p, tabs, *, batch, seq, last_split):
    w, b = p["w_in"][l], p["b_in"][l]
    cols = lambda lo, n: (w[:, lo:lo + n], b[lo:lo + n])
    cat = lambda parts: (jnp.concatenate([q[0] for q in parts], 1).astype(BF16),
                         jnp.concatenate([q[1] for q in parts], 0).astype(F32))
    scaled = lambda part, s: (part[0] * s, part[1] * s)
    q_scale = NA_HEAD_DIM ** -0.5
    proj = functools.partial(_proj, xb, seq=seq, tm=PROJ_TM)

    gates = proj(*cat([cols(_G0, 4 * D_MODEL)]), BF16, tn=1024, sigmoid=True, name="proj_gates")
    fb = proj(*cat([cols(_B0, _C0 - _B0), cols(_A0 + 3 * BW, BW), cols(_C0 + 3 * BW, BW),
                    cols(_DG0, BW)]), F32, tn=1024, name="proj_f32")
    hb = proj(*cat([scaled(cols(_A0, BW), q_scale), cols(_A0 + BW, 2 * BW), cols(_C0 + 2 * BW, BW)]),
              BF16, tn=1024, name="proj_bf16")
    cpb = BW // LANES
    cqk = proj(*cat([cols(_C0, 2 * BW)]), BF16, tn=2 * BW, name="proj_rot_c",
               rot=(DF_HEAD_DIM,) + tabs["c"] + ((0,) * cpb + (1,) * cpb,))
    dparts = []
    for g in range(len(DL_DILATIONS)):
        dparts += [scaled(cols(_D0 + 3 * g * BW, BW), q_scale), cols(_D0 + (3 * g + 1) * BW, 2 * BW)]
    dq = proj(*cat(dparts), F32, tn=3 * BW, chunk_major=True, name="proj_rot_d",
              rot=(DL_HEAD_DIM,) + tabs["d"] + ((0,) * (2 * cpb) + (None,) * cpb,))

    r3 = lambda z: z.reshape(batch, seq, z.shape[-1])
    hb3, fb3 = r3(hb), r3(fb)
    ya = _na(hb3, fb3, _na_bias_tables(p["na_rpb"][l], seq // GRID_W), batch=batch, seq=seq)
    f, e = _rwprep(fb3, p["rw_mu"][l], p["rw_w0"][l], p["rw_w2"][l], p["rw_a0"][l], p["rw_a2"][l],
                   p["rw_kk"][l], p["rw_ka"][l], p["rw_rk"][l].reshape(-1), batch=batch, seq=seq)
    yf, yr = _rwscan(f, batch=batch, seq=seq)
    m = batch * seq
    yb = _rwpost(yf.reshape(m, BW), yr.reshape(m, BW), e.reshape(m, 2 * BW), p["rw_lnx_g"][l],
                 p["rw_lnx_b"][l])
    lam_init = 0.8 - 0.6 * math.exp(-0.3 * l)
    yc = _df(r3(cqk), hb3, fb3, p["df_lam"][l], p["df_subln_g"][l], lam_init, batch=batch, seq=seq)
    yd = _dl(dq.reshape(dq.shape[0], batch, seq, LANES), fb3, batch=batch, seq=seq)

    margs = (ya.reshape(m, BW), yb, yc.reshape(m, BW), yd.reshape(m, BW), gates, xf,
             p["w_branch"][l].astype(BF16), p["w_out"][l].astype(BF16), p["b_out"][l],
             p["ln_g"][l], p["ln_b"][l])
    if last_split is None:
        return _merge(*margs, row0=0, rows=m)
    return [_merge(*margs, row0=r0, rows=n) for r0, n in last_split]


def kernel(x_prompt, x_sample, ln0_g, ln0_b, w_in, b_in, na_rpb, rw_mu, rw_w0, rw_w2, rw_a0, rw_a2,
           rw_kk, rw_ka, rw_rk, rw_lnx_g, rw_lnx_b, df_lam, df_subln_g, w_branch, w_out, b_out,
           ln_g, ln_b):
    p = dict(w_in=w_in, b_in=b_in, na_rpb=na_rpb, rw_mu=rw_mu, rw_w0=rw_w0, rw_w2=rw_w2,
             rw_a0=rw_a0, rw_a2=rw_a2, rw_kk=rw_kk, rw_ka=rw_ka, rw_rk=rw_rk, rw_lnx_g=rw_lnx_g,
             rw_lnx_b=rw_lnx_b, df_lam=df_lam, df_subln_g=df_subln_g, w_branch=w_branch,
             w_out=w_out, b_out=b_out, ln_g=ln_g, ln_b=ln_b)
    bp, seq, _ = x_prompt.shape
    bs = x_sample.shape[0]
    assert x_sample.shape[1] == seq
    batch = bp + bs
    xf, xb = _ln0(x_prompt.reshape(bp * seq, D_MODEL), x_sample.reshape(bs * seq, D_MODEL),
                  ln0_g, ln0_b)
    tabs = {"c": _rope_tables(seq, DF_HEAD_DIM, (DF_HEAD_DIM ** -0.5 * math.log2(math.e), 1.0)),
            "d": _rope_tables(seq, DL_HEAD_DIM, (1.0,))}
    for l in range(DEPTH - 1):
        xf, xb = _layer(xf, xb, l, p, tabs, batch=batch, seq=seq, last_split=None)
    split = [(0, bp * seq), (bp * seq, bs * seq)]
    (yp, _), (ys, _) = _layer(xf, xb, DEPTH - 1, p, tabs, batch=batch, seq=seq, last_split=split)
    return yp.reshape(bp, seq, D_MODEL), ys.reshape(bs, seq, D_MODEL)
```

```python
import functools
import math

import numpy as np
import jax
import jax.numpy as jnp
from jax import lax
from jax.experimental import pallas as pl
from jax.experimental.pallas import tpu as pltpu

F32 = jnp.float32
BF16 = jnp.bfloat16

D_MODEL = 1024
DEPTH = 2
GRID_W = 64
NA_HEADS, NA_HEAD_DIM, NA_ROWS, NA_COLS = 4, 64, 8, 16
RW_HEADS, RW_HEAD_DIM, RW_LORA = 4, 64, 64
RW_LNX_EPS = 64e-5
DF_HEADS, DF_HEAD_DIM, DF_EPS = 4, 32, 1e-5
DL_HEADS, DL_HEAD_DIM = 4, 64
DL_DILATIONS = (1, 4, 16)
DL_WINDOWS = (128, 512, 2048)
DL_HALF = 64
BW = 256
ROPE_THETA = 10000.0
LN_EPS = 1e-5
DEEPNORM_ALPHA = (2 * DEPTH) ** 0.25
MASKED = -float("inf")

LANES = 128
VMEM_LIMIT = 56 * 1024 * 1024


def _cparams(sem):
    return pltpu.CompilerParams(dimension_semantics=sem, vmem_limit_bytes=VMEM_LIMIT)


def _dot_nt(a, b):
    return lax.dot_general(a, b, (((1,), (1,)), ((), ())), preferred_element_type=F32)


def _split2(x):
    hi = x.astype(BF16)
    return hi, (x - hi.astype(F32)).astype(BF16)


def _seg_sum(x, seg):
    n = x.shape[-1]
    r = lax.broadcasted_iota(jnp.int32, (n, n), 0) // seg
    c = lax.broadcasted_iota(jnp.int32, (n, n), 1) // seg
    ones = (r == c).astype(BF16)
    out = None
    for t in _split2(x):
        y = jnp.dot(t, ones, preferred_element_type=F32)
        out = y if out is None else out + y
    return out


def _ln0_kernel(xp_ref, xs_ref, g_ref, b_ref, of_ref, ob_ref, *, n_prompt):
    def norm(x):
        mu = jnp.mean(x, -1, keepdims=True)
        xc = x - mu
        var = jnp.mean(xc * xc, -1, keepdims=True)
        y = xc * lax.rsqrt(var + LN_EPS) * g_ref[...] + b_ref[...]
        of_ref[...] = y
        ob_ref[...] = y.astype(BF16)

    i = pl.program_id(0)

    @pl.when(i < n_prompt)
    def _():
        norm(xp_ref[...])

    @pl.when(i >= n_prompt)
    def _():
        norm(xs_ref[...])


def _ln0(xp, xs, g, b, tm=512):
    mp, ms = xp.shape[0], xs.shape[0]
    n_p, n_s = mp // tm, ms // tm
    out = jax.ShapeDtypeStruct((mp + ms, D_MODEL), F32)
    outb = jax.ShapeDtypeStruct((mp + ms, D_MODEL), BF16)
    return pl.pallas_call(
        functools.partial(_ln0_kernel, n_prompt=n_p),
        grid=(n_p + n_s,),
        in_specs=[
            pl.BlockSpec((tm, D_MODEL), lambda i: (jnp.minimum(i, n_p - 1), 0)),
            pl.BlockSpec((tm, D_MODEL), lambda i: (jnp.maximum(i - n_p, 0), 0)),
            pl.BlockSpec((1, D_MODEL), lambda i: (0, 0)),
            pl.BlockSpec((1, D_MODEL), lambda i: (0, 0)),
        ],
        out_specs=[pl.BlockSpec((tm, D_MODEL), lambda i: (i, 0)),
                   pl.BlockSpec((tm, D_MODEL), lambda i: (i, 0))],
        out_shape=[out, outb],
        compiler_params=_cparams(("parallel",)),
        name="ln0",
    )(xp, xs, g.reshape(1, -1), b.reshape(1, -1))


def _proj_kernel(x_ref, w_ref, b_ref, *rest, rot_hd, kinds, residue_dil, sigmoid):
    acc = jnp.dot(x_ref[...], w_ref[...], preferred_element_type=F32) + b_ref[...]
    if not rot_hd:
        (o_ref,) = rest
        o_ref[...] = (jax.nn.sigmoid(acc) if sigmoid else acc).astype(o_ref.dtype)
        return
    cs_ref, sn_ref, o_ref = rest[:3]
    h2 = rot_hd // 2
    lane = lax.broadcasted_iota(jnp.int32, (1, LANES), 1)
    first = (lane % rot_hd) < h2
    for c, kind in enumerate(kinds):
        val = acc[:, c * LANES:(c + 1) * LANES]
        if kind is not None:
            sw = jnp.where(first, pltpu.roll(val, LANES - h2, 1), pltpu.roll(val, h2, 1))
            val = val * cs_ref[kind] + sw * sn_ref[kind]
        if not residue_dil:
            o_ref[:, c * LANES:(c + 1) * LANES] = val.astype(o_ref.dtype)
        elif residue_dil == 1:
            o_ref[c] = val.astype(o_ref.dtype)
        else:
            tmp_ref = rest[3]
            tmp_ref[...] = val
            L = val.shape[0] // residue_dil
            for rho in range(residue_dil):
                o_ref[c, rho * L:(rho + 1) * L, :] = (
                    tmp_ref[pl.ds(rho, L, stride=residue_dil), :].astype(o_ref.dtype))


def _proj(xb, w, b, out_dtype, *, seq, tm, tn, rot=None, residue_dil=0, sigmoid=False,
          name="proj"):
    m, n = xb.shape[0], w.shape[1]
    assert m % tm == 0 and n % tn == 0 and seq % tm == 0
    in_specs = [
        pl.BlockSpec((tm, D_MODEL), lambda i, j: (i, 0)),
        pl.BlockSpec((D_MODEL, tn), lambda i, j: (0, j)),
        pl.BlockSpec((1, tn), lambda i, j: (0, j)),
    ]
    args = [xb, w, b.reshape(1, -1)]
    rot_hd, kinds, scratch = 0, None, []
    if rot is not None:
        rot_hd, cos, sin, kinds = rot
        assert len(kinds) == tn // LANES and LANES % rot_hd == 0
        spt = seq // tm
        tab = pl.BlockSpec((cos.shape[0], tm, LANES), lambda i, j: (0, i % spt, 0))
        in_specs += [tab, tab]
        args += [cos, sin]
    if residue_dil:
        assert rot is not None and tm == seq and seq % residue_dil == 0
        out_spec = pl.BlockSpec((tn // LANES, tm, LANES), lambda i, j: (j, i, 0))
        out_shape = jax.ShapeDtypeStruct((n // LANES, m, LANES), out_dtype)
        if residue_dil > 1:
            scratch = [pltpu.VMEM((tm, LANES), F32)]
    else:
        out_spec = pl.BlockSpec((tm, tn), lambda i, j: (i, j))
        out_shape = jax.ShapeDtypeStruct((m, n), out_dtype)
    return pl.pallas_call(
        functools.partial(_proj_kernel, rot_hd=rot_hd, kinds=kinds, residue_dil=residue_dil,
                          sigmoid=sigmoid),
        grid=(m // tm, n // tn),
        in_specs=in_specs,
        out_specs=out_spec,
        out_shape=out_shape,
        scratch_shapes=scratch,
        compiler_params=_cparams(("parallel", "arbitrary")),
        name=name,
    )(*args)


def _rope_tables(seq, head_dim, scales):
    half = head_dim // 2
    inv_freq = jnp.power(ROPE_THETA, -jnp.arange(half, dtype=F32) / half)
    ang = jnp.arange(seq, dtype=F32)[:, None] * inv_freq[None, :]
    lane = np.arange(LANES)
    f_idx = lane % half
    sign = np.where((lane % head_dim) < half, -1.0, 1.0).astype(np.float32)
    cos = jnp.cos(ang)[:, f_idx]
    sin = jnp.sin(ang)[:, f_idx] * sign[None, :]
    return (jnp.stack([cos * F32(s) for s in scales]), jnp.stack([sin * F32(s) for s in scales]))


def _merge_kernel(ya_ref, yb_ref, yc_ref, yd_ref, g_ref, x_ref, wb_ref, wo_ref, bo_ref, lg_ref,
                  lb_ref, of_ref, ob_ref):
    merged = None
    for i, y_ref in enumerate((ya_ref, yb_ref, yc_ref, yd_ref)):
        p = jnp.dot(y_ref[...], wb_ref[i], preferred_element_type=F32)
        gate = g_ref[:, i * D_MODEL:(i + 1) * D_MODEL].astype(F32)
        merged = gate * p if merged is None else merged + gate * p
    y = jnp.dot(merged.astype(BF16), wo_ref[...], preferred_element_type=F32) + bo_ref[...]
    z = DEEPNORM_ALPHA * x_ref[...] + y
    mu = jnp.mean(z, -1, keepdims=True)
    zc = z - mu
    var = jnp.mean(zc * zc, -1, keepdims=True)
    out = zc * lax.rsqrt(var + LN_EPS) * lg_ref[...] + lb_ref[...]
    of_ref[...] = out
    ob_ref[...] = out.astype(BF16)


def _merge(ya, yb, yc, yd, gates, x, wb, wo, bo, lg, lb, *, row0, rows, tm=512):
    assert row0 % tm == 0 and rows % tm == 0
    o = row0 // tm
    ysp = pl.BlockSpec((tm, BW), lambda i: (i + o, 0))
    const2 = lambda i: (0, 0)
    return pl.pallas_call(
        _merge_kernel,
        grid=(rows // tm,),
        in_specs=[ysp, ysp, ysp, ysp,
                  pl.BlockSpec((tm, 4 * D_MODEL), lambda i: (i + o, 0)),
                  pl.BlockSpec((tm, D_MODEL), lambda i: (i + o, 0)),
                  pl.BlockSpec((4, BW, D_MODEL), lambda i: (0, 0, 0)),
                  pl.BlockSpec((D_MODEL, D_MODEL), const2),
                  pl.BlockSpec((1, D_MODEL), const2),
                  pl.BlockSpec((1, D_MODEL), const2),
                  pl.BlockSpec((1, D_MODEL), const2)],
        out_specs=[pl.BlockSpec((tm, D_MODEL), lambda i: (i, 0)),
                   pl.BlockSpec((tm, D_MODEL), lambda i: (i, 0))],
        out_shape=[jax.ShapeDtypeStruct((rows, D_MODEL), F32),
                   jax.ShapeDtypeStruct((rows, D_MODEL), BF16)],
        compiler_params=_cparams(("parallel",)),
        name="merge",
    )(ya, yb, yc, yd, gates, x, wb, wo, bo.reshape(1, -1), lg.reshape(1, -1), lb.reshape(1, -1))


NA_QR = 4


def _na_geometry(rows):
    kr, wr = min(NA_ROWS, rows), min(NA_ROWS, rows) + NA_QR - 1
    assert rows % NA_QR == 0 and rows >= wr and kr == NA_ROWS
    steps = rows // NA_QR
    a = np.arange(NA_QR)[:, None]
    j = np.arange(wr)[None, :]
    pats = []
    for g in range(steps):
        r = NA_QR * g + a
        rs = np.clip(r - kr // 2, 0, rows - kr)
        ws = int(np.clip(NA_QR * g - kr // 2, 0, rows - wr))
        assert rs.min() >= ws and rs.max() + kr <= ws + wr
        valid = (ws + j >= rs) & (ws + j < rs + kr)
        dr = np.clip(ws + j - r + NA_ROWS - 1, 0, 2 * NA_ROWS - 2)
        pats.append((valid, np.where(valid, dr, 0)))
    same = lambda x, y: np.array_equal(x[0], y[0]) and np.array_equal(x[1], y[1])
    assert steps >= 3 and all(same(pats[g], pats[1]) for g in range(1, steps - 1))
    return kr, wr, steps, [pats[0], pats[1], pats[steps - 1]]


def _na_bias_tables(rpb, rows):
    kr, wr, steps, pats = _na_geometry(rows)
    qc = np.arange(GRID_W)[:, None]
    kc = np.arange(GRID_W)[None, :]
    c_start = np.clip(qc - NA_COLS // 2, 0, GRID_W - NA_COLS)
    col_ok = (kc >= c_start) & (kc < c_start + NA_COLS)
    dc = np.clip(kc - qc + NA_COLS - 1, 0, 2 * NA_COLS - 2)
    tabs = []
    by_col = rpb.astype(F32)[:, :, dc]
    for valid, dr in pats:
        b = by_col[:, dr]
        ok = valid[:, :, None, None] & col_ok[None, None]
        b = jnp.where(ok[None], b, MASKED)
        tabs.append(b.transpose(0, 1, 3, 2, 4).reshape(NA_HEADS, NA_QR * GRID_W, wr * GRID_W))
    return jnp.stack(tabs, axis=1)


def _stack_heads(q, n_heads, head_dim):
    lane = lax.broadcasted_iota(jnp.int32, (1, q.shape[1]), 1)
    zero = jnp.zeros_like(q)
    return jnp.concatenate(
        [jnp.where((lane >= h * head_dim) & (lane < (h + 1) * head_dim), q, zero)
         for h in range(n_heads)], axis=0)


def _unstack_heads(o, n_heads, head_dim):
    m = o.shape[0] // n_heads
    lane = lax.broadcasted_iota(jnp.int32, (1, o.shape[1]), 1)
    acc = o[0:m]
    for h in range(1, n_heads):
        in_head = (lane >= h * head_dim) & (lane < (h + 1) * head_dim)
        acc = jnp.where(in_head, o[h * m:(h + 1) * m], acc)
    return acc


def _na_kernel(q_ref, k_ref, v_ref, g_ref, bias_ref, o_ref, *, rows):
    kr, wr = min(NA_ROWS, rows), min(NA_ROWS, rows) + NA_QR - 1
    g = pl.program_id(1)
    ws = jnp.clip(NA_QR * g - kr // 2, 0, rows - wr)
    start = pl.multiple_of(ws * GRID_W, GRID_W)
    k = k_ref[pl.ds(start, wr * GRID_W), :]
    v = v_ref[pl.ds(start, wr * GRID_W), :]
    qs = _stack_heads(q_ref[...], NA_HEADS, NA_HEAD_DIM)
    m_rows = q_ref.shape[0]
    halves = [slice(0, 2 * m_rows), slice(2 * m_rows, 4 * m_rows)]
    ss = [_dot_nt(qs[h], k) for h in halves]
    ps, ls = [], []
    for i, s in enumerate(ss):
        s = s + bias_ref[2 * i:2 * i + 2].reshape(s.shape)
        p = jnp.exp(s - jnp.max(s, -1, keepdims=True))
        ls.append(jnp.sum(p, -1, keepdims=True))
        ps.append(p.astype(BF16))
    o = jnp.concatenate([jnp.dot(p, v, preferred_element_type=F32) / l for p, l in zip(ps, ls)], 0)
    o = _unstack_heads(o, NA_HEADS, NA_HEAD_DIM)
    o_ref[...] = (o * jax.nn.silu(g_ref[...])).astype(o_ref.dtype)


def _na(hb, fb, bias_tab, *, batch, seq):
    rows = seq // GRID_W
    kr, wr, steps, _ = _na_geometry(rows)
    qb = NA_QR * GRID_W

    def pattern_of(g):
        return jnp.where(g == 0, 0, jnp.where(g == steps - 1, 2, 1))

    return pl.pallas_call(
        functools.partial(_na_kernel, rows=rows),
        grid=(batch, steps),
        in_specs=[
            pl.BlockSpec((None, qb, BW), lambda b, g: (b, g, 0)),
            pl.BlockSpec((None, seq, BW), lambda b, g: (b, 0, 1)),
            pl.BlockSpec((None, seq, BW), lambda b, g: (b, 0, 2)),
            pl.BlockSpec((None, qb, BW), lambda b, g: (b, g, 5)),
            pl.BlockSpec((NA_HEADS, None, qb, wr * GRID_W), lambda b, g: (0, pattern_of(g), 0, 0)),
        ],
        out_specs=pl.BlockSpec((None, qb, BW), lambda b, g: (b, g, 0)),
        out_shape=jax.ShapeDtypeStruct((batch, seq, BW), BF16),
        compiler_params=_cparams(("parallel", "arbitrary")),
        name="na_attn",
    )(hb, hb, hb, fb, bias_tab)


RW_C = 64
RW_FIELDS = 9


def _mm(a, b, mode="nn", passes=1):
    dims = {"nn": (((1,), (0,)), ((), ())), "nt": (((1,), (1,)), ((), ()))}[mode]
    dg = lambda x, y: lax.dot_general(x, y, dims, preferred_element_type=F32)
    ah, bh = a.astype(BF16), b.astype(BF16)
    out = dg(ah, bh)
    if passes == 3:
        al = (a - ah.astype(F32)).astype(BF16)
        bl = (b - bh.astype(F32)).astype(BF16)
        out = out + (dg(al, bh) + dg(ah, bl))
    return out


def _mm_bd(a, b, mode="nn"):
    halves = [slice(i * LANES, (i + 1) * LANES) for i in range(BW // LANES)]
    return jnp.concatenate([_mm(a[:, h], b[h, h], mode) for h in halves], axis=1)


def _softplus(x):
    return jnp.maximum(x, 0.0) + jnp.log(1.0 + jnp.exp(-jnp.abs(x)))


def _rwprep_kernel(x_ref, prev_ref, next_ref, mu_ref, w0_ref, w2_ref, a0_ref, a2_ref, kk_ref, ka_ref,
                   rk_ref, f_ref, e_ref, *, rc):
    c = pl.program_id(1)
    lane = lax.broadcasted_iota(jnp.int32, (1, 2 * RW_LORA), 1)
    ri = lax.broadcasted_iota(jnp.int32, (rc, 1), 0)
    x = x_ref[...]
    prev_row = jnp.where(c > 0, prev_ref[7:8, :], 0.0)
    next_row = jnp.where(c < pl.num_programs(1) - 1, next_ref[0:1, :], 0.0)
    up = jnp.where(ri == 0, prev_row, pltpu.roll(x, 1, 0))
    dn = jnp.where(ri == rc - 1, next_row, pltpu.roll(x, rc - 1, 0))
    xs = x + mu_ref[...] * (0.5 * (up + dn) - x)
    r, k, v, g = (xs[:, i * BW:(i + 1) * BW] for i in range(4))
    wl = xs[:, 4 * BW:4 * BW + 2 * RW_LORA]
    al = xs[:, 4 * BW + 2 * RW_LORA:4 * BW + 4 * RW_LORA]
    kk = k * kk_ref[...]
    kap = kk * lax.rsqrt(jnp.maximum(_seg_sum(kk * kk, RW_HEAD_DIM), 1e-24))
    tw = jnp.tanh(wl)
    kd_sum = None
    for d in range(2):
        dm = ((lane >= d * RW_LORA) & (lane < (d + 1) * RW_LORA)).astype(F32)
        w_raw = w0_ref[d:d + 1, :] + _mm(tw * dm, w2_ref[...], passes=3)
        lw = -jnp.exp(-_softplus(-w_raw) - 0.5)
        a = jax.nn.sigmoid(a0_ref[d:d + 1, :] + _mm(al * dm, a2_ref[...], passes=3))
        kd = k * (1.0 + (a - 1.0) * ka_ref[...])
        f_ref[:, (3 + 3 * d) * BW:(4 + 3 * d) * BW] = lw
        f_ref[:, (4 + 3 * d) * BW:(5 + 3 * d) * BW] = kd
        f_ref[:, (5 + 3 * d) * BW:(6 + 3 * d) * BW] = kap * a
        kd_sum = kd if kd_sum is None else kd_sum + kd
    f_ref[:, 0:BW] = r
    f_ref[:, BW:2 * BW] = v
    f_ref[:, 2 * BW:3 * BW] = kap
    bonus = _seg_sum(r * kd_sum * rk_ref[...], RW_HEAD_DIM)
    e_ref[:, 0:BW] = bonus * v
    e_ref[:, BW:2 * BW] = jax.nn.silu(g)


def _rwprep(fb, mu, w0, w2, a0, a2, k_k, k_a, r_k, *, batch, seq, rc=256):
    ucols = 4 * BW + 4 * RW_LORA
    row = lambda z: z.astype(F32).reshape(1, -1)
    c2 = lambda b, c: (0, 0)
    tpb = rc // 8
    return pl.pallas_call(
        functools.partial(_rwprep_kernel, rc=rc),
        grid=(batch, seq // rc),
        in_specs=[
            pl.BlockSpec((None, rc, ucols), lambda b, c: (b, c, 0)),
            pl.BlockSpec((None, 8, ucols), lambda b, c: (b, jnp.maximum(c * tpb - 1, 0), 0)),
            pl.BlockSpec((None, 8, ucols),
                         lambda b, c: (b, jnp.minimum((c + 1) * tpb, seq // 8 - 1), 0)),
            pl.BlockSpec((1, ucols), c2),
            pl.BlockSpec((2, BW), c2),
            pl.BlockSpec((2 * RW_LORA, BW), c2),
            pl.BlockSpec((2, BW), c2),
            pl.BlockSpec((2 * RW_LORA, BW), c2),
            pl.BlockSpec((1, BW), c2),
            pl.BlockSpec((1, BW), c2),
            pl.BlockSpec((1, BW), c2),
        ],
        out_specs=[pl.BlockSpec((None, rc, RW_FIELDS * BW), lambda b, c: (b, c, 0)),
                   pl.BlockSpec((None, rc, 2 * BW), lambda b, c: (b, c, 0))],
        out_shape=[jax.ShapeDtypeStruct((batch, seq, RW_FIELDS * BW), F32),
                   jax.ShapeDtypeStruct((batch, seq, 2 * BW), F32)],
        compiler_params=_cparams(("parallel", "arbitrary")),
        name="rwkv_prep",
    )(fb, fb, fb, row(mu), w0.astype(F32), w2.astype(F32).reshape(2 * RW_LORA, BW), a0.astype(F32),
      a2.astype(F32).reshape(2 * RW_LORA, BW), row(k_k), row(k_a), row(r_k))


def _rw_bd(x):
    lane = lax.broadcasted_iota(jnp.int32, (1, BW), 1)
    xb = x.astype(BF16)
    zero = jnp.zeros_like(xb)
    return jnp.concatenate(
        [jnp.where((lane >= h * RW_C) & (lane < (h + 1) * RW_C), xb, zero) for h in range(RW_HEADS)],
        axis=0)


def _rw_advance(terms, z):
    gm, hv, nt = terms
    both = _mm_bd(gm, z)
    return both[0:RW_C] + hv, _rw_bd(both[RW_C:2 * RW_C] + nt)


def _rw_prepare(specs):
    C, H = RW_C, RW_HEADS
    assert H * C == BW and RW_HEAD_DIM == C
    bd = _rw_bd
    cat = lambda a, b: jnp.concatenate([a, b], axis=0)
    ri = lax.broadcasted_iota(jnp.int32, (C, C), 0)
    ci = lax.broadcasted_iota(jnp.int32, (C, C), 1)
    tris = [(ci <= ri).astype(BF16), (ci >= ri).astype(BF16)]
    row = lax.broadcasted_iota(jnp.int32, (C, BW), 0)
    col = lax.broadcasted_iota(jnp.int32, (C, BW), 1) % C
    eye = col == row
    strict = [col < row, col > row]
    incl = [col <= row, col >= row]

    def lc_t(x):
        t = bd(x).astype(F32).T
        return t[0:C] + t[C:2 * C] + t[2 * C:3 * C] + t[3 * C:4 * C]

    def load(spec):
        f_ref, r0, d = spec
        fld = lambda i: f_ref[r0:r0 + C, i * BW:(i + 1) * BW]
        c = dict(d=d, r=fld(0), v=fld(1), kap=fld(2), lw=fld(3 + 3 * d), kd=fld(4 + 3 * d),
                 beta=fld(5 + 3 * d))
        cs = None
        for t in _split2(c["lw"]):
            y = jnp.dot(tris[d], t, preferred_element_type=F32)
            cs = y if cs is None else cs + y
        c["cs"] = cs
        return c

    def scale(c):
        cs, d = c["cs"], c["d"]
        tot = cs[0:1] if d == 1 else cs[C - 1:C]
        ginv = jnp.exp(-cs)
        gto = jnp.exp(tot - cs)
        kt = c["kap"] * jnp.exp(cs - c["lw"])
        rt = c["r"] * jnp.exp(cs)
        c.update(gc=jnp.exp(tot), rt=rt, ktrt=cat(kt, rt), kt_b=bd(kt), v_b=bd(c["v"]),
                 kg_b=bd(c["kd"] * ginv), bg_b=bd(c["beta"] * ginv),
                 kh=c["kd"] * gto, bh=c["beta"] * gto)
        return c

    def gram(c):
        d = c["d"]
        gk = _mm_bd(c["ktrt"], c["kg_b"], "nt")
        gb = _mm_bd(c["ktrt"], c["bg_b"], "nt")
        c.update(l_kk=jnp.where(strict[d], gk[0:C], 0.0), l_kr=jnp.where(incl[d], gk[C:2 * C], 0.0),
                 l_br=jnp.where(incl[d], gb[C:2 * C], 0.0),
                 npow=-jnp.where(strict[d], gb[0:C], 0.0), w=jnp.where(eye, 1.0, 0.0))
        return c

    def level(c, last):
        if last:
            c["w"] = c["w"] + _mm_bd(c["w"], bd(c["npow"]))
        else:
            both = _mm_bd(cat(c["npow"], c["w"]), bd(c["npow"]))
            c["npow"], c["w"] = both[0:C], c["w"] + both[C:2 * C]
        return c

    def solve(c):
        c["p1_b"] = bd(_mm_bd(c["w"], c["kt_b"]))
        c["p2"] = _mm_bd(c["w"], bd(c["l_kk"]))
        c["p2_b"] = bd(c["p2"])
        return c

    def outputs(c):
        both = _mm_bd(cat(c["l_br"], c["bh_t"]), c["p1_b"])
        c["gm"] = cat(c["rt"] - both[0:C], jnp.where(eye, c["gc"], 0.0) - both[C:2 * C])
        hm = c["l_kr"] - _mm_bd(c["l_br"], c["p2_b"])
        three = _mm_bd(jnp.concatenate([hm, c["p2"], c["kh_t"]], axis=0), c["v_b"])
        c["hv"], c["p2v_b"], c["khv"] = three[0:C], bd(three[C:2 * C]), three[2 * C:3 * C]
        return c

    def state(c):
        nt = c["khv"] - _mm_bd(c["bh_t"], c["p2v_b"])
        return c["gm"].astype(BF16), c["hv"], nt

    def transposes(c):
        c["kh_t"], c["bh_t"] = lc_t(c.pop("kh")), lc_t(c.pop("bh"))
        return c

    cs = [load(s) for s in specs]
    cs = [scale(c) for c in cs]
    cs = [gram(c) for c in cs]
    cs = [transposes(c) for c in cs]
    levels = int(math.log2(C))
    for lvl in range(levels):
        cs = [level(c, lvl == levels - 1) for c in cs]
    cs = [solve(c) for c in cs]
    cs = [outputs(c) for c in cs]
    return [state(c) for c in cs]


def _rwscan_kernel(ff_ref, fr_ref, yf_ref, yr_ref, zf_ref, zr_ref, *, nch):
    @pl.when(pl.program_id(1) == 0)
    def _():
        zf_ref[...] = jnp.zeros_like(zf_ref)
        zr_ref[...] = jnp.zeros_like(zr_ref)

    C = RW_C
    pre = _rw_prepare([(ff_ref, j * C, 0) for j in range(nch)]
                      + [(fr_ref, j * C, 1) for j in range(nch)])
    pre_f, pre_r = pre[:nch], pre[nch:]
    zf, zr = zf_ref[...], zr_ref[...]
    for j in range(nch):
        yf_ref[j * C:(j + 1) * C, :], zf = _rw_advance(pre_f[j], zf)
        jr = nch - 1 - j
        yr_ref[jr * C:(jr + 1) * C, :], zr = _rw_advance(pre_r[jr], zr)
    zf_ref[...] = zf
    zr_ref[...] = zr


def _rwscan(f, *, batch, seq, nch=4):
    rb = nch * RW_C
    nc = seq // rb
    fsp = lambda imap: pl.BlockSpec((None, rb, RW_FIELDS * BW), imap)
    ysp = lambda imap: pl.BlockSpec((None, rb, BW), imap)
    fwd = lambda b, s: (b, s, 0)
    rev = lambda b, s: (b, nc - 1 - s, 0)
    y = jax.ShapeDtypeStruct((batch, seq, BW), F32)
    return pl.pallas_call(
        functools.partial(_rwscan_kernel, nch=nch),
        grid=(batch, nc),
        in_specs=[fsp(fwd), fsp(rev)],
        out_specs=[ysp(fwd), ysp(rev)],
        out_shape=[y, y],
        scratch_shapes=[pltpu.VMEM((BW, BW), BF16), pltpu.VMEM((BW, BW), BF16)],
        compiler_params=_cparams(("parallel", "arbitrary")),
        name="rwkv_scan",
    )(f, f)


def _rwpost_kernel(yf_ref, yr_ref, e_ref, lg_ref, lb_ref, o_ref):
    y = yf_ref[...] + yr_ref[...]
    inv = 1.0 / RW_HEAD_DIM
    yc = y - _seg_sum(y, RW_HEAD_DIM) * inv
    var = _seg_sum(yc * yc, RW_HEAD_DIM) * inv
    yn = yc * lax.rsqrt(var + RW_LNX_EPS) * lg_ref[...] + lb_ref[...]
    o_ref[...] = ((yn + e_ref[:, 0:BW]) * e_ref[:, BW:2 * BW]).astype(o_ref.dtype)


def _rwpost(yf, yr, e, lnx_g, lnx_b, *, tm=512):
    m = yf.shape[0]
    row = lambda z: z.astype(F32).reshape(1, -1)
    ysp = pl.BlockSpec((tm, BW), lambda i: (i, 0))
    return pl.pallas_call(
        _rwpost_kernel,
        grid=(m // tm,),
        in_specs=[ysp, ysp, pl.BlockSpec((tm, 2 * BW), lambda i: (i, 0)),
                  pl.BlockSpec((1, BW), lambda i: (0, 0)), pl.BlockSpec((1, BW), lambda i: (0, 0))],
        out_specs=ysp,
        out_shape=jax.ShapeDtypeStruct((m, BW), BF16),
        compiler_params=_cparams(("parallel",)),
        name="rwkv_post",
    )(yf, yr, e, row(lnx_g), row(lnx_b))


def _df_kernel(lam_ref, sg_ref, q_ref, k_ref, v_ref, g_ref, o_ref, *, lam_init):
    lam = lam_ref[...]
    e1 = jnp.exp(jnp.sum(lam[0:1] * lam[1:2], -1, keepdims=True))
    e2 = jnp.exp(jnp.sum(lam[2:3] * lam[3:4], -1, keepdims=True))
    lam_full = e1 - e2 + lam_init
    d = DF_HEAD_DIM
    tq = q_ref.shape[0]
    qs = _stack_heads(q_ref[...], 2 * DF_HEADS, d)
    k, v = k_ref[...], v_ref[...]
    ss = [_dot_nt(qs[2 * h * tq:2 * (h + 1) * tq], k) for h in range(DF_HEADS)]
    ws, r1s = [], []
    for s in ss:
        e = jnp.exp2(s - jnp.max(s, -1, keepdims=True))
        l = jnp.sum(e, -1, keepdims=True)
        r1 = 1.0 / l[0:tq]
        ws.append((e[0:tq] - e[tq:2 * tq] * (lam_full * l[0:tq] / l[tq:2 * tq])).astype(BF16))
        r1s.append(r1)
    os_ = [jnp.dot(w, v, preferred_element_type=F32) * r1 for w, r1 in zip(ws, r1s)]
    lane = lax.broadcasted_iota(jnp.int32, (1, BW), 1)
    acc = None
    for h, o in enumerate(os_):
        in_head = (lane >= 2 * h * d) & (lane < 2 * (h + 1) * d)
        acc = o if acc is None else jnp.where(in_head, o, acc)
    ms = _seg_sum(acc * acc, 2 * d) * (1.0 / (2 * d))
    o = acc * lax.rsqrt(ms + DF_EPS) * sg_ref[...] * (1.0 - lam_init)
    o_ref[...] = (o * jax.nn.silu(g_ref[...])).astype(o_ref.dtype)


def _df(cqk, hb, fb, lam, subln_g, lam_init, *, batch, seq, tq=256):
    sg = jnp.tile(subln_g.astype(F32), DF_HEADS).reshape(1, BW)
    return pl.pallas_call(
        functools.partial(_df_kernel, lam_init=lam_init),
        grid=(batch, seq // tq),
        in_specs=[
            pl.BlockSpec((4, DF_HEAD_DIM), lambda b, j: (0, 0)),
            pl.BlockSpec((1, BW), lambda b, j: (0, 0)),
            pl.BlockSpec((None, tq, BW), lambda b, j: (b, j, 0)),
            pl.BlockSpec((None, seq, BW), lambda b, j: (b, 0, 1)),
            pl.BlockSpec((None, seq, BW), lambda b, j: (b, 0, 3)),
            pl.BlockSpec((None, tq, BW), lambda b, j: (b, j, 6)),
        ],
        out_specs=pl.BlockSpec((None, tq, BW), lambda b, j: (b, j, 0)),
        out_shape=jax.ShapeDtypeStruct((batch, seq, BW), BF16),
        compiler_params=_cparams(("parallel", "arbitrary")),
        name="diff_attn",
    )(lam.astype(F32), sg, cqk, cqk, hb, fb)


DL_QB = 2 * DL_HALF
DL_UNROLL = 4


def _dl_group(qkv_ref, kpad, vpad, oacc, lacc, gidx, dil, seq):
    L = seq // dil
    assert L % DL_QB == 0 and (seq // DL_QB) % DL_UNROLL == 0
    nblk = L // DL_QB
    seg = L + 2 * DL_HALF
    zpad = jnp.zeros((DL_HALF, BW), BF16)

    def stage(rho, carry):
        src = pl.ds(pl.multiple_of(rho * L, DL_QB), L)
        krow = pl.multiple_of(rho * seg, DL_HALF)
        for ref in (kpad, vpad):
            ref[pl.ds(krow, DL_HALF), :] = zpad
            ref[pl.ds(krow + DL_HALF + L, DL_HALF), :] = zpad
        for c in range(BW // LANES):
            cols = slice(c * LANES, (c + 1) * LANES)
            kpad[pl.ds(krow + DL_HALF, L), cols] = qkv_ref[2 + c, src, :]
            vpad[pl.ds(krow + DL_HALF, L), cols] = qkv_ref[4 + c, src, :]
        return carry

    lax.fori_loop(0, dil, stage, 0)

    hq = DL_HEADS * DL_QB
    qi = lax.broadcasted_iota(jnp.int32, (hq, 2 * DL_QB), 0) % DL_QB
    ji = lax.broadcasted_iota(jnp.int32, (hq, 2 * DL_QB), 1)
    band = jnp.where(jnp.abs(ji - DL_HALF - qi) <= DL_HALF, 0.0, MASKED)
    jrow = lax.broadcasted_iota(jnp.int32, (1, 2 * DL_QB), 1)
    lane = lax.broadcasted_iota(jnp.int32, (1, BW), 1)

    def blocks(it, carry):
        fs = [it * DL_UNROLL + u for u in range(DL_UNROLL)]
        rhos = [f // nblk for f in fs]
        ns = [f % nblk for f in fs]
        qrows = [pl.ds(pl.multiple_of(f * DL_QB, DL_QB), DL_QB) for f in fs]
        qst = [_stack_heads(jnp.concatenate([qkv_ref[c, r, :] for c in range(BW // LANES)], axis=1),
                            DL_HEADS, DL_HEAD_DIM) for r in qrows]
        krows = [pl.multiple_of(rho * seg + n * DL_QB, DL_HALF) for rho, n in zip(rhos, ns)]
        ss = [_dot_nt(q, kpad[pl.ds(kr, 2 * DL_QB), :]) for q, kr in zip(qst, krows)]
        ps, lses, dens = [], [], []
        for s, n in zip(ss, ns):
            kpos = n * DL_QB - DL_HALF + jrow
            s = s + band + jnp.where((kpos >= 0) & (kpos < L), 0.0, MASKED)
            m = jnp.max(s, -1, keepdims=True)
            p = jnp.exp(s - m)
            den = jnp.sum(p, -1, keepdims=True)
            ps.append(p.astype(BF16))
            dens.append(den)
            lses.append(m + jnp.log(den))
        os_ = [jnp.dot(p, vpad[pl.ds(kr, 2 * DL_QB), :], preferred_element_type=F32) / den
               for p, kr, den in zip(ps, krows, dens)]
        for o, lse, rho, n in zip(os_, lses, rhos, ns):
            acc_o = _unstack_heads(o, DL_HEADS, DL_HEAD_DIM)
            acc_l = _unstack_heads(jnp.broadcast_to(lse, (hq, BW)), DL_HEADS, DL_HEAD_DIM)
            if dil > 1:
                dst = pl.ds(rho + dil * n * DL_QB, DL_QB, stride=dil)
            else:
                dst = pl.ds(pl.multiple_of(n * DL_QB, DL_QB), DL_QB)
            for c in range(BW // LANES):
                oacc[gidx, c, dst, :] = acc_o[:, c * LANES:(c + 1) * LANES]
                lacc[gidx, c, dst, :] = acc_l[:, c * LANES:(c + 1) * LANES]
        return carry

    lax.fori_loop(0, seq // DL_QB // DL_UNROLL, blocks, 0)


def _dl_kernel(*refs, seq):
    ng = len(DL_DILATIONS)
    qkv_refs, (g_ref, o_ref, kpad, vpad, oacc, lacc) = refs[:ng], refs[ng:]
    gi = pl.program_id(1)
    for gidx, dil in enumerate(DL_DILATIONS):
        @pl.when(gi == gidx)
        def _(gidx=gidx, dil=dil):
            _dl_group(qkv_refs[gidx], kpad, vpad, oacc, lacc, gidx, dil, seq)

    @pl.when(gi == len(DL_DILATIONS) - 1)
    def _():
        rc = 256

        def chunk(c, carry):
            rows = pl.ds(pl.multiple_of(c * rc, rc), rc)
            for lc in range(BW // LANES):
                cols = slice(lc * LANES, (lc + 1) * LANES)
                ls = [lacc[g, lc, rows, :] for g in range(len(DL_DILATIONS))]
                mx = functools.reduce(jnp.maximum, ls)
                ws = [jnp.exp(l - mx) for l in ls]
                num = sum(w * oacc[g, lc, rows, :] for g, w in enumerate(ws))
                out = num / sum(ws)
                o_ref[rows, cols] = (out * jax.nn.silu(g_ref[rows, cols])).astype(o_ref.dtype)
            return carry

        lax.fori_loop(0, seq // rc, chunk, 0)


def _dl(dqkvs, fb, *, batch, seq):
    ng = len(DL_DILATIONS)
    nch = 3 * BW // LANES
    assert all(w // (2 * d) == DL_HALF for w, d in zip(DL_WINDOWS, DL_DILATIONS))
    return pl.pallas_call(
        functools.partial(_dl_kernel, seq=seq),
        grid=(batch, ng),
        in_specs=[pl.BlockSpec((nch, None, seq, LANES), lambda b, g: (0, b, 0, 0))] * ng
        + [pl.BlockSpec((None, seq, BW), lambda b, g: (b, 0, 7))],
        out_specs=pl.BlockSpec((None, seq, BW), lambda b, g: (b, 0, 0)),
        out_shape=jax.ShapeDtypeStruct((batch, seq, BW), BF16),
        scratch_shapes=[
            pltpu.VMEM((seq + 2 * DL_HALF * max(DL_DILATIONS), BW), BF16),
            pltpu.VMEM((seq + 2 * DL_HALF * max(DL_DILATIONS), BW), BF16),
            pltpu.VMEM((ng, BW // LANES, seq, LANES), F32),
            pltpu.VMEM((ng, BW // LANES, seq, LANES), F32),
        ],
        compiler_params=_cparams(("parallel", "arbitrary")),
        name="dilated_attn",
    )(*dqkvs, fb)


_A0 = 0
_B0 = _A0 + 4 * BW
_C0 = _B0 + 4 * BW + 4 * RW_LORA
_D0 = _C0 + 4 * BW
_DG0 = _D0 + 9 * BW
_G0 = _DG0 + BW
PROJ_TM = 2048


def _layer(xf, xb, l, p, tabs, *, batch, seq, last_split):
    w, b = p["w_in"][l], p["b_in"][l]
    cols = lambda lo, n: (w[:, lo:lo + n], b[lo:lo + n])
    cat = lambda parts: (jnp.concatenate([q[0] for q in parts], 1).astype(BF16),
                         jnp.concatenate([q[1] for q in parts], 0).astype(F32))
    scaled = lambda part, s: (part[0] * s, part[1] * s)
    q_scale = NA_HEAD_DIM ** -0.5
    proj = functools.partial(_proj, xb, seq=seq, tm=PROJ_TM)

    gates = proj(*cat([cols(_G0, 4 * D_MODEL)]), BF16, tn=1024, sigmoid=True, name="proj_gates")
    fb = proj(*cat([cols(_B0, _C0 - _B0), cols(_A0 + 3 * BW, BW), cols(_C0 + 3 * BW, BW),
                    cols(_DG0, BW)]), F32, tn=1024, name="proj_f32")
    hb = proj(*cat([scaled(cols(_A0, BW), q_scale), cols(_A0 + BW, 2 * BW), cols(_C0 + 2 * BW, BW)]),
              BF16, tn=1024, name="proj_bf16")
    cpb = BW // LANES
    cqk = proj(*cat([cols(_C0, 2 * BW)]), BF16, tn=2 * BW, name="proj_rot_c",
               rot=(DF_HEAD_DIM,) + tabs["c"] + ((0,) * cpb + (1,) * cpb,))
    dqs = [proj(*cat([scaled(cols(_D0 + 3 * g * BW, BW), q_scale),
                      cols(_D0 + (3 * g + 1) * BW, 2 * BW)]),
                BF16, tn=3 * BW, residue_dil=dil, name="proj_rot_d",
                rot=(DL_HEAD_DIM,) + tabs["d"] + ((0,) * (2 * cpb) + (None,) * cpb,))
           for g, dil in enumerate(DL_DILATIONS)]

    r3 = lambda z: z.reshape(batch, seq, z.shape[-1])
    hb3, fb3 = r3(hb), r3(fb)
    ya = _na(hb3, fb3, _na_bias_tables(p["na_rpb"][l], seq // GRID_W), batch=batch, seq=seq)
    f, e = _rwprep(fb3, p["rw_mu"][l], p["rw_w0"][l], p["rw_w2"][l], p["rw_a0"][l], p["rw_a2"][l],
                   p["rw_kk"][l], p["rw_ka"][l], p["rw_rk"][l].reshape(-1), batch=batch, seq=seq)
    yf, yr = _rwscan(f, batch=batch, seq=seq)
    m = batch * seq
    yb = _rwpost(yf.reshape(m, BW), yr.reshape(m, BW), e.reshape(m, 2 * BW), p["rw_lnx_g"][l],
                 p["rw_lnx_b"][l])
    lam_init = 0.8 - 0.6 * math.exp(-0.3 * l)
    yc = _df(r3(cqk), hb3, fb3, p["df_lam"][l], p["df_subln_g"][l], lam_init, batch=batch, seq=seq)
    yd = _dl([dq.reshape(dq.shape[0], batch, seq, LANES) for dq in dqs], fb3, batch=batch, seq=seq)

    margs = (ya.reshape(m, BW), yb, yc.reshape(m, BW), yd.reshape(m, BW), gates, xf,
             p["w_branch"][l].astype(BF16), p["w_out"][l].astype(BF16), p["b_out"][l],
             p["ln_g"][l], p["ln_b"][l])
    if last_split is None:
        return _merge(*margs, row0=0, rows=m)
    return [_merge(*margs, row0=r0, rows=n) for r0, n in last_split]


def kernel(x_prompt, x_sample, ln0_g, ln0_b, w_in, b_in, na_rpb, rw_mu, rw_w0, rw_w2, rw_a0, rw_a2,
           rw_kk, rw_ka, rw_rk, rw_lnx_g, rw_lnx_b, df_lam, df_subln_g, w_branch, w_out, b_out,
           ln_g, ln_b):
    p = dict(w_in=w_in, b_in=b_in, na_rpb=na_rpb, rw_mu=rw_mu, rw_w0=rw_w0, rw_w2=rw_w2,
             rw_a0=rw_a0, rw_a2=rw_a2, rw_kk=rw_kk, rw_ka=rw_ka, rw_rk=rw_rk, rw_lnx_g=rw_lnx_g,
             rw_lnx_b=rw_lnx_b, df_lam=df_lam, df_subln_g=df_subln_g, w_branch=w_branch,
             w_out=w_out, b_out=b_out, ln_g=ln_g, ln_b=ln_b)
    bp, seq, _ = x_prompt.shape
    bs = x_sample.shape[0]
    assert x_sample.shape[1] == seq
    batch = bp + bs
    xf, xb = _ln0(x_prompt.reshape(bp * seq, D_MODEL), x_sample.reshape(bs * seq, D_MODEL),
                  ln0_g, ln0_b)
    tabs = {"c": _rope_tables(seq, DF_HEAD_DIM, (DF_HEAD_DIM ** -0.5 * math.log2(math.e), 1.0)),
            "d": _rope_tables(seq, DL_HEAD_DIM, (1.0,))}
    for l in range(DEPTH - 1):
        xf, xb = _layer(xf, xb, l, p, tabs, batch=batch, seq=seq, last_split=None)
    split = [(0, bp * seq), (bp * seq, bs * seq)]
    (yp, _), (ys, _) = _layer(xf, xb, DEPTH - 1, p, tabs, batch=batch, seq=seq, last_split=split)
    return yp.reshape(bp, seq, D_MODEL), ys.reshape(bs, seq, D_MODEL)
```

```python
import functools
import math

import numpy as np
import jax
import jax.numpy as jnp
from jax import lax
from jax.experimental import pallas as pl
from jax.experimental.pallas import tpu as pltpu

F32 = jnp.float32
BF16 = jnp.bfloat16

D_MODEL = 1024
DEPTH = 2
GRID_W = 64
NA_HEADS, NA_HEAD_DIM, NA_ROWS, NA_COLS = 4, 64, 8, 16
RW_HEADS, RW_HEAD_DIM, RW_LORA = 4, 64, 64
RW_LNX_EPS = 64e-5
DF_HEADS, DF_HEAD_DIM, DF_EPS = 4, 32, 1e-5
DL_HEADS, DL_HEAD_DIM = 4, 64
DL_DILATIONS = (1, 4, 16)
DL_WINDOWS = (128, 512, 2048)
DL_HALF = 64
BW = 256
ROPE_THETA = 10000.0
LN_EPS = 1e-5
DEEPNORM_ALPHA = (2 * DEPTH) ** 0.25
LOG2E = math.log2(math.e)
MASKED = -float("inf")

LANES = 128
VMEM_LIMIT = 56 * 1024 * 1024


def _cparams(sem):
    return pltpu.CompilerParams(dimension_semantics=sem, vmem_limit_bytes=VMEM_LIMIT)


def _dot_nt(a, b):
    return lax.dot_general(a, b, (((1,), (1,)), ((), ())), preferred_element_type=F32)


def _split2(x):
    hi = x.astype(BF16)
    return hi, (x - hi.astype(F32)).astype(BF16)


def _seg_sum(x, seg):
    n = x.shape[-1]
    r = lax.broadcasted_iota(jnp.int32, (n, n), 0) // seg
    c = lax.broadcasted_iota(jnp.int32, (n, n), 1) // seg
    ones = (r == c).astype(BF16)
    out = None
    for t in _split2(x):
        y = jnp.dot(t, ones, preferred_element_type=F32)
        out = y if out is None else out + y
    return out


def _ln0_kernel(xp_ref, xs_ref, g_ref, b_ref, of_ref, ob_ref, *, n_prompt):
    def norm(x):
        mu = jnp.mean(x, -1, keepdims=True)
        xc = x - mu
        var = jnp.mean(xc * xc, -1, keepdims=True)
        y = xc * lax.rsqrt(var + LN_EPS) * g_ref[...] + b_ref[...]
        of_ref[...] = y
        ob_ref[...] = y.astype(BF16)

    i = pl.program_id(0)

    @pl.when(i < n_prompt)
    def _():
        norm(xp_ref[...])

    @pl.when(i >= n_prompt)
    def _():
        norm(xs_ref[...])


def _ln0(xp, xs, g, b, tm=512):
    mp, ms = xp.shape[0], xs.shape[0]
    n_p, n_s = mp // tm, ms // tm
    out = jax.ShapeDtypeStruct((mp + ms, D_MODEL), F32)
    outb = jax.ShapeDtypeStruct((mp + ms, D_MODEL), BF16)
    return pl.pallas_call(
        functools.partial(_ln0_kernel, n_prompt=n_p),
        grid=(n_p + n_s,),
        in_specs=[
            pl.BlockSpec((tm, D_MODEL), lambda i: (jnp.minimum(i, n_p - 1), 0)),
            pl.BlockSpec((tm, D_MODEL), lambda i: (jnp.maximum(i - n_p, 0), 0)),
            pl.BlockSpec((1, D_MODEL), lambda i: (0, 0)),
            pl.BlockSpec((1, D_MODEL), lambda i: (0, 0)),
        ],
        out_specs=[pl.BlockSpec((tm, D_MODEL), lambda i: (i, 0)),
                   pl.BlockSpec((tm, D_MODEL), lambda i: (i, 0))],
        out_shape=[out, outb],
        compiler_params=_cparams(("parallel",)),
        name="ln0",
    )(xp, xs, g.reshape(1, -1), b.reshape(1, -1))


def _proj_kernel(x_ref, w_ref, b_ref, *rest, rot_hd, kinds, residue_dil, sigmoid_2x):
    acc = jnp.dot(x_ref[...], w_ref[...], preferred_element_type=F32) + b_ref[...]
    if not rot_hd:
        (o_ref,) = rest
        o_ref[...] = (0.5 * jnp.tanh(acc) + 0.5 if sigmoid_2x else acc).astype(o_ref.dtype)
        return
    cs_ref, sn_ref, o_ref = rest[:3]
    h2 = rot_hd // 2
    lane = lax.broadcasted_iota(jnp.int32, (1, LANES), 1)
    first = (lane % rot_hd) < h2
    for c, kind in enumerate(kinds):
        val = acc[:, c * LANES:(c + 1) * LANES]
        if kind is not None:
            sw = jnp.where(first, pltpu.roll(val, LANES - h2, 1), pltpu.roll(val, h2, 1))
            val = val * cs_ref[kind] + sw * sn_ref[kind]
        if not residue_dil:
            o_ref[:, c * LANES:(c + 1) * LANES] = val.astype(o_ref.dtype)
        elif residue_dil == 1:
            o_ref[c] = val.astype(o_ref.dtype)
        else:
            tmp_ref = rest[3]
            tmp_ref[...] = val
            L = val.shape[0] // residue_dil
            for rho in range(residue_dil):
                o_ref[c, rho * L:(rho + 1) * L, :] = (
                    tmp_ref[pl.ds(rho, L, stride=residue_dil), :].astype(o_ref.dtype))


def _proj(xb, w, b, out_dtype, *, seq, tm, tn, rot=None, residue_dil=0, sigmoid_2x=False,
          name="proj"):
    m, n = xb.shape[0], w.shape[1]
    assert m % tm == 0 and n % tn == 0 and seq % tm == 0
    in_specs = [
        pl.BlockSpec((tm, D_MODEL), lambda i, j: (i, 0)),
        pl.BlockSpec((D_MODEL, tn), lambda i, j: (0, j)),
        pl.BlockSpec((1, tn), lambda i, j: (0, j)),
    ]
    args = [xb, w, b.reshape(1, -1)]
    rot_hd, kinds, scratch = 0, None, []
    if rot is not None:
        rot_hd, cos, sin, kinds = rot
        assert len(kinds) == tn // LANES and LANES % rot_hd == 0
        spt = seq // tm
        tab = pl.BlockSpec((cos.shape[0], tm, LANES), lambda i, j: (0, i % spt, 0))
        in_specs += [tab, tab]
        args += [cos, sin]
    if residue_dil:
        assert rot is not None and tm == seq and seq % residue_dil == 0
        out_spec = pl.BlockSpec((tn // LANES, tm, LANES), lambda i, j: (j, i, 0))
        out_shape = jax.ShapeDtypeStruct((n // LANES, m, LANES), out_dtype)
        if residue_dil > 1:
            scratch = [pltpu.VMEM((tm, LANES), F32)]
    else:
        out_spec = pl.BlockSpec((tm, tn), lambda i, j: (i, j))
        out_shape = jax.ShapeDtypeStruct((m, n), out_dtype)
    return pl.pallas_call(
        functools.partial(_proj_kernel, rot_hd=rot_hd, kinds=kinds, residue_dil=residue_dil,
                          sigmoid_2x=sigmoid_2x),
        grid=(m // tm, n // tn),
        in_specs=in_specs,
        out_specs=out_spec,
        out_shape=out_shape,
        scratch_shapes=scratch,
        compiler_params=_cparams(("parallel", "arbitrary")),
        name=name,
    )(*args)


def _rope_tables(seq, head_dim, scales):
    half = head_dim // 2
    inv_freq = jnp.power(ROPE_THETA, -jnp.arange(half, dtype=F32) / half)
    ang = jnp.arange(seq, dtype=F32)[:, None] * inv_freq[None, :]
    lane = np.arange(LANES)
    f_idx = lane % half
    sign = np.where((lane % head_dim) < half, -1.0, 1.0).astype(np.float32)
    cos = jnp.cos(ang)[:, f_idx]
    sin = jnp.sin(ang)[:, f_idx] * sign[None, :]
    return (jnp.stack([cos * F32(s) for s in scales]), jnp.stack([sin * F32(s) for s in scales]))


def _merge_kernel(ya_ref, yb_ref, yc_ref, yd_ref, g_ref, x_ref, wb_ref, wo_ref, bo_ref, lg_ref,
                  lb_ref, of_ref, *maybe_ob_ref):
    merged = None
    for i, y_ref in enumerate((ya_ref, yb_ref, yc_ref, yd_ref)):
        p = jnp.dot(y_ref[...], wb_ref[i], preferred_element_type=F32)
        gate = g_ref[:, i * D_MODEL:(i + 1) * D_MODEL].astype(F32)
        merged = gate * p if merged is None else merged + gate * p
    y = jnp.dot(merged.astype(BF16), wo_ref[...], preferred_element_type=F32) + bo_ref[...]
    z = DEEPNORM_ALPHA * x_ref[...] + y
    mu = jnp.mean(z, -1, keepdims=True)
    zc = z - mu
    var = jnp.mean(zc * zc, -1, keepdims=True)
    out = zc * lax.rsqrt(var + LN_EPS) * lg_ref[...] + lb_ref[...]
    of_ref[...] = out
    for ob_ref in maybe_ob_ref:
        ob_ref[...] = out.astype(BF16)


def _merge(ya, yb, yc, yd, gates, x, wb, wo, bo, lg, lb, *, row0, rows, with_bf16, tm=512):
    outs = [F32, BF16] if with_bf16 else [F32]
    assert row0 % tm == 0 and rows % tm == 0
    o = row0 // tm
    ysp = pl.BlockSpec((tm, BW), lambda i: (i + o, 0))
    const2 = lambda i: (0, 0)
    return pl.pallas_call(
        _merge_kernel,
        grid=(rows // tm,),
        in_specs=[ysp, ysp, ysp, ysp,
                  pl.BlockSpec((tm, 4 * D_MODEL), lambda i: (i + o, 0)),
                  pl.BlockSpec((tm, D_MODEL), lambda i: (i + o, 0)),
                  pl.BlockSpec((4, BW, D_MODEL), lambda i: (0, 0, 0)),
                  pl.BlockSpec((D_MODEL, D_MODEL), const2),
                  pl.BlockSpec((1, D_MODEL), const2),
                  pl.BlockSpec((1, D_MODEL), const2),
                  pl.BlockSpec((1, D_MODEL), const2)],
        out_specs=[pl.BlockSpec((tm, D_MODEL), lambda i: (i, 0)) for _ in outs],
        out_shape=[jax.ShapeDtypeStruct((rows, D_MODEL), dt) for dt in outs],
        compiler_params=_cparams(("parallel",)),
        name="merge",
    )(ya, yb, yc, yd, gates, x, wb, wo, bo.reshape(1, -1), lg.reshape(1, -1), lb.reshape(1, -1))


NA_QR = 4


def _na_geometry(rows):
    kr, wr = min(NA_ROWS, rows), min(NA_ROWS, rows) + NA_QR - 1
    assert rows % NA_QR == 0 and rows >= wr and kr == NA_ROWS
    steps = rows // NA_QR
    a = np.arange(NA_QR)[:, None]
    j = np.arange(wr)[None, :]
    pats = []
    for g in range(steps):
        r = NA_QR * g + a
        rs = np.clip(r - kr // 2, 0, rows - kr)
        ws = int(np.clip(NA_QR * g - kr // 2, 0, rows - wr))
        assert rs.min() >= ws and rs.max() + kr <= ws + wr
        valid = (ws + j >= rs) & (ws + j < rs + kr)
        dr = np.clip(ws + j - r + NA_ROWS - 1, 0, 2 * NA_ROWS - 2)
        pats.append((valid, np.where(valid, dr, 0)))
    same = lambda x, y: np.array_equal(x[0], y[0]) and np.array_equal(x[1], y[1])
    assert steps >= 3 and all(same(pats[g], pats[1]) for g in range(1, steps - 1))
    return kr, wr, steps, [pats[0], pats[1], pats[steps - 1]]


def _na_bias_tables(rpb, rows):
    kr, wr, steps, pats = _na_geometry(rows)
    qc = np.arange(GRID_W)[:, None]
    kc = np.arange(GRID_W)[None, :]
    c_start = np.clip(qc - NA_COLS // 2, 0, GRID_W - NA_COLS)
    col_ok = (kc >= c_start) & (kc < c_start + NA_COLS)
    dc = np.clip(kc - qc + NA_COLS - 1, 0, 2 * NA_COLS - 2)
    tabs = []
    by_col = rpb.astype(F32)[:, :, dc]
    for valid, dr in pats:
        b = by_col[:, dr] * LOG2E
        ok = valid[:, :, None, None] & col_ok[None, None]
        b = jnp.where(ok[None], b, MASKED)
        tabs.append(b.transpose(0, 1, 3, 2, 4).reshape(NA_HEADS, NA_QR * GRID_W, wr * GRID_W))
    return jnp.stack(tabs, axis=1)


def _stack_heads(q, n_heads, head_dim):
    lane = lax.broadcasted_iota(jnp.int32, (1, q.shape[1]), 1)
    zero = jnp.zeros_like(q)
    return jnp.concatenate(
        [jnp.where((lane >= h * head_dim) & (lane < (h + 1) * head_dim), q, zero)
         for h in range(n_heads)], axis=0)


def _unstack_heads(o, n_heads, head_dim):
    m = o.shape[0] // n_heads
    lane = lax.broadcasted_iota(jnp.int32, (1, o.shape[1]), 1)
    acc = o[0:m]
    for h in range(1, n_heads):
        in_head = (lane >= h * head_dim) & (lane < (h + 1) * head_dim)
        acc = jnp.where(in_head, o[h * m:(h + 1) * m], acc)
    return acc


def _na_kernel(q_ref, k_ref, v_ref, g_ref, bias_ref, o_ref, *, rows):
    kr, wr = min(NA_ROWS, rows), min(NA_ROWS, rows) + NA_QR - 1
    g = pl.program_id(1)
    ws = jnp.clip(NA_QR * g - kr // 2, 0, rows - wr)
    start = pl.multiple_of(ws * GRID_W, GRID_W)
    k = k_ref[pl.ds(start, wr * GRID_W), :]
    v = v_ref[pl.ds(start, wr * GRID_W), :]
    qs = _stack_heads(q_ref[...], NA_HEADS, NA_HEAD_DIM)
    m_rows = q_ref.shape[0]
    halves = [slice(0, 2 * m_rows), slice(2 * m_rows, 4 * m_rows)]
    ss = [_dot_nt(qs[h], k) for h in halves]
    ps, ls = [], []
    for i, s in enumerate(ss):
        s = s + bias_ref[2 * i:2 * i + 2].reshape(s.shape)
        p = jnp.exp2(s - jnp.max(s, -1, keepdims=True))
        ls.append(jnp.sum(p, -1, keepdims=True))
        ps.append(p.astype(BF16))
    o = jnp.concatenate([jnp.dot(p, v, preferred_element_type=F32) / l for p, l in zip(ps, ls)], 0)
    o = _unstack_heads(o, NA_HEADS, NA_HEAD_DIM)
    o_ref[...] = (o * jax.nn.silu(g_ref[...])).astype(o_ref.dtype)


def _na(hb, fb, bias_tab, *, batch, seq):
    rows = seq // GRID_W
    kr, wr, steps, _ = _na_geometry(rows)
    qb = NA_QR * GRID_W

    def pattern_of(g):
        return jnp.where(g == 0, 0, jnp.where(g == steps - 1, 2, 1))

    return pl.pallas_call(
        functools.partial(_na_kernel, rows=rows),
        grid=(batch, steps),
        in_specs=[
            pl.BlockSpec((None, qb, BW), lambda b, g: (b, g, 0)),
            pl.BlockSpec((None, seq, BW), lambda b, g: (b, 0, 1)),
            pl.BlockSpec((None, seq, BW), lambda b, g: (b, 0, 2)),
            pl.BlockSpec((None, qb, BW), lambda b, g: (b, g, 5)),
            pl.BlockSpec((NA_HEADS, None, qb, wr * GRID_W), lambda b, g: (0, pattern_of(g), 0, 0)),
        ],
        out_specs=pl.BlockSpec((None, qb, BW), lambda b, g: (b, g, 0)),
        out_shape=jax.ShapeDtypeStruct((batch, seq, BW), BF16),
        compiler_params=_cparams(("parallel", "arbitrary")),
        name="na_attn",
    )(hb, hb, hb, fb, bias_tab)


RW_C = 64
RW_FIELDS = 9


def _mm(a, b, mode="nn", passes=1):
    dims = {"nn": (((1,), (0,)), ((), ())), "nt": (((1,), (1,)), ((), ()))}[mode]
    dg = lambda x, y: lax.dot_general(x, y, dims, preferred_element_type=F32)
    ah, bh = a.astype(BF16), b.astype(BF16)
    out = dg(ah, bh)
    if passes == 3:
        al = (a - ah.astype(F32)).astype(BF16)
        bl = (b - bh.astype(F32)).astype(BF16)
        out = out + (dg(al, bh) + dg(ah, bl))
    return out


def _mm_bd(a, b, mode="nn"):
    halves = [slice(i * LANES, (i + 1) * LANES) for i in range(BW // LANES)]
    return jnp.concatenate([_mm(a[:, h], b[h, h], mode) for h in halves], axis=1)


def _softplus(x):
    return jnp.maximum(x, 0.0) + jnp.log(1.0 + jnp.exp(-jnp.abs(x)))


def _rwprep_kernel(x_ref, prev_ref, next_ref, mu_ref, w0_ref, w2_ref, a0_ref, a2_ref, kk_ref, ka_ref,
                   rk_ref, f_ref, e_ref, *, rc):
    c = pl.program_id(1)
    lane = lax.broadcasted_iota(jnp.int32, (1, 2 * RW_LORA), 1)
    ri = lax.broadcasted_iota(jnp.int32, (rc, 1), 0)
    x = x_ref[...]
    prev_row = jnp.where(c > 0, prev_ref[7:8, :], 0.0)
    next_row = jnp.where(c < pl.num_programs(1) - 1, next_ref[0:1, :], 0.0)
    up = jnp.where(ri == 0, prev_row, pltpu.roll(x, 1, 0))
    dn = jnp.where(ri == rc - 1, next_row, pltpu.roll(x, rc - 1, 0))
    xs = x + mu_ref[...] * (0.5 * (up + dn) - x)
    r, k, v, g = (xs[:, i * BW:(i + 1) * BW] for i in range(4))
    wl = xs[:, 4 * BW:4 * BW + 2 * RW_LORA]
    al = xs[:, 4 * BW + 2 * RW_LORA:4 * BW + 4 * RW_LORA]
    kk = k * kk_ref[...]
    kap = kk * lax.rsqrt(jnp.maximum(_seg_sum(kk * kk, RW_HEAD_DIM), 1e-24))
    tw = jnp.tanh(wl)
    kd_sum = None
    for d in range(2):
        dm = ((lane >= d * RW_LORA) & (lane < (d + 1) * RW_LORA)).astype(F32)
        w_raw = w0_ref[d:d + 1, :] + _mm(tw * dm, w2_ref[...], passes=3)
        lw = -jnp.exp(-_softplus(-w_raw) - 0.5)
        a = jax.nn.sigmoid(a0_ref[d:d + 1, :] + _mm(al * dm, a2_ref[...], passes=3))
        kd = k * (1.0 + (a - 1.0) * ka_ref[...])
        f_ref[:, (3 + 3 * d) * BW:(4 + 3 * d) * BW] = lw
        f_ref[:, (4 + 3 * d) * BW:(5 + 3 * d) * BW] = kd
        f_ref[:, (5 + 3 * d) * BW:(6 + 3 * d) * BW] = kap * a
        kd_sum = kd if kd_sum is None else kd_sum + kd
    f_ref[:, 0:BW] = r
    f_ref[:, BW:2 * BW] = v
    f_ref[:, 2 * BW:3 * BW] = kap
    bonus = _seg_sum(r * kd_sum * rk_ref[...], RW_HEAD_DIM)
    e_ref[:, 0:BW] = bonus * v
    e_ref[:, BW:2 * BW] = jax.nn.silu(g)


def _rwprep(fb, mu, w0, w2, a0, a2, k_k, k_a, r_k, *, batch, seq, rc=256):
    ucols = 4 * BW + 4 * RW_LORA
    row = lambda z: z.astype(F32).reshape(1, -1)
    c2 = lambda b, c: (0, 0)
    tpb = rc // 8
    return pl.pallas_call(
        functools.partial(_rwprep_kernel, rc=rc),
        grid=(batch, seq // rc),
        in_specs=[
            pl.BlockSpec((None, rc, ucols), lambda b, c: (b, c, 0)),
            pl.BlockSpec((None, 8, ucols), lambda b, c: (b, jnp.maximum(c * tpb - 1, 0), 0)),
            pl.BlockSpec((None, 8, ucols),
                         lambda b, c: (b, jnp.minimum((c + 1) * tpb, seq // 8 - 1), 0)),
            pl.BlockSpec((1, ucols), c2),
            pl.BlockSpec((2, BW), c2),
            pl.BlockSpec((2 * RW_LORA, BW), c2),
            pl.BlockSpec((2, BW), c2),
            pl.BlockSpec((2 * RW_LORA, BW), c2),
            pl.BlockSpec((1, BW), c2),
            pl.BlockSpec((1, BW), c2),
            pl.BlockSpec((1, BW), c2),
        ],
        out_specs=[pl.BlockSpec((None, rc, RW_FIELDS * BW), lambda b, c: (b, c, 0)),
                   pl.BlockSpec((None, rc, 2 * BW), lambda b, c: (b, c, 0))],
        out_shape=[jax.ShapeDtypeStruct((batch, seq, RW_FIELDS * BW), F32),
                   jax.ShapeDtypeStruct((batch, seq, 2 * BW), F32)],
        compiler_params=_cparams(("parallel", "arbitrary")),
        name="rwkv_prep",
    )(fb, fb, fb, row(mu), w0.astype(F32), w2.astype(F32).reshape(2 * RW_LORA, BW), a0.astype(F32),
      a2.astype(F32).reshape(2 * RW_LORA, BW), row(k_k), row(k_a), row(r_k))


def _rw_bd(x):
    lane = lax.broadcasted_iota(jnp.int32, (1, BW), 1)
    xb = x.astype(BF16)
    zero = jnp.zeros_like(xb)
    return jnp.concatenate(
        [jnp.where((lane >= h * RW_C) & (lane < (h + 1) * RW_C), xb, zero) for h in range(RW_HEADS)],
        axis=0)


def _rw_advance(terms, z):
    gm, hv, nt = terms
    both = _mm_bd(gm, z)
    return both[0:RW_C] + hv, _rw_bd(both[RW_C:2 * RW_C] + nt)


def _rw_prepare(specs):
    C, H = RW_C, RW_HEADS
    assert H * C == BW and RW_HEAD_DIM == C
    bd = _rw_bd
    cat = lambda a, b: jnp.concatenate([a, b], axis=0)
    ri = lax.broadcasted_iota(jnp.int32, (C, C), 0)
    ci = lax.broadcasted_iota(jnp.int32, (C, C), 1)
    tris = [(ci <= ri).astype(BF16), (ci >= ri).astype(BF16)]
    row = lax.broadcasted_iota(jnp.int32, (C, BW), 0)
    col = lax.broadcasted_iota(jnp.int32, (C, BW), 1) % C
    eye = col == row
    strict = [col < row, col > row]
    incl = [col <= row, col >= row]

    def lc_t(x):
        t = bd(x).astype(F32).T
        return t[0:C] + t[C:2 * C] + t[2 * C:3 * C] + t[3 * C:4 * C]

    def load(spec):
        f_ref, r0, d = spec
        fld = lambda i: f_ref[r0:r0 + C, i * BW:(i + 1) * BW]
        c = dict(d=d, r=fld(0), v=fld(1), kap=fld(2), lw=fld(3 + 3 * d), kd=fld(4 + 3 * d),
                 beta=fld(5 + 3 * d))
        cs = None
        for t in _split2(c["lw"]):
            y = jnp.dot(tris[d], t, preferred_element_type=F32)
            cs = y if cs is None else cs + y
        c["cs"] = cs
        return c

    def scale(c):
        cs, d = c["cs"], c["d"]
        tot = cs[0:1] if d == 1 else cs[C - 1:C]
        ginv = jnp.exp(-cs)
        gto = jnp.exp(tot - cs)
        kt = c["kap"] * jnp.exp(cs - c["lw"])
        rt = c["r"] * jnp.exp(cs)
        c.update(gc=jnp.exp(tot), rt=rt, ktrt=cat(kt, rt), kt_b=bd(kt), v_b=bd(c["v"]),
                 kg_b=bd(c["kd"] * ginv), bg_b=bd(c["beta"] * ginv),
                 kh=c["kd"] * gto, bh=c["beta"] * gto)
        return c

    def gram(c):
        d = c["d"]
        gk = _mm_bd(c["ktrt"], c["kg_b"], "nt")
        gb = _mm_bd(c["ktrt"], c["bg_b"], "nt")
        c.update(l_kk=jnp.where(strict[d], gk[0:C], 0.0), l_kr=jnp.where(incl[d], gk[C:2 * C], 0.0),
                 l_br=jnp.where(incl[d], gb[C:2 * C], 0.0),
                 npow=-jnp.where(strict[d], gb[0:C], 0.0), w=jnp.where(eye, 1.0, 0.0))
        return c

    def level(c, last):
        if last:
            c["w"] = c["w"] + _mm_bd(c["w"], bd(c["npow"]))
        else:
            both = _mm_bd(cat(c["npow"], c["w"]), bd(c["npow"]))
            c["npow"], c["w"] = both[0:C], c["w"] + both[C:2 * C]
        return c

    def solve(c):
        c["p1_b"] = bd(_mm_bd(c["w"], c["kt_b"]))
        c["p2"] = _mm_bd(c["w"], bd(c["l_kk"]))
        c["p2_b"] = bd(c["p2"])
        return c

    def outputs(c):
        both = _mm_bd(cat(c["l_br"], c["bh_t"]), c["p1_b"])
        c["gm"] = cat(c["rt"] - both[0:C], jnp.where(eye, c["gc"], 0.0) - both[C:2 * C])
        hm = c["l_kr"] - _mm_bd(c["l_br"], c["p2_b"])
        three = _mm_bd(jnp.concatenate([hm, c["p2"], c["kh_t"]], axis=0), c["v_b"])
        c["hv"], c["p2v_b"], c["khv"] = three[0:C], bd(three[C:2 * C]), three[2 * C:3 * C]
        return c

    def state(c):
        nt = c["khv"] - _mm_bd(c["bh_t"], c["p2v_b"])
        return c["gm"].astype(BF16), c["hv"], nt

    def transposes(c):
        c["kh_t"], c["bh_t"] = lc_t(c.pop("kh")), lc_t(c.pop("bh"))
        return c

    cs = [load(s) for s in specs]
    cs = [scale(c) for c in cs]
    cs = [gram(c) for c in cs]
    cs = [transposes(c) for c in cs]
    levels = int(math.log2(C))
    for lvl in range(levels):
        cs = [level(c, lvl == levels - 1) for c in cs]
    cs = [solve(c) for c in cs]
    cs = [outputs(c) for c in cs]
    return [state(c) for c in cs]


def _rwscan_kernel(ff_ref, fr_ref, yf_ref, yr_ref, zf_ref, zr_ref, *, nch):
    @pl.when(pl.program_id(1) == 0)
    def _():
        zf_ref[...] = jnp.zeros_like(zf_ref)
        zr_ref[...] = jnp.zeros_like(zr_ref)

    C = RW_C
    pre = _rw_prepare([(ff_ref, j * C, 0) for j in range(nch)]
                      + [(fr_ref, j * C, 1) for j in range(nch)])
    pre_f, pre_r = pre[:nch], pre[nch:]
    zf, zr = zf_ref[...], zr_ref[...]
    for j in range(nch):
        yf_ref[j * C:(j + 1) * C, :], zf = _rw_advance(pre_f[j], zf)
        jr = nch - 1 - j
        yr_ref[jr * C:(jr + 1) * C, :], zr = _rw_advance(pre_r[jr], zr)
    zf_ref[...] = zf
    zr_ref[...] = zr


def _rwscan(f, *, batch, seq, nch=4):
    rb = nch * RW_C
    nc = seq // rb
    fsp = lambda imap: pl.BlockSpec((None, rb, RW_FIELDS * BW), imap)
    ysp = lambda imap: pl.BlockSpec((None, rb, BW), imap)
    fwd = lambda b, s: (b, s, 0)
    rev = lambda b, s: (b, nc - 1 - s, 0)
    y = jax.ShapeDtypeStruct((batch, seq, BW), F32)
    return pl.pallas_call(
        functools.partial(_rwscan_kernel, nch=nch),
        grid=(batch, nc),
        in_specs=[fsp(fwd), fsp(rev)],
        out_specs=[ysp(fwd), ysp(rev)],
        out_shape=[y, y],
        scratch_shapes=[pltpu.VMEM((BW, BW), BF16), pltpu.VMEM((BW, BW), BF16)],
        compiler_params=_cparams(("parallel", "arbitrary")),
        name="rwkv_scan",
    )(f, f)


def _rwpost_kernel(yf_ref, yr_ref, e_ref, lg_ref, lb_ref, o_ref):
    y = yf_ref[...] + yr_ref[...]
    inv = 1.0 / RW_HEAD_DIM
    yc = y - _seg_sum(y, RW_HEAD_DIM) * inv
    var = _seg_sum(yc * yc, RW_HEAD_DIM) * inv
    yn = yc * lax.rsqrt(var + RW_LNX_EPS) * lg_ref[...] + lb_ref[...]
    o_ref[...] = ((yn + e_ref[:, 0:BW]) * e_ref[:, BW:2 * BW]).astype(o_ref.dtype)


def _rwpost(yf, yr, e, lnx_g, lnx_b, *, tm=512):
    m = yf.shape[0]
    row = lambda z: z.astype(F32).reshape(1, -1)
    ysp = pl.BlockSpec((tm, BW), lambda i: (i, 0))
    return pl.pallas_call(
        _rwpost_kernel,
        grid=(m // tm,),
        in_specs=[ysp, ysp, pl.BlockSpec((tm, 2 * BW), lambda i: (i, 0)),
                  pl.BlockSpec((1, BW), lambda i: (0, 0)), pl.BlockSpec((1, BW), lambda i: (0, 0))],
        out_specs=ysp,
        out_shape=jax.ShapeDtypeStruct((m, BW), BF16),
        compiler_params=_cparams(("parallel",)),
        name="rwkv_post",
    )(yf, yr, e, row(lnx_g), row(lnx_b))


def _df_kernel(lam_ref, sg_ref, q_ref, k_ref, v_ref, g_ref, o_ref, *, lam_init):
    lam = lam_ref[...]
    e1 = jnp.exp(jnp.sum(lam[0:1] * lam[1:2], -1, keepdims=True))
    e2 = jnp.exp(jnp.sum(lam[2:3] * lam[3:4], -1, keepdims=True))
    lam_full = e1 - e2 + lam_init
    d = DF_HEAD_DIM
    tq = q_ref.shape[0]
    qs = _stack_heads(q_ref[...], 2 * DF_HEADS, d)
    k, v = k_ref[...], v_ref[...]
    ss = [_dot_nt(qs[2 * h * tq:2 * (h + 1) * tq], k) for h in range(DF_HEADS)]
    ws, r1s = [], []
    for s in ss:
        e = jnp.exp2(s - jnp.max(s, -1, keepdims=True))
        l = jnp.sum(e, -1, keepdims=True)
        r1 = 1.0 / l[0:tq]
        ws.append((e[0:tq] - e[tq:2 * tq] * (lam_full * l[0:tq] / l[tq:2 * tq])).astype(BF16))
        r1s.append(r1)
    os_ = [jnp.dot(w, v, preferred_element_type=F32) * r1 for w, r1 in zip(ws, r1s)]
    lane = lax.broadcasted_iota(jnp.int32, (1, BW), 1)
    acc = None
    for h, o in enumerate(os_):
        in_head = (lane >= 2 * h * d) & (lane < 2 * (h + 1) * d)
        acc = o if acc is None else jnp.where(in_head, o, acc)
    ms = _seg_sum(acc * acc, 2 * d) * (1.0 / (2 * d))
    o = acc * lax.rsqrt(ms + DF_EPS) * sg_ref[...] * (1.0 - lam_init)
    o_ref[...] = (o * jax.nn.silu(g_ref[...])).astype(o_ref.dtype)


def _df(cqk, hb, fb, lam, subln_g, lam_init, *, batch, seq, tq=256):
    sg = jnp.tile(subln_g.astype(F32), DF_HEADS).reshape(1, BW)
    return pl.pallas_call(
        functools.partial(_df_kernel, lam_init=lam_init),
        grid=(batch, seq // tq),
        in_specs=[
            pl.BlockSpec((4, DF_HEAD_DIM), lambda b, j: (0, 0)),
            pl.BlockSpec((1, BW), lambda b, j: (0, 0)),
            pl.BlockSpec((None, tq, BW), lambda b, j: (b, j, 0)),
            pl.BlockSpec((None, seq, BW), lambda b, j: (b, 0, 1)),
            pl.BlockSpec((None, seq, BW), lambda b, j: (b, 0, 3)),
            pl.BlockSpec((None, tq, BW), lambda b, j: (b, j, 6)),
        ],
        out_specs=pl.BlockSpec((None, tq, BW), lambda b, j: (b, j, 0)),
        out_shape=jax.ShapeDtypeStruct((batch, seq, BW), BF16),
        compiler_params=_cparams(("parallel", "arbitrary")),
        name="diff_attn",
    )(lam.astype(F32), sg, cqk, cqk, hb, fb)


DL_QB = 2 * DL_HALF
DL_UNROLL = 4


def _dl_group(qkv_ref, kpad, vpad, oacc, lacc, gidx, dil, seq):
    L = seq // dil
    assert L % DL_QB == 0 and (seq // DL_QB) % DL_UNROLL == 0
    nblk = L // DL_QB
    seg = L + 2 * DL_HALF
    zpad = jnp.zeros((DL_HALF, BW), BF16)

    def stage(rho, carry):
        src = pl.ds(pl.multiple_of(rho * L, DL_QB), L)
        krow = pl.multiple_of(rho * seg, DL_HALF)
        for ref in (kpad, vpad):
            ref[pl.ds(krow, DL_HALF), :] = zpad
            ref[pl.ds(krow + DL_HALF + L, DL_HALF), :] = zpad
        for c in range(BW // LANES):
            cols = slice(c * LANES, (c + 1) * LANES)
            kpad[pl.ds(krow + DL_HALF, L), cols] = qkv_ref[2 + c, src, :]
            vpad[pl.ds(krow + DL_HALF, L), cols] = qkv_ref[4 + c, src, :]
        return carry

    lax.fori_loop(0, dil, stage, 0)

    hq = DL_HEADS * DL_QB
    qi = lax.broadcasted_iota(jnp.int32, (hq, 2 * DL_QB), 0) % DL_QB
    ji = lax.broadcasted_iota(jnp.int32, (hq, 2 * DL_QB), 1)
    band = jnp.where(jnp.abs(ji - DL_HALF - qi) <= DL_HALF, 0.0, MASKED)
    jrow = lax.broadcasted_iota(jnp.int32, (1, 2 * DL_QB), 1)
    lane = lax.broadcasted_iota(jnp.int32, (1, BW), 1)

    def blocks(it, carry):
        fs = [it * DL_UNROLL + u for u in range(DL_UNROLL)]
        rhos = [f // nblk for f in fs]
        ns = [f % nblk for f in fs]
        qrows = [pl.ds(pl.multiple_of(f * DL_QB, DL_QB), DL_QB) for f in fs]
        qst = [_stack_heads(jnp.concatenate([qkv_ref[c, r, :] for c in range(BW // LANES)], axis=1),
                            DL_HEADS, DL_HEAD_DIM) for r in qrows]
        krows = [pl.multiple_of(rho * seg + n * DL_QB, DL_HALF) for rho, n in zip(rhos, ns)]
        ss = [_dot_nt(q, kpad[pl.ds(kr, 2 * DL_QB), :]) for q, kr in zip(qst, krows)]
        ps, lses, dens = [], [], []
        for s, n in zip(ss, ns):
            kpos = n * DL_QB - DL_HALF + jrow
            s = s + band + jnp.where((kpos >= 0) & (kpos < L), 0.0, MASKED)
            m = jnp.max(s, -1, keepdims=True)
            p = jnp.exp2(s - m)
            den = jnp.sum(p, -1, keepdims=True)
            ps.append(p.astype(BF16))
            dens.append(den)
            lses.append(m + jnp.log2(den))
        os_ = [jnp.dot(p, vpad[pl.ds(kr, 2 * DL_QB), :], preferred_element_type=F32) / den
               for p, kr, den in zip(ps, krows, dens)]
        for o, lse, rho, n in zip(os_, lses, rhos, ns):
            acc_o = _unstack_heads(o, DL_HEADS, DL_HEAD_DIM)
            acc_l = _unstack_heads(jnp.broadcast_to(lse, (hq, BW)), DL_HEADS, DL_HEAD_DIM)
            if dil > 1:
                dst = pl.ds(rho + dil * n * DL_QB, DL_QB, stride=dil)
            else:
                dst = pl.ds(pl.multiple_of(n * DL_QB, DL_QB), DL_QB)
            for c in range(BW // LANES):
                oacc[gidx, c, dst, :] = acc_o[:, c * LANES:(c + 1) * LANES]
                lacc[gidx, c, dst, :] = acc_l[:, c * LANES:(c + 1) * LANES]
        return carry

    lax.fori_loop(0, seq // DL_QB // DL_UNROLL, blocks, 0)


def _dl_kernel(*refs, seq):
    ng = len(DL_DILATIONS)
    qkv_refs, (g_ref, o_ref, kpad, vpad, oacc, lacc) = refs[:ng], refs[ng:]
    gi = pl.program_id(1)
    for gidx, dil in enumerate(DL_DILATIONS):
        @pl.when(gi == gidx)
        def _(gidx=gidx, dil=dil):
            _dl_group(qkv_refs[gidx], kpad, vpad, oacc, lacc, gidx, dil, seq)

    @pl.when(gi == len(DL_DILATIONS) - 1)
    def _():
        rc = 256

        def chunk(c, carry):
            rows = pl.ds(pl.multiple_of(c * rc, rc), rc)
            for lc in range(BW // LANES):
                cols = slice(lc * LANES, (lc + 1) * LANES)
                ls = [lacc[g, lc, rows, :] for g in range(len(DL_DILATIONS))]
                mx = functools.reduce(jnp.maximum, ls)
                ws = [jnp.exp2(l - mx) for l in ls]
                num = sum(w * oacc[g, lc, rows, :] for g, w in enumerate(ws))
                out = num / sum(ws)
                o_ref[rows, cols] = (out * jax.nn.silu(g_ref[rows, cols])).astype(o_ref.dtype)
            return carry

        lax.fori_loop(0, seq // rc, chunk, 0)


def _dl(dqkvs, fb, *, batch, seq):
    ng = len(DL_DILATIONS)
    nch = 3 * BW // LANES
    assert all(w // (2 * d) == DL_HALF for w, d in zip(DL_WINDOWS, DL_DILATIONS))
    return pl.pallas_call(
        functools.partial(_dl_kernel, seq=seq),
        grid=(batch, ng),
        in_specs=[pl.BlockSpec((nch, None, seq, LANES), lambda b, g: (0, b, 0, 0))] * ng
        + [pl.BlockSpec((None, seq, BW), lambda b, g: (b, 0, 7))],
        out_specs=pl.BlockSpec((None, seq, BW), lambda b, g: (b, 0, 0)),
        out_shape=jax.ShapeDtypeStruct((batch, seq, BW), BF16),
        scratch_shapes=[
            pltpu.VMEM((seq + 2 * DL_HALF * max(DL_DILATIONS), BW), BF16),
            pltpu.VMEM((seq + 2 * DL_HALF * max(DL_DILATIONS), BW), BF16),
            pltpu.VMEM((ng, BW // LANES, seq, LANES), F32),
            pltpu.VMEM((ng, BW // LANES, seq, LANES), F32),
        ],
        compiler_params=_cparams(("parallel", "arbitrary")),
        name="dilated_attn",
    )(*dqkvs, fb)


_A0 = 0
_B0 = _A0 + 4 * BW
_C0 = _B0 + 4 * BW + 4 * RW_LORA
_D0 = _C0 + 4 * BW
_DG0 = _D0 + 9 * BW
_G0 = _DG0 + BW
PROJ_TM = 2048


def _layer(xf, xb, l, p, tabs, *, batch, seq, last_split):
    w, b = p["w_in"][l], p["b_in"][l]
    cols = lambda lo, n: (w[:, lo:lo + n], b[lo:lo + n])
    cat = lambda parts: (jnp.concatenate([q[0] for q in parts], 1).astype(BF16),
                         jnp.concatenate([q[1] for q in parts], 0).astype(F32))
    scaled = lambda part, s: (part[0] * s, part[1] * s)
    q_scale = NA_HEAD_DIM ** -0.5 * LOG2E
    proj = functools.partial(_proj, xb, seq=seq, tm=PROJ_TM)

    gates = proj(*cat([scaled(cols(_G0, 4 * D_MODEL), 0.5)]), BF16, tn=1024, sigmoid_2x=True,
                 name="proj_gates")
    fb = proj(*cat([cols(_B0, _C0 - _B0), cols(_A0 + 3 * BW, BW), cols(_C0 + 3 * BW, BW),
                    cols(_DG0, BW)]), F32, tn=1024, name="proj_f32")
    hb = proj(*cat([scaled(cols(_A0, BW), q_scale), cols(_A0 + BW, 2 * BW), cols(_C0 + 2 * BW, BW)]),
              BF16, tn=1024, name="proj_bf16")
    cpb = BW // LANES
    cqk = proj(*cat([cols(_C0, 2 * BW)]), BF16, tn=2 * BW, name="proj_rot_c",
               rot=(DF_HEAD_DIM,) + tabs["c"] + ((0,) * cpb + (1,) * cpb,))
    dqs = [proj(*cat([scaled(cols(_D0 + 3 * g * BW, BW), q_scale),
                      cols(_D0 + (3 * g + 1) * BW, 2 * BW)]),
                BF16, tn=3 * BW, residue_dil=dil, name="proj_rot_d",
                rot=(DL_HEAD_DIM,) + tabs["d"] + ((0,) * (2 * cpb) + (None,) * cpb,))
           for g, dil in enumerate(DL_DILATIONS)]

    r3 = lambda z: z.reshape(batch, seq, z.shape[-1])
    hb3, fb3 = r3(hb), r3(fb)
    ya = _na(hb3, fb3, _na_bias_tables(p["na_rpb"][l], seq // GRID_W), batch=batch, seq=seq)
    f, e = _rwprep(fb3, p["rw_mu"][l], p["rw_w0"][l], p["rw_w2"][l], p["rw_a0"][l], p["rw_a2"][l],
                   p["rw_kk"][l], p["rw_ka"][l], p["rw_rk"][l].reshape(-1), batch=batch, seq=seq)
    yf, yr = _rwscan(f, batch=batch, seq=seq)
    m = batch * seq
    yb = _rwpost(yf.reshape(m, BW), yr.reshape(m, BW), e.reshape(m, 2 * BW), p["rw_lnx_g"][l],
                 p["rw_lnx_b"][l])
    lam_init = 0.8 - 0.6 * math.exp(-0.3 * l)
    yc = _df(r3(cqk), hb3, fb3, p["df_lam"][l], p["df_subln_g"][l], lam_init, batch=batch, seq=seq)
    yd = _dl([dq.reshape(dq.shape[0], batch, seq, LANES) for dq in dqs], fb3, batch=batch, seq=seq)

    margs = (ya.reshape(m, BW), yb, yc.reshape(m, BW), yd.reshape(m, BW), gates, xf,
             p["w_branch"][l].astype(BF16), p["w_out"][l].astype(BF16), p["b_out"][l],
             p["ln_g"][l], p["ln_b"][l])
    if last_split is None:
        return _merge(*margs, row0=0, rows=m, with_bf16=True)
    return [_merge(*margs, row0=r0, rows=n, with_bf16=False)[0] for r0, n in last_split]


def kernel(x_prompt, x_sample, ln0_g, ln0_b, w_in, b_in, na_rpb, rw_mu, rw_w0, rw_w2, rw_a0, rw_a2,
           rw_kk, rw_ka, rw_rk, rw_lnx_g, rw_lnx_b, df_lam, df_subln_g, w_branch, w_out, b_out,
           ln_g, ln_b):
    p = dict(w_in=w_in, b_in=b_in, na_rpb=na_rpb, rw_mu=rw_mu, rw_w0=rw_w0, rw_w2=rw_w2,
             rw_a0=rw_a0, rw_a2=rw_a2, rw_kk=rw_kk, rw_ka=rw_ka, rw_rk=rw_rk, rw_lnx_g=rw_lnx_g,
             rw_lnx_b=rw_lnx_b, df_lam=df_lam, df_subln_g=df_subln_g, w_branch=w_branch,
             w_out=w_out, b_out=b_out, ln_g=ln_g, ln_b=ln_b)
    bp, seq, _ = x_prompt.shape
    bs = x_sample.shape[0]
    assert x_sample.shape[1] == seq
    batch = bp + bs
    xf, xb = _ln0(x_prompt.reshape(bp * seq, D_MODEL), x_sample.reshape(bs * seq, D_MODEL),
                  ln0_g, ln0_b)
    tabs = {"c": _rope_tables(seq, DF_HEAD_DIM, (DF_HEAD_DIM ** -0.5 * LOG2E, 1.0)),
            "d": _rope_tables(seq, DL_HEAD_DIM, (1.0,))}
    for l in range(DEPTH - 1):
        xf, xb = _layer(xf, xb, l, p, tabs, batch=batch, seq=seq, last_split=None)
    split = [(0, bp * seq), (bp * seq, bs * seq)]
    yp, ys = _layer(xf, xb, DEPTH - 1, p, tabs, batch=batch, seq=seq, last_split=split)
    return yp.reshape(bp, seq, D_MODEL), ys.reshape(bs, seq, D_MODEL)
```

```python
import functools
import math

import numpy as np
import jax
import jax.numpy as jnp
from jax import lax
from jax.experimental import pallas as pl
from jax.experimental.pallas import tpu as pltpu

F32 = jnp.float32
BF16 = jnp.bfloat16

D_MODEL = 1024
DEPTH = 2
GRID_W = 64
NA_HEADS, NA_HEAD_DIM, NA_ROWS, NA_COLS = 4, 64, 8, 16
RW_HEADS, RW_HEAD_DIM, RW_LORA = 4, 64, 64
RW_LNX_EPS = 64e-5
DF_HEADS, DF_HEAD_DIM, DF_EPS = 4, 32, 1e-5
DL_HEADS, DL_HEAD_DIM = 4, 64
DL_DILATIONS = (1, 4, 16)
DL_WINDOWS = (128, 512, 2048)
DL_HALF = 64
BW = 256
ROPE_THETA = 10000.0
LN_EPS = 1e-5
DEEPNORM_ALPHA = (2 * DEPTH) ** 0.25
LOG2E = math.log2(math.e)
MASKED = -float("inf")

LANES = 128
VMEM_LIMIT = 56 * 1024 * 1024


def _cparams(sem):
    return pltpu.CompilerParams(dimension_semantics=sem, vmem_limit_bytes=VMEM_LIMIT)


def _dot_nt(a, b):
    return lax.dot_general(a, b, (((1,), (1,)), ((), ())), preferred_element_type=F32)


def _split2(x):
    hi = x.astype(BF16)
    return hi, (x - hi.astype(F32)).astype(BF16)


def _seg_sum(x, seg):
    n = x.shape[-1]
    r = lax.broadcasted_iota(jnp.int32, (n, n), 0) // seg
    c = lax.broadcasted_iota(jnp.int32, (n, n), 1) // seg
    ones = (r == c).astype(BF16)
    out = None
    for t in _split2(x):
        y = jnp.dot(t, ones, preferred_element_type=F32)
        out = y if out is None else out + y
    return out


def _ln0_kernel(xp_ref, xs_ref, g_ref, b_ref, of_ref, ob_ref, *, n_prompt):
    def norm(x):
        mu = jnp.mean(x, -1, keepdims=True)
        xc = x - mu
        var = jnp.mean(xc * xc, -1, keepdims=True)
        y = xc * lax.rsqrt(var + LN_EPS) * g_ref[...] + b_ref[...]
        of_ref[...] = y
        ob_ref[...] = y.astype(BF16)

    i = pl.program_id(0)

    @pl.when(i < n_prompt)
    def _():
        norm(xp_ref[...])

    @pl.when(i >= n_prompt)
    def _():
        norm(xs_ref[...])


def _ln0(xp, xs, g, b, tm=512):
    mp, ms = xp.shape[0], xs.shape[0]
    n_p, n_s = mp // tm, ms // tm
    out = jax.ShapeDtypeStruct((mp + ms, D_MODEL), F32)
    outb = jax.ShapeDtypeStruct((mp + ms, D_MODEL), BF16)
    return pl.pallas_call(
        functools.partial(_ln0_kernel, n_prompt=n_p),
        grid=(n_p + n_s,),
        in_specs=[
            pl.BlockSpec((tm, D_MODEL), lambda i: (jnp.minimum(i, n_p - 1), 0)),
            pl.BlockSpec((tm, D_MODEL), lambda i: (jnp.maximum(i - n_p, 0), 0)),
            pl.BlockSpec((1, D_MODEL), lambda i: (0, 0)),
            pl.BlockSpec((1, D_MODEL), lambda i: (0, 0)),
        ],
        out_specs=[pl.BlockSpec((tm, D_MODEL), lambda i: (i, 0)),
                   pl.BlockSpec((tm, D_MODEL), lambda i: (i, 0))],
        out_shape=[out, outb],
        compiler_params=_cparams(("parallel",)),
        name="ln0",
    )(xp, xs, g.reshape(1, -1), b.reshape(1, -1))


def _proj_kernel(x_ref, w_ref, b_ref, *rest, rot_hd, kinds, residue_dil, sigmoid_2x):
    acc = jnp.dot(x_ref[...], w_ref[...], preferred_element_type=F32) + b_ref[...]
    if not rot_hd:
        (o_ref,) = rest
        o_ref[...] = (0.5 * jnp.tanh(acc) + 0.5 if sigmoid_2x else acc).astype(o_ref.dtype)
        return
    cs_ref, sn_ref, o_ref = rest[:3]
    h2 = rot_hd // 2
    lane = lax.broadcasted_iota(jnp.int32, (1, LANES), 1)
    first = (lane % rot_hd) < h2
    for c, kind in enumerate(kinds):
        val = acc[:, c * LANES:(c + 1) * LANES]
        if kind is not None:
            sw = jnp.where(first, pltpu.roll(val, LANES - h2, 1), pltpu.roll(val, h2, 1))
            val = val * cs_ref[kind] + sw * sn_ref[kind]
        if not residue_dil:
            o_ref[:, c * LANES:(c + 1) * LANES] = val.astype(o_ref.dtype)
        elif residue_dil == 1:
            o_ref[c] = val.astype(o_ref.dtype)
        else:
            tmp_ref = rest[3]
            tmp_ref[...] = val
            L = val.shape[0] // residue_dil
            for rho in range(residue_dil):
                o_ref[c, rho * L:(rho + 1) * L, :] = (
                    tmp_ref[pl.ds(rho, L, stride=residue_dil), :].astype(o_ref.dtype))


def _proj(xb, w, b, out_dtype, *, seq, tm, tn, rot=None, residue_dil=0, sigmoid_2x=False,
          name="proj"):
    m, n = xb.shape[0], w.shape[1]
    assert m % tm == 0 and n % tn == 0 and seq % tm == 0
    in_specs = [
        pl.BlockSpec((tm, D_MODEL), lambda i, j: (i, 0)),
        pl.BlockSpec((D_MODEL, tn), lambda i, j: (0, j)),
        pl.BlockSpec((1, tn), lambda i, j: (0, j)),
    ]
    args = [xb, w, b.reshape(1, -1)]
    rot_hd, kinds, scratch = 0, None, []
    if rot is not None:
        rot_hd, cos, sin, kinds = rot
        assert len(kinds) == tn // LANES and LANES % rot_hd == 0
        spt = seq // tm
        tab = pl.BlockSpec((cos.shape[0], tm, LANES), lambda i, j: (0, i % spt, 0))
        in_specs += [tab, tab]
        args += [cos, sin]
    if residue_dil:
        assert rot is not None and tm == seq and seq % residue_dil == 0
        out_spec = pl.BlockSpec((tn // LANES, tm, LANES), lambda i, j: (j, i, 0))
        out_shape = jax.ShapeDtypeStruct((n // LANES, m, LANES), out_dtype)
        if residue_dil > 1:
            scratch = [pltpu.VMEM((tm, LANES), F32)]
    else:
        out_spec = pl.BlockSpec((tm, tn), lambda i, j: (i, j))
        out_shape = jax.ShapeDtypeStruct((m, n), out_dtype)
    return pl.pallas_call(
        functools.partial(_proj_kernel, rot_hd=rot_hd, kinds=kinds, residue_dil=residue_dil,
                          sigmoid_2x=sigmoid_2x),
        grid=(m // tm, n // tn),
        in_specs=in_specs,
        out_specs=out_spec,
        out_shape=out_shape,
        scratch_shapes=scratch,
        compiler_params=_cparams(("parallel", "arbitrary")),
        name=name,
    )(*args)


def _rope_tables(seq, head_dim, scales):
    half = head_dim // 2
    inv_freq = jnp.power(ROPE_THETA, -jnp.arange(half, dtype=F32) / half)
    ang = jnp.arange(seq, dtype=F32)[:, None] * inv_freq[None, :]
    lane = np.arange(LANES)
    f_idx = lane % half
    sign = np.where((lane % head_dim) < half, -1.0, 1.0).astype(np.float32)
    cos = jnp.cos(ang)[:, f_idx]
    sin = jnp.sin(ang)[:, f_idx] * sign[None, :]
    return (jnp.stack([cos * F32(s) for s in scales]), jnp.stack([sin * F32(s) for s in scales]))


def _merge_kernel(ya_ref, yb_ref, yc_ref, yd_ref, g_ref, x_ref, wb_ref, wo_ref, bo_ref, lg_ref,
                  lb_ref, of_ref, *maybe_ob_ref):
    merged = None
    for i, y_ref in enumerate((ya_ref, yb_ref, yc_ref, yd_ref)):
        p = jnp.dot(y_ref[...], wb_ref[i], preferred_element_type=F32)
        gate = g_ref[:, i * D_MODEL:(i + 1) * D_MODEL].astype(F32)
        merged = gate * p if merged is None else merged + gate * p
    y = jnp.dot(merged.astype(BF16), wo_ref[...], preferred_element_type=F32) + bo_ref[...]
    z = DEEPNORM_ALPHA * x_ref[...] + y
    mu = jnp.mean(z, -1, keepdims=True)
    zc = z - mu
    var = jnp.mean(zc * zc, -1, keepdims=True)
    out = zc * lax.rsqrt(var + LN_EPS) * lg_ref[...] + lb_ref[...]
    of_ref[...] = out
    for ob_ref in maybe_ob_ref:
        ob_ref[...] = out.astype(BF16)


def _merge(ya, yb, yc, yd, gates, x, wb, wo, bo, lg, lb, *, row0, rows, with_bf16, tm=512):
    outs = [F32, BF16] if with_bf16 else [F32]
    assert row0 % tm == 0 and rows % tm == 0
    o = row0 // tm
    ysp = pl.BlockSpec((tm, BW), lambda i: (i + o, 0))
    const2 = lambda i: (0, 0)
    return pl.pallas_call(
        _merge_kernel,
        grid=(rows // tm,),
        in_specs=[ysp, ysp, ysp, ysp,
                  pl.BlockSpec((tm, 4 * D_MODEL), lambda i: (i + o, 0)),
                  pl.BlockSpec((tm, D_MODEL), lambda i: (i + o, 0)),
                  pl.BlockSpec((4, BW, D_MODEL), lambda i: (0, 0, 0)),
                  pl.BlockSpec((D_MODEL, D_MODEL), const2),
                  pl.BlockSpec((1, D_MODEL), const2),
                  pl.BlockSpec((1, D_MODEL), const2),
                  pl.BlockSpec((1, D_MODEL), const2)],
        out_specs=[pl.BlockSpec((tm, D_MODEL), lambda i: (i, 0)) for _ in outs],
        out_shape=[jax.ShapeDtypeStruct((rows, D_MODEL), dt) for dt in outs],
        compiler_params=_cparams(("parallel",)),
        name="merge",
    )(ya, yb, yc, yd, gates, x, wb, wo, bo.reshape(1, -1), lg.reshape(1, -1), lb.reshape(1, -1))


NA_QR = 4


def _na_geometry(rows):
    kr, wr = min(NA_ROWS, rows), min(NA_ROWS, rows) + NA_QR - 1
    assert rows % NA_QR == 0 and rows >= wr and kr == NA_ROWS
    steps = rows // NA_QR
    a = np.arange(NA_QR)[:, None]
    j = np.arange(wr)[None, :]
    pats = []
    for g in range(steps):
        r = NA_QR * g + a
        rs = np.clip(r - kr // 2, 0, rows - kr)
        ws = int(np.clip(NA_QR * g - kr // 2, 0, rows - wr))
        assert rs.min() >= ws and rs.max() + kr <= ws + wr
        valid = (ws + j >= rs) & (ws + j < rs + kr)
        dr = np.clip(ws + j - r + NA_ROWS - 1, 0, 2 * NA_ROWS - 2)
        pats.append((valid, np.where(valid, dr, 0)))
    same = lambda x, y: np.array_equal(x[0], y[0]) and np.array_equal(x[1], y[1])
    assert steps >= 3 and all(same(pats[g], pats[1]) for g in range(1, steps - 1))
    return kr, wr, steps, [pats[0], pats[1], pats[steps - 1]]


def _na_bias_tables(rpb, rows):
    kr, wr, steps, pats = _na_geometry(rows)
    qc = np.arange(GRID_W)[:, None]
    kc = np.arange(GRID_W)[None, :]
    c_start = np.clip(qc - NA_COLS // 2, 0, GRID_W - NA_COLS)
    col_ok = (kc >= c_start) & (kc < c_start + NA_COLS)
    dc = np.clip(kc - qc + NA_COLS - 1, 0, 2 * NA_COLS - 2)
    tabs = []
    by_col = rpb.astype(F32)[:, :, dc]
    for valid, dr in pats:
        b = by_col[:, dr] * LOG2E
        ok = valid[:, :, None, None] & col_ok[None, None]
        b = jnp.where(ok[None], b, MASKED)
        tabs.append(b.transpose(0, 1, 3, 2, 4).reshape(NA_HEADS, NA_QR * GRID_W, wr * GRID_W))
    return jnp.stack(tabs, axis=1)


def _stack_heads(q, n_heads, head_dim):
    lane = lax.broadcasted_iota(jnp.int32, (1, q.shape[1]), 1)
    zero = jnp.zeros_like(q)
    return jnp.concatenate(
        [jnp.where((lane >= h * head_dim) & (lane < (h + 1) * head_dim), q, zero)
         for h in range(n_heads)], axis=0)


def _unstack_heads(o, n_heads, head_dim):
    m = o.shape[0] // n_heads
    lane = lax.broadcasted_iota(jnp.int32, (1, o.shape[1]), 1)
    acc = o[0:m]
    for h in range(1, n_heads):
        in_head = (lane >= h * head_dim) & (lane < (h + 1) * head_dim)
        acc = jnp.where(in_head, o[h * m:(h + 1) * m], acc)
    return acc


def _na_kernel(q_ref, k_ref, v_ref, g_ref, bias_ref, o_ref, *, rows):
    kr, wr = min(NA_ROWS, rows), min(NA_ROWS, rows) + NA_QR - 1
    g = pl.program_id(1)
    ws = jnp.clip(NA_QR * g - kr // 2, 0, rows - wr)
    start = pl.multiple_of(ws * GRID_W, GRID_W)
    k = k_ref[pl.ds(start, wr * GRID_W), :]
    v = v_ref[pl.ds(start, wr * GRID_W), :]
    qs = _stack_heads(q_ref[...], NA_HEADS, NA_HEAD_DIM)
    m_rows = q_ref.shape[0]
    halves = [slice(0, 2 * m_rows), slice(2 * m_rows, 4 * m_rows)]
    ss = [_dot_nt(qs[h], k) for h in halves]
    ps, ls = [], []
    for i, s in enumerate(ss):
        s = s + bias_ref[2 * i:2 * i + 2].reshape(s.shape)
        p = jnp.exp2(s - jnp.max(s, -1, keepdims=True))
        ls.append(jnp.sum(p, -1, keepdims=True))
        ps.append(p.astype(BF16))
    o = jnp.concatenate([jnp.dot(p, v, preferred_element_type=F32) / l for p, l in zip(ps, ls)], 0)
    o = _unstack_heads(o, NA_HEADS, NA_HEAD_DIM)
    o_ref[...] = (o * jax.nn.silu(g_ref[...])).astype(o_ref.dtype)


def _na(hb, fb, bias_tab, *, batch, seq):
    rows = seq // GRID_W
    kr, wr, steps, _ = _na_geometry(rows)
    qb = NA_QR * GRID_W

    def pattern_of(g):
        return jnp.where(g == 0, 0, jnp.where(g == steps - 1, 2, 1))

    return pl.pallas_call(
        functools.partial(_na_kernel, rows=rows),
        grid=(batch, steps),
        in_specs=[
            pl.BlockSpec((None, qb, BW), lambda b, g: (b, g, 0)),
            pl.BlockSpec((None, seq, BW), lambda b, g: (b, 0, 1)),
            pl.BlockSpec((None, seq, BW), lambda b, g: (b, 0, 2)),
            pl.BlockSpec((None, qb, BW), lambda b, g: (b, g, 5)),
            pl.BlockSpec((NA_HEADS, None, qb, wr * GRID_W), lambda b, g: (0, pattern_of(g), 0, 0)),
        ],
        out_specs=pl.BlockSpec((None, qb, BW), lambda b, g: (b, g, 0)),
        out_shape=jax.ShapeDtypeStruct((batch, seq, BW), BF16),
        compiler_params=_cparams(("parallel", "arbitrary")),
        name="na_attn",
    )(hb, hb, hb, fb, bias_tab)


RW_C = 64
RW_FIELDS = 9
RW_INV_BASE = 4


def _mm(a, b, mode="nn", passes=1):
    dims = {"nn": (((1,), (0,)), ((), ())), "nt": (((1,), (1,)), ((), ()))}[mode]
    dg = lambda x, y: lax.dot_general(x, y, dims, preferred_element_type=F32)
    ah, bh = a.astype(BF16), b.astype(BF16)
    out = dg(ah, bh)
    if passes == 3:
        al = (a - ah.astype(F32)).astype(BF16)
        bl = (b - bh.astype(F32)).astype(BF16)
        out = out + (dg(al, bh) + dg(ah, bl))
    return out


def _mm_bd(a, b, mode="nn"):
    halves = [slice(i * LANES, (i + 1) * LANES) for i in range(BW // LANES)]
    return jnp.concatenate([_mm(a[:, h], b[h, h], mode) for h in halves], axis=1)


def _softplus(x):
    return jnp.maximum(x, 0.0) + jnp.log(1.0 + jnp.exp(-jnp.abs(x)))


def _rwprep_kernel(x_ref, prev_ref, next_ref, mu_ref, w0_ref, w2_ref, a0_ref, a2_ref, kk_ref, ka_ref,
                   rk_ref, f_ref, e_ref, *, rc):
    c = pl.program_id(1)
    lane = lax.broadcasted_iota(jnp.int32, (1, 2 * RW_LORA), 1)
    ri = lax.broadcasted_iota(jnp.int32, (rc, 1), 0)
    x = x_ref[...]
    prev_row = jnp.where(c > 0, prev_ref[7:8, :], 0.0)
    next_row = jnp.where(c < pl.num_programs(1) - 1, next_ref[0:1, :], 0.0)
    up = jnp.where(ri == 0, prev_row, pltpu.roll(x, 1, 0))
    dn = jnp.where(ri == rc - 1, next_row, pltpu.roll(x, rc - 1, 0))
    xs = x + mu_ref[...] * (0.5 * (up + dn) - x)
    r, k, v, g = (xs[:, i * BW:(i + 1) * BW] for i in range(4))
    wl = xs[:, 4 * BW:4 * BW + 2 * RW_LORA]
    al = xs[:, 4 * BW + 2 * RW_LORA:4 * BW + 4 * RW_LORA]
    kk = k * kk_ref[...]
    kap = kk * lax.rsqrt(jnp.maximum(_seg_sum(kk * kk, RW_HEAD_DIM), 1e-24))
    tw = jnp.tanh(wl)
    kd_sum = None
    for d in range(2):
        dm = ((lane >= d * RW_LORA) & (lane < (d + 1) * RW_LORA)).astype(F32)
        w_raw = w0_ref[d:d + 1, :] + _mm(tw * dm, w2_ref[...], passes=3)
        lw = -jnp.exp(-_softplus(-w_raw) - 0.5)
        a = jax.nn.sigmoid(a0_ref[d:d + 1, :] + _mm(al * dm, a2_ref[...], passes=3))
        kd = k * (1.0 + (a - 1.0) * ka_ref[...])
        f_ref[:, (3 + 3 * d) * BW:(4 + 3 * d) * BW] = lw
        f_ref[:, (4 + 3 * d) * BW:(5 + 3 * d) * BW] = kd
        f_ref[:, (5 + 3 * d) * BW:(6 + 3 * d) * BW] = kap * a
        kd_sum = kd if kd_sum is None else kd_sum + kd
    f_ref[:, 0:BW] = r
    f_ref[:, BW:2 * BW] = v
    f_ref[:, 2 * BW:3 * BW] = kap
    bonus = _seg_sum(r * kd_sum * rk_ref[...], RW_HEAD_DIM)
    e_ref[:, 0:BW] = bonus * v
    e_ref[:, BW:2 * BW] = jax.nn.silu(g)


def _rwprep(fb, mu, w0, w2, a0, a2, k_k, k_a, r_k, *, batch, seq, rc=256):
    ucols = 4 * BW + 4 * RW_LORA
    row = lambda z: z.astype(F32).reshape(1, -1)
    c2 = lambda b, c: (0, 0)
    tpb = rc // 8
    return pl.pallas_call(
        functools.partial(_rwprep_kernel, rc=rc),
        grid=(batch, seq // rc),
        in_specs=[
            pl.BlockSpec((None, rc, ucols), lambda b, c: (b, c, 0)),
            pl.BlockSpec((None, 8, ucols), lambda b, c: (b, jnp.maximum(c * tpb - 1, 0), 0)),
            pl.BlockSpec((None, 8, ucols),
                         lambda b, c: (b, jnp.minimum((c + 1) * tpb, seq // 8 - 1), 0)),
            pl.BlockSpec((1, ucols), c2),
            pl.BlockSpec((2, BW), c2),
            pl.BlockSpec((2 * RW_LORA, BW), c2),
            pl.BlockSpec((2, BW), c2),
            pl.BlockSpec((2 * RW_LORA, BW), c2),
            pl.BlockSpec((1, BW), c2),
            pl.BlockSpec((1, BW), c2),
            pl.BlockSpec((1, BW), c2),
        ],
        out_specs=[pl.BlockSpec((None, rc, RW_FIELDS * BW), lambda b, c: (b, c, 0)),
                   pl.BlockSpec((None, rc, 2 * BW), lambda b, c: (b, c, 0))],
        out_shape=[jax.ShapeDtypeStruct((batch, seq, RW_FIELDS * BW), F32),
                   jax.ShapeDtypeStruct((batch, seq, 2 * BW), F32)],
        compiler_params=_cparams(("parallel", "arbitrary")),
        name="rwkv_prep",
    )(fb, fb, fb, row(mu), w0.astype(F32), w2.astype(F32).reshape(2 * RW_LORA, BW), a0.astype(F32),
      a2.astype(F32).reshape(2 * RW_LORA, BW), row(k_k), row(k_a), row(r_k))


def _rw_bd(x):
    lane = lax.broadcasted_iota(jnp.int32, (1, BW), 1)
    xb = x.astype(BF16)
    zero = jnp.zeros_like(xb)
    return jnp.concatenate(
        [jnp.where((lane >= h * RW_C) & (lane < (h + 1) * RW_C), xb, zero) for h in range(RW_HEADS)],
        axis=0)


def _rw_advance(terms, z):
    gm, hv, nt = terms
    both = _mm_bd(gm, z)
    return both[0:RW_C] + hv, _rw_bd(both[RW_C:2 * RW_C] + nt)


def _rw_prepare(specs):
    C, H = RW_C, RW_HEADS
    assert H * C == BW and RW_HEAD_DIM == C
    bd = _rw_bd
    cat = lambda a, b: jnp.concatenate([a, b], axis=0)
    ri = lax.broadcasted_iota(jnp.int32, (C, C), 0)
    ci = lax.broadcasted_iota(jnp.int32, (C, C), 1)
    tris = [(ci <= ri).astype(BF16), (ci >= ri).astype(BF16)]
    row = lax.broadcasted_iota(jnp.int32, (C, BW), 0)
    col = lax.broadcasted_iota(jnp.int32, (C, BW), 1) % C
    eye = col == row
    strict = [col < row, col > row]
    incl = [col <= row, col >= row]
    same_block = lambda s: (row // s) == (col // s)

    def lc_t(x):
        t = bd(x).astype(F32).T
        return t[0:C] + t[C:2 * C] + t[2 * C:3 * C] + t[3 * C:4 * C]

    def load(spec):
        f_ref, r0, d = spec
        fld = lambda i: f_ref[r0:r0 + C, i * BW:(i + 1) * BW]
        c = dict(d=d, r=fld(0), v=fld(1), kap=fld(2), lw=fld(3 + 3 * d), kd=fld(4 + 3 * d),
                 beta=fld(5 + 3 * d))
        cs = None
        for t in _split2(c["lw"]):
            y = jnp.dot(tris[d], t, preferred_element_type=F32)
            cs = y if cs is None else cs + y
        c["cs"] = cs
        return c

    def scale(c):
        cs, d = c["cs"], c["d"]
        tot = cs[0:1] if d == 1 else cs[C - 1:C]
        ginv = jnp.exp(-cs)
        gto = jnp.exp(tot - cs)
        kt = c["kap"] * jnp.exp(cs - c["lw"])
        rt = c["r"] * jnp.exp(cs)
        c.update(gc=jnp.exp(tot), rt=rt, ktrt=cat(kt, rt), kt_b=bd(kt), v_b=bd(c["v"]),
                 kg_b=bd(c["kd"] * ginv), bg_b=bd(c["beta"] * ginv),
                 kh=c["kd"] * gto, bh=c["beta"] * gto)
        return c

    def gram(c):
        d = c["d"]
        gk = _mm_bd(c["ktrt"], c["kg_b"], "nt")
        gb = _mm_bd(c["ktrt"], c["bg_b"], "nt")
        l_bk = jnp.where(strict[d], gb[0:C], 0.0)
        c.update(l_kk=jnp.where(strict[d], gk[0:C], 0.0), l_kr=jnp.where(incl[d], gk[C:2 * C], 0.0),
                 l_br=jnp.where(incl[d], gb[C:2 * C], 0.0), l_bk=l_bk,
                 nbase=-jnp.where(same_block(RW_INV_BASE), l_bk, 0.0))
        return c

    def base_square(c):
        c["nsq"] = _mm_bd(c["nbase"], bd(c["nbase"]))
        return c

    def base_inverse(c):
        w = jnp.where(eye, 1.0, 0.0) + c.pop("nbase")
        c["w"] = w + _mm_bd(w, bd(c.pop("nsq")))
        return c

    def merge_left(c, s):
        off = same_block(2 * s) & jnp.logical_not(same_block(s))
        c["t"] = _mm_bd(c["w"], bd(jnp.where(off, c["l_bk"], 0.0)))
        return c

    def merge_right(c):
        c["w"] = c["w"] - _mm_bd(c.pop("t"), bd(c["w"]))
        return c

    def solve(c):
        c["p1_b"] = bd(_mm_bd(c["w"], c["kt_b"]))
        c["p2"] = _mm_bd(c["w"], bd(c["l_kk"]))
        c["p2_b"] = bd(c["p2"])
        return c

    def outputs(c):
        both = _mm_bd(cat(c["l_br"], c["bh_t"]), c["p1_b"])
        c["gm"] = cat(c["rt"] - both[0:C], jnp.where(eye, c["gc"], 0.0) - both[C:2 * C])
        hm = c["l_kr"] - _mm_bd(c["l_br"], c["p2_b"])
        three = _mm_bd(jnp.concatenate([hm, c["p2"], c["kh_t"]], axis=0), c["v_b"])
        c["hv"], c["p2v_b"], c["khv"] = three[0:C], bd(three[C:2 * C]), three[2 * C:3 * C]
        return c

    def state(c):
        nt = c["khv"] - _mm_bd(c["bh_t"], c["p2v_b"])
        return c["gm"].astype(BF16), c["hv"], nt

    def transposes(c):
        c["kh_t"], c["bh_t"] = lc_t(c.pop("kh")), lc_t(c.pop("bh"))
        return c

    cs = [load(s) for s in specs]
    cs = [scale(c) for c in cs]
    cs = [gram(c) for c in cs]
    cs = [transposes(c) for c in cs]
    cs = [base_square(c) for c in cs]
    cs = [base_inverse(c) for c in cs]
    s = RW_INV_BASE
    while s < C:
        cs = [merge_left(c, s) for c in cs]
        cs = [merge_right(c) for c in cs]
        s *= 2
    cs = [solve(c) for c in cs]
    cs = [outputs(c) for c in cs]
    return [state(c) for c in cs]


def _rwscan_kernel(ff_ref, fr_ref, yf_ref, yr_ref, zf_ref, zr_ref, *, nch):
    @pl.when(pl.program_id(1) == 0)
    def _():
        zf_ref[...] = jnp.zeros_like(zf_ref)
        zr_ref[...] = jnp.zeros_like(zr_ref)

    C = RW_C
    pre = _rw_prepare([(ff_ref, j * C, 0) for j in range(nch)]
                      + [(fr_ref, j * C, 1) for j in range(nch)])
    pre_f, pre_r = pre[:nch], pre[nch:]
    zf, zr = zf_ref[...], zr_ref[...]
    for j in range(nch):
        yf_ref[j * C:(j + 1) * C, :], zf = _rw_advance(pre_f[j], zf)
        jr = nch - 1 - j
        yr_ref[jr * C:(jr + 1) * C, :], zr = _rw_advance(pre_r[jr], zr)
    zf_ref[...] = zf
    zr_ref[...] = zr


def _rwscan(f, *, batch, seq, nch=4):
    rb = nch * RW_C
    nc = seq // rb
    fsp = lambda imap: pl.BlockSpec((None, rb, RW_FIELDS * BW), imap)
    ysp = lambda imap: pl.BlockSpec((None, rb, BW), imap)
    fwd = lambda b, s: (b, s, 0)
    rev = lambda b, s: (b, nc - 1 - s, 0)
    y = jax.ShapeDtypeStruct((batch, seq, BW), F32)
    return pl.pallas_call(
        functools.partial(_rwscan_kernel, nch=nch),
        grid=(batch, nc),
        in_specs=[fsp(fwd), fsp(rev)],
        out_specs=[ysp(fwd), ysp(rev)],
        out_shape=[y, y],
        scratch_shapes=[pltpu.VMEM((BW, BW), BF16), pltpu.VMEM((BW, BW), BF16)],
        compiler_params=_cparams(("parallel", "arbitrary")),
        name="rwkv_scan",
    )(f, f)


def _rwpost_kernel(yf_ref, yr_ref, e_ref, lg_ref, lb_ref, o_ref):
    y = yf_ref[...] + yr_ref[...]
    inv = 1.0 / RW_HEAD_DIM
    yc = y - _seg_sum(y, RW_HEAD_DIM) * inv
    var = _seg_sum(yc * yc, RW_HEAD_DIM) * inv
    yn = yc * lax.rsqrt(var + RW_LNX_EPS) * lg_ref[...] + lb_ref[...]
    o_ref[...] = ((yn + e_ref[:, 0:BW]) * e_ref[:, BW:2 * BW]).astype(o_ref.dtype)


def _rwpost(yf, yr, e, lnx_g, lnx_b, *, tm=512):
    m = yf.shape[0]
    row = lambda z: z.astype(F32).reshape(1, -1)
    ysp = pl.BlockSpec((tm, BW), lambda i: (i, 0))
    return pl.pallas_call(
        _rwpost_kernel,
        grid=(m // tm,),
        in_specs=[ysp, ysp, pl.BlockSpec((tm, 2 * BW), lambda i: (i, 0)),
                  pl.BlockSpec((1, BW), lambda i: (0, 0)), pl.BlockSpec((1, BW), lambda i: (0, 0))],
        out_specs=ysp,
        out_shape=jax.ShapeDtypeStruct((m, BW), BF16),
        compiler_params=_cparams(("parallel",)),
        name="rwkv_post",
    )(yf, yr, e, row(lnx_g), row(lnx_b))


def _df_kernel(lam_ref, sg_ref, q_ref, k_ref, v_ref, g_ref, o_ref, *, lam_init):
    lam = lam_ref[...]
    e1 = jnp.exp(jnp.sum(lam[0:1] * lam[1:2], -1, keepdims=True))
    e2 = jnp.exp(jnp.sum(lam[2:3] * lam[3:4], -1, keepdims=True))
    lam_full = e1 - e2 + lam_init
    d = DF_HEAD_DIM
    tq = q_ref.shape[0]
    qs = _stack_heads(q_ref[...], 2 * DF_HEADS, d)
    k, v = k_ref[...], v_ref[...]
    ss = [_dot_nt(qs[2 * h * tq:2 * (h + 1) * tq], k) for h in range(DF_HEADS)]
    ws, r1s = [], []
    for s in ss:
        e = jnp.exp2(s - jnp.max(s, -1, keepdims=True))
        l = jnp.sum(e, -1, keepdims=True)
        r1 = 1.0 / l[0:tq]
        ws.append((e[0:tq] - e[tq:2 * tq] * (lam_full * l[0:tq] / l[tq:2 * tq])).astype(BF16))
        r1s.append(r1)
    os_ = [jnp.dot(w, v, preferred_element_type=F32) * r1 for w, r1 in zip(ws, r1s)]
    lane = lax.broadcasted_iota(jnp.int32, (1, BW), 1)
    acc = None
    for h, o in enumerate(os_):
        in_head = (lane >= 2 * h * d) & (lane < 2 * (h + 1) * d)
        acc = o if acc is None else jnp.where(in_head, o, acc)
    ms = _seg_sum(acc * acc, 2 * d) * (1.0 / (2 * d))
    o = acc * lax.rsqrt(ms + DF_EPS) * sg_ref[...] * (1.0 - lam_init)
    o_ref[...] = (o * jax.nn.silu(g_ref[...])).astype(o_ref.dtype)


def _df(cqk, hb, fb, lam, subln_g, lam_init, *, batch, seq, tq=256):
    sg = jnp.tile(subln_g.astype(F32), DF_HEADS).reshape(1, BW)
    return pl.pallas_call(
        functools.partial(_df_kernel, lam_init=lam_init),
        grid=(batch, seq // tq),
        in_specs=[
            pl.BlockSpec((4, DF_HEAD_DIM), lambda b, j: (0, 0)),
            pl.BlockSpec((1, BW), lambda b, j: (0, 0)),
            pl.BlockSpec((None, tq, BW), lambda b, j: (b, j, 0)),
            pl.BlockSpec((None, seq, BW), lambda b, j: (b, 0, 1)),
            pl.BlockSpec((None, seq, BW), lambda b, j: (b, 0, 3)),
            pl.BlockSpec((None, tq, BW), lambda b, j: (b, j, 6)),
        ],
        out_specs=pl.BlockSpec((None, tq, BW), lambda b, j: (b, j, 0)),
        out_shape=jax.ShapeDtypeStruct((batch, seq, BW), BF16),
        compiler_params=_cparams(("parallel", "arbitrary")),
        name="diff_attn",
    )(lam.astype(F32), sg, cqk, cqk, hb, fb)


DL_QB = 2 * DL_HALF
DL_UNROLL = 4


def _dl_group(qkv_ref, kpad, vpad, oacc, lacc, gidx, dil, seq):
    L = seq // dil
    assert L % DL_QB == 0 and (seq // DL_QB) % DL_UNROLL == 0
    nblk = L // DL_QB
    seg = L + 2 * DL_HALF
    zpad = jnp.zeros((DL_HALF, BW), BF16)

    def stage(rho, carry):
        src = pl.ds(pl.multiple_of(rho * L, DL_QB), L)
        krow = pl.multiple_of(rho * seg, DL_HALF)
        for ref in (kpad, vpad):
            ref[pl.ds(krow, DL_HALF), :] = zpad
            ref[pl.ds(krow + DL_HALF + L, DL_HALF), :] = zpad
        for c in range(BW // LANES):
            cols = slice(c * LANES, (c + 1) * LANES)
            kpad[pl.ds(krow + DL_HALF, L), cols] = qkv_ref[2 + c, src, :]
            vpad[pl.ds(krow + DL_HALF, L), cols] = qkv_ref[4 + c, src, :]
        return carry

    lax.fori_loop(0, dil, stage, 0)

    hq = DL_HEADS * DL_QB
    qi = lax.broadcasted_iota(jnp.int32, (hq, 2 * DL_QB), 0) % DL_QB
    ji = lax.broadcasted_iota(jnp.int32, (hq, 2 * DL_QB), 1)
    band = jnp.where(jnp.abs(ji - DL_HALF - qi) <= DL_HALF, 0.0, MASKED)
    jrow = lax.broadcasted_iota(jnp.int32, (1, 2 * DL_QB), 1)
    lane = lax.broadcasted_iota(jnp.int32, (1, BW), 1)

    def blocks(it, carry):
        fs = [it * DL_UNROLL + u for u in range(DL_UNROLL)]
        rhos = [f // nblk for f in fs]
        ns = [f % nblk for f in fs]
        qrows = [pl.ds(pl.multiple_of(f * DL_QB, DL_QB), DL_QB) for f in fs]
        qst = [_stack_heads(jnp.concatenate([qkv_ref[c, r, :] for c in range(BW // LANES)], axis=1),
                            DL_HEADS, DL_HEAD_DIM) for r in qrows]
        krows = [pl.multiple_of(rho * seg + n * DL_QB, DL_HALF) for rho, n in zip(rhos, ns)]
        ss = [_dot_nt(q, kpad[pl.ds(kr, 2 * DL_QB), :]) for q, kr in zip(qst, krows)]
        ps, lses, dens = [], [], []
        for s, n in zip(ss, ns):
            kpos = n * DL_QB - DL_HALF + jrow
            s = s + band + jnp.where((kpos >= 0) & (kpos < L), 0.0, MASKED)
            m = jnp.max(s, -1, keepdims=True)
            p = jnp.exp2(s - m)
            den = jnp.sum(p, -1, keepdims=True)
            ps.append(p.astype(BF16))
            dens.append(den)
            lses.append(m + jnp.log2(den))
        os_ = [jnp.dot(p, vpad[pl.ds(kr, 2 * DL_QB), :], preferred_element_type=F32) / den
               for p, kr, den in zip(ps, krows, dens)]
        for o, lse, rho, n in zip(os_, lses, rhos, ns):
            acc_o = _unstack_heads(o, DL_HEADS, DL_HEAD_DIM)
            acc_l = _unstack_heads(jnp.broadcast_to(lse, (hq, BW)), DL_HEADS, DL_HEAD_DIM)
            if dil > 1:
                dst = pl.ds(rho + dil * n * DL_QB, DL_QB, stride=dil)
            else:
                dst = pl.ds(pl.multiple_of(n * DL_QB, DL_QB), DL_QB)
            for c in range(BW // LANES):
                oacc[gidx, c, dst, :] = acc_o[:, c * LANES:(c + 1) * LANES]
                lacc[gidx, c, dst, :] = acc_l[:, c * LANES:(c + 1) * LANES]
        return carry

    lax.fori_loop(0, seq // DL_QB // DL_UNROLL, blocks, 0)


def _dl_kernel(*refs, seq):
    ng = len(DL_DILATIONS)
    qkv_refs, (g_ref, o_ref, kpad, vpad, oacc, lacc) = refs[:ng], refs[ng:]
    gi = pl.program_id(1)
    for gidx, dil in enumerate(DL_DILATIONS):
        @pl.when(gi == gidx)
        def _(gidx=gidx, dil=dil):
            _dl_group(qkv_refs[gidx], kpad, vpad, oacc, lacc, gidx, dil, seq)

    @pl.when(gi == len(DL_DILATIONS) - 1)
    def _():
        rc = 256

        def chunk(c, carry):
            rows = pl.ds(pl.multiple_of(c * rc, rc), rc)
            for lc in range(BW // LANES):
                cols = slice(lc * LANES, (lc + 1) * LANES)
                ls = [lacc[g, lc, rows, :] for g in range(len(DL_DILATIONS))]
                mx = functools.reduce(jnp.maximum, ls)
                ws = [jnp.exp2(l - mx) for l in ls]
                num = sum(w * oacc[g, lc, rows, :] for g, w in enumerate(ws))
                out = num / sum(ws)
                o_ref[rows, cols] = (out * jax.nn.silu(g_ref[rows, cols])).astype(o_ref.dtype)
            return carry

        lax.fori_loop(0, seq // rc, chunk, 0)


def _dl(dqkvs, fb, *, batch, seq):
    ng = len(DL_DILATIONS)
    nch = 3 * BW // LANES
    assert all(w // (2 * d) == DL_HALF for w, d in zip(DL_WINDOWS, DL_DILATIONS))
    return pl.pallas_call(
        functools.partial(_dl_kernel, seq=seq),
        grid=(batch, ng),
        in_specs=[pl.BlockSpec((nch, None, seq, LANES), lambda b, g: (0, b, 0, 0))] * ng
        + [pl.BlockSpec((None, seq, BW), lambda b, g: (b, 0, 7))],
        out_specs=pl.BlockSpec((None, seq, BW), lambda b, g: (b, 0, 0)),
        out_shape=jax.ShapeDtypeStruct((batch, seq, BW), BF16),
        scratch_shapes=[
            pltpu.VMEM((seq + 2 * DL_HALF * max(DL_DILATIONS), BW), BF16),
            pltpu.VMEM((seq + 2 * DL_HALF * max(DL_DILATIONS), BW), BF16),
            pltpu.VMEM((ng, BW // LANES, seq, LANES), F32),
            pltpu.VMEM((ng, BW // LANES, seq, LANES), F32),
        ],
        compiler_params=_cparams(("parallel", "arbitrary")),
        name="dilated_attn",
    )(*dqkvs, fb)


_A0 = 0
_B0 = _A0 + 4 * BW
_C0 = _B0 + 4 * BW + 4 * RW_LORA
_D0 = _C0 + 4 * BW
_DG0 = _D0 + 9 * BW
_G0 = _DG0 + BW
PROJ_TM = 2048


def _layer(xf, xb, l, p, tabs, *, batch, seq, last_split):
    w, b = p["w_in"][l], p["b_in"][l]
    cols = lambda lo, n: (w[:, lo:lo + n], b[lo:lo + n])
    cat = lambda parts: (jnp.concatenate([q[0] for q in parts], 1).astype(BF16),
                         jnp.concatenate([q[1] for q in parts], 0).astype(F32))
    scaled = lambda part, s: (part[0] * s, part[1] * s)
    q_scale = NA_HEAD_DIM ** -0.5 * LOG2E
    proj = functools.partial(_proj, xb, seq=seq, tm=PROJ_TM)

    gates = proj(*cat([scaled(cols(_G0, 4 * D_MODEL), 0.5)]), BF16, tn=1024, sigmoid_2x=True,
                 name="proj_gates")
    fb = proj(*cat([cols(_B0, _C0 - _B0), cols(_A0 + 3 * BW, BW), cols(_C0 + 3 * BW, BW),
                    cols(_DG0, BW)]), F32, tn=1024, name="proj_f32")
    hb = proj(*cat([scaled(cols(_A0, BW), q_scale), cols(_A0 + BW, 2 * BW), cols(_C0 + 2 * BW, BW)]),
              BF16, tn=1024, name="proj_bf16")
    cpb = BW // LANES
    cqk = proj(*cat([cols(_C0, 2 * BW)]), BF16, tn=2 * BW, name="proj_rot_c",
               rot=(DF_HEAD_DIM,) + tabs["c"] + ((0,) * cpb + (1,) * cpb,))
    dqs = [proj(*cat([scaled(cols(_D0 + 3 * g * BW, BW), q_scale),
                      cols(_D0 + (3 * g + 1) * BW, 2 * BW)]),
                BF16, tn=3 * BW, residue_dil=dil, name="proj_rot_d",
                rot=(DL_HEAD_DIM,) + tabs["d"] + ((0,) * (2 * cpb) + (None,) * cpb,))
           for g, dil in enumerate(DL_DILATIONS)]

    r3 = lambda z: z.reshape(batch, seq, z.shape[-1])
    hb3, fb3 = r3(hb), r3(fb)
    ya = _na(hb3, fb3, _na_bias_tables(p["na_rpb"][l], seq // GRID_W), batch=batch, seq=seq)
    f, e = _rwprep(fb3, p["rw_mu"][l], p["rw_w0"][l], p["rw_w2"][l], p["rw_a0"][l], p["rw_a2"][l],
                   p["rw_kk"][l], p["rw_ka"][l], p["rw_rk"][l].reshape(-1), batch=batch, seq=seq)
    yf, yr = _rwscan(f, batch=batch, seq=seq)
    m = batch * seq
    yb = _rwpost(yf.reshape(m, BW), yr.reshape(m, BW), e.reshape(m, 2 * BW), p["rw_lnx_g"][l],
                 p["rw_lnx_b"][l])
    lam_init = 0.8 - 0.6 * math.exp(-0.3 * l)
    yc = _df(r3(cqk), hb3, fb3, p["df_lam"][l], p["df_subln_g"][l], lam_init, batch=batch, seq=seq)
    yd = _dl([dq.reshape(dq.shape[0], batch, seq, LANES) for dq in dqs], fb3, batch=batch, seq=seq)

    margs = (ya.reshape(m, BW), yb, yc.reshape(m, BW), yd.reshape(m, BW), gates, xf,
             p["w_branch"][l].astype(BF16), p["w_out"][l].astype(BF16), p["b_out"][l],
             p["ln_g"][l], p["ln_b"][l])
    if last_split is None:
        return _merge(*margs, row0=0, rows=m, with_bf16=True)
    return [_merge(*margs, row0=r0, rows=n, with_bf16=False)[0] for r0, n in last_split]


def kernel(x_prompt, x_sample, ln0_g, ln0_b, w_in, b_in, na_rpb, rw_mu, rw_w0, rw_w2, rw_a0, rw_a2,
           rw_kk, rw_ka, rw_rk, rw_lnx_g, rw_lnx_b, df_lam, df_subln_g, w_branch, w_out, b_out,
           ln_g, ln_b):
    p = dict(w_in=w_in, b_in=b_in, na_rpb=na_rpb, rw_mu=rw_mu, rw_w0=rw_w0, rw_w2=rw_w2,
             rw_a0=rw_a0, rw_a2=rw_a2, rw_kk=rw_kk, rw_ka=rw_ka, rw_rk=rw_rk, rw_lnx_g=rw_lnx_g,
             rw_lnx_b=rw_lnx_b, df_lam=df_lam, df_subln_g=df_subln_g, w_branch=w_branch,
             w_out=w_out, b_out=b_out, ln_g=ln_g, ln_b=ln_b)
    bp, seq, _ = x_prompt.shape
    bs = x_sample.shape[0]
    assert x_sample.shape[1] == seq
    batch = bp + bs
    xf, xb = _ln0(x_prompt.reshape(bp * seq, D_MODEL), x_sample.reshape(bs * seq, D_MODEL),
                  ln0_g, ln0_b)
    tabs = {"c": _rope_tables(seq, DF_HEAD_DIM, (DF_HEAD_DIM ** -0.5 * LOG2E, 1.0)),
            "d": _rope_tables(seq, DL_HEAD_DIM, (1.0,))}
    for l in range(DEPTH - 1):
        xf, xb = _layer(xf, xb, l, p, tabs, batch=batch, seq=seq, last_split=None)
    split = [(0, bp * seq), (bp * seq, bs * seq)]
    yp, ys = _layer(xf, xb, DEPTH - 1, p, tabs, batch=batch, seq=seq, last_split=split)
    return yp.reshape(bp, seq, D_MODEL), ys.reshape(bs, seq, D_MODEL)
```

```python
import functools
import math

import numpy as np
import jax
import jax.numpy as jnp
from jax import lax
from jax.experimental import pallas as pl
from jax.experimental.pallas import tpu as pltpu

F32 = jnp.float32
BF16 = jnp.bfloat16

D_MODEL = 1024
DEPTH = 2
GRID_W = 64
NA_HEADS, NA_HEAD_DIM, NA_ROWS, NA_COLS = 4, 64, 8, 16
RW_HEADS, RW_HEAD_DIM, RW_LORA = 4, 64, 64
RW_LNX_EPS = 64e-5
DF_HEADS, DF_HEAD_DIM, DF_EPS = 4, 32, 1e-5
DL_HEADS, DL_HEAD_DIM = 4, 64
DL_DILATIONS = (1, 4, 16)
DL_WINDOWS = (128, 512, 2048)
DL_HALF = 64
BW = 256
ROPE_THETA = 10000.0
LN_EPS = 1e-5
DEEPNORM_ALPHA = (2 * DEPTH) ** 0.25
LOG2E = math.log2(math.e)
MASKED = -float("inf")

LANES = 128
VMEM_LIMIT = 56 * 1024 * 1024


def _cparams(sem):
    return pltpu.CompilerParams(dimension_semantics=sem, vmem_limit_bytes=VMEM_LIMIT)


def _sigmoid(x):
    return 0.5 * jnp.tanh(0.5 * x) + 0.5


def _silu(x):
    return x * _sigmoid(x)


def _dot_nt(a, b):
    return lax.dot_general(a, b, (((1,), (1,)), ((), ())), preferred_element_type=F32)


def _split2(x):
    hi = x.astype(BF16)
    return hi, (x - hi.astype(F32)).astype(BF16)


def _seg_sum(x, seg):
    n = x.shape[-1]
    r = lax.broadcasted_iota(jnp.int32, (n, n), 0) // seg
    c = lax.broadcasted_iota(jnp.int32, (n, n), 1) // seg
    ones = (r == c).astype(BF16)
    out = None
    for t in _split2(x):
        y = jnp.dot(t, ones, preferred_element_type=F32)
        out = y if out is None else out + y
    return out


def _ln0_kernel(xp_ref, xs_ref, g_ref, b_ref, of_ref, ob_ref, *, n_prompt):
    def norm(x):
        mu = jnp.mean(x, -1, keepdims=True)
        xc = x - mu
        var = jnp.mean(xc * xc, -1, keepdims=True)
        y = xc * lax.rsqrt(var + LN_EPS) * g_ref[...] + b_ref[...]
        of_ref[...] = y
        ob_ref[...] = y.astype(BF16)

    i = pl.program_id(0)

    @pl.when(i < n_prompt)
    def _():
        norm(xp_ref[...])

    @pl.when(i >= n_prompt)
    def _():
        norm(xs_ref[...])


def _ln0(xp, xs, g, b, tm=512):
    mp, ms = xp.shape[0], xs.shape[0]
    n_p, n_s = mp // tm, ms // tm
    out = jax.ShapeDtypeStruct((mp + ms, D_MODEL), F32)
    outb = jax.ShapeDtypeStruct((mp + ms, D_MODEL), BF16)
    return pl.pallas_call(
        functools.partial(_ln0_kernel, n_prompt=n_p),
        grid=(n_p + n_s,),
        in_specs=[
            pl.BlockSpec((tm, D_MODEL), lambda i: (jnp.minimum(i, n_p - 1), 0)),
            pl.BlockSpec((tm, D_MODEL), lambda i: (jnp.maximum(i - n_p, 0), 0)),
            pl.BlockSpec((1, D_MODEL), lambda i: (0, 0)),
            pl.BlockSpec((1, D_MODEL), lambda i: (0, 0)),
        ],
        out_specs=[pl.BlockSpec((tm, D_MODEL), lambda i: (i, 0)),
                   pl.BlockSpec((tm, D_MODEL), lambda i: (i, 0))],
        out_shape=[out, outb],
        compiler_params=_cparams(("parallel",)),
        name="ln0",
    )(xp, xs, g.reshape(1, -1), b.reshape(1, -1))


def _proj_kernel(x_ref, w_ref, b_ref, *rest, rot_hd, kinds, residue_dil, sigmoid_2x):
    acc = jnp.dot(x_ref[...], w_ref[...], preferred_element_type=F32) + b_ref[...]
    if not rot_hd:
        (o_ref,) = rest
        o_ref[...] = (0.5 * jnp.tanh(acc) + 0.5 if sigmoid_2x else acc).astype(o_ref.dtype)
        return
    cs_ref, sn_ref, o_ref = rest[:3]
    h2 = rot_hd // 2
    lane = lax.broadcasted_iota(jnp.int32, (1, LANES), 1)
    first = (lane % rot_hd) < h2
    for c, kind in enumerate(kinds):
        val = acc[:, c * LANES:(c + 1) * LANES]
        if kind is not None:
            sw = jnp.where(first, pltpu.roll(val, LANES - h2, 1), pltpu.roll(val, h2, 1))
            val = val * cs_ref[kind] + sw * sn_ref[kind]
        if not residue_dil:
            o_ref[:, c * LANES:(c + 1) * LANES] = val.astype(o_ref.dtype)
        elif residue_dil == 1:
            o_ref[c] = val.astype(o_ref.dtype)
        else:
            tmp_ref = rest[3]
            tmp_ref[...] = val
            L = val.shape[0] // residue_dil
            for rho in range(residue_dil):
                o_ref[c, rho * L:(rho + 1) * L, :] = (
                    tmp_ref[pl.ds(rho, L, stride=residue_dil), :].astype(o_ref.dtype))


def _proj(xb, w, b, out_dtype, *, seq, tm, tn, rot=None, residue_dil=0, sigmoid_2x=False,
          name="proj"):
    m, n = xb.shape[0], w.shape[1]
    assert m % tm == 0 and n % tn == 0 and seq % tm == 0
    in_specs = [
        pl.BlockSpec((tm, D_MODEL), lambda i, j: (i, 0)),
        pl.BlockSpec((D_MODEL, tn), lambda i, j: (0, j)),
        pl.BlockSpec((1, tn), lambda i, j: (0, j)),
    ]
    args = [xb, w, b.reshape(1, -1)]
    rot_hd, kinds, scratch = 0, None, []
    if rot is not None:
        rot_hd, cos, sin, kinds = rot
        assert len(kinds) == tn // LANES and LANES % rot_hd == 0
        spt = seq // tm
        tab = pl.BlockSpec((cos.shape[0], tm, LANES), lambda i, j: (0, i % spt, 0))
        in_specs += [tab, tab]
        args += [cos, sin]
    if residue_dil:
        assert rot is not None and tm == seq and seq % residue_dil == 0
        out_spec = pl.BlockSpec((tn // LANES, tm, LANES), lambda i, j: (j, i, 0))
        out_shape = jax.ShapeDtypeStruct((n // LANES, m, LANES), out_dtype)
        if residue_dil > 1:
            scratch = [pltpu.VMEM((tm, LANES), F32)]
    else:
        out_spec = pl.BlockSpec((tm, tn), lambda i, j: (i, j))
        out_shape = jax.ShapeDtypeStruct((m, n), out_dtype)
    return pl.pallas_call(
        functools.partial(_proj_kernel, rot_hd=rot_hd, kinds=kinds, residue_dil=residue_dil,
                          sigmoid_2x=sigmoid_2x),
        grid=(m // tm, n // tn),
        in_specs=in_specs,
        out_specs=out_spec,
        out_shape=out_shape,
        scratch_shapes=scratch,
        compiler_params=_cparams(("parallel", "arbitrary")),
        name=name,
    )(*args)


def _rope_tables(seq, head_dim, scales):
    half = head_dim // 2
    inv_freq = jnp.power(ROPE_THETA, -jnp.arange(half, dtype=F32) / half)
    ang = jnp.arange(seq, dtype=F32)[:, None] * inv_freq[None, :]
    lane = np.arange(LANES)
    f_idx = lane % half
    sign = np.where((lane % head_dim) < half, -1.0, 1.0).astype(np.float32)
    cos = jnp.cos(ang)[:, f_idx]
    sin = jnp.sin(ang)[:, f_idx] * sign[None, :]
    return (jnp.stack([cos * F32(s) for s in scales]), jnp.stack([sin * F32(s) for s in scales]))


def _merge_kernel(ya_ref, yb_ref, yc_ref, yd_ref, g_ref, x_ref, wb_ref, wo_ref, bo_ref, lg_ref,
                  lb_ref, of_ref, *maybe_ob_ref):
    merged = None
    for i, y_ref in enumerate((ya_ref, yb_ref, yc_ref, yd_ref)):
        p = jnp.dot(y_ref[...], wb_ref[i], preferred_element_type=F32)
        gate = g_ref[:, i * D_MODEL:(i + 1) * D_MODEL].astype(F32)
        merged = gate * p if merged is None else merged + gate * p
    y = jnp.dot(merged.astype(BF16), wo_ref[...], preferred_element_type=F32) + bo_ref[...]
    z = DEEPNORM_ALPHA * x_ref[...] + y
    mu = jnp.mean(z, -1, keepdims=True)
    zc = z - mu
    var = jnp.mean(zc * zc, -1, keepdims=True)
    out = zc * lax.rsqrt(var + LN_EPS) * lg_ref[...] + lb_ref[...]
    of_ref[...] = out
    for ob_ref in maybe_ob_ref:
        ob_ref[...] = out.astype(BF16)


def _merge(ya, yb, yc, yd, gates, x, wb, wo, bo, lg, lb, *, row0, rows, with_bf16, tm=512):
    outs = [F32, BF16] if with_bf16 else [F32]
    assert row0 % tm == 0 and rows % tm == 0
    o = row0 // tm
    ysp = pl.BlockSpec((tm, BW), lambda i: (i + o, 0))
    const2 = lambda i: (0, 0)
    return pl.pallas_call(
        _merge_kernel,
        grid=(rows // tm,),
        in_specs=[ysp, ysp, ysp, ysp,
                  pl.BlockSpec((tm, 4 * D_MODEL), lambda i: (i + o, 0)),
                  pl.BlockSpec((tm, D_MODEL), lambda i: (i + o, 0)),
                  pl.BlockSpec((4, BW, D_MODEL), lambda i: (0, 0, 0)),
                  pl.BlockSpec((D_MODEL, D_MODEL), const2),
                  pl.BlockSpec((1, D_MODEL), const2),
                  pl.BlockSpec((1, D_MODEL), const2),
                  pl.BlockSpec((1, D_MODEL), const2)],
        out_specs=[pl.BlockSpec((tm, D_MODEL), lambda i: (i, 0)) for _ in outs],
        out_shape=[jax.ShapeDtypeStruct((rows, D_MODEL), dt) for dt in outs],
        compiler_params=_cparams(("parallel",)),
        name="merge",
    )(ya, yb, yc, yd, gates, x, wb, wo, bo.reshape(1, -1), lg.reshape(1, -1), lb.reshape(1, -1))


NA_QR = 4


def _na_geometry(rows):
    kr, wr = min(NA_ROWS, rows), min(NA_ROWS, rows) + NA_QR - 1
    assert rows % NA_QR == 0 and rows >= wr and kr == NA_ROWS
    steps = rows // NA_QR
    a = np.arange(NA_QR)[:, None]
    j = np.arange(wr)[None, :]
    pats = []
    for g in range(steps):
        r = NA_QR * g + a
        rs = np.clip(r - kr // 2, 0, rows - kr)
        ws = int(np.clip(NA_QR * g - kr // 2, 0, rows - wr))
        assert rs.min() >= ws and rs.max() + kr <= ws + wr
        valid = (ws + j >= rs) & (ws + j < rs + kr)
        dr = np.clip(ws + j - r + NA_ROWS - 1, 0, 2 * NA_ROWS - 2)
        pats.append((valid, np.where(valid, dr, 0)))
    same = lambda x, y: np.array_equal(x[0], y[0]) and np.array_equal(x[1], y[1])
    assert steps >= 3 and all(same(pats[g], pats[1]) for g in range(1, steps - 1))
    return kr, wr, steps, [pats[0], pats[1], pats[steps - 1]]


def _na_bias_tables(rpb, rows):
    kr, wr, steps, pats = _na_geometry(rows)
    qc = np.arange(GRID_W)[:, None]
    kc = np.arange(GRID_W)[None, :]
    c_start = np.clip(qc - NA_COLS // 2, 0, GRID_W - NA_COLS)
    col_ok = (kc >= c_start) & (kc < c_start + NA_COLS)
    dc = np.clip(kc - qc + NA_COLS - 1, 0, 2 * NA_COLS - 2)
    tabs = []
    by_col = rpb.astype(F32)[:, :, dc]
    for valid, dr in pats:
        b = by_col[:, dr] * LOG2E
        ok = valid[:, :, None, None] & col_ok[None, None]
        b = jnp.where(ok[None], b, MASKED)
        tabs.append(b.transpose(0, 1, 3, 2, 4).reshape(NA_HEADS, NA_QR * GRID_W, wr * GRID_W))
    return jnp.stack(tabs, axis=1)


def _stack_heads(q, n_heads, head_dim):
    lane = lax.broadcasted_iota(jnp.int32, (1, q.shape[1]), 1)
    zero = jnp.zeros_like(q)
    return jnp.concatenate(
        [jnp.where((lane >= h * head_dim) & (lane < (h + 1) * head_dim), q, zero)
         for h in range(n_heads)], axis=0)


def _unstack_heads(o, n_heads, head_dim):
    m = o.shape[0] // n_heads
    lane = lax.broadcasted_iota(jnp.int32, (1, o.shape[1]), 1)
    acc = o[0:m]
    for h in range(1, n_heads):
        in_head = (lane >= h * head_dim) & (lane < (h + 1) * head_dim)
        acc = jnp.where(in_head, o[h * m:(h + 1) * m], acc)
    return acc


def _na_kernel(q_ref, k_ref, v_ref, g_ref, bias_ref, o_ref, *, rows):
    kr, wr = min(NA_ROWS, rows), min(NA_ROWS, rows) + NA_QR - 1
    g = pl.program_id(1)
    ws = jnp.clip(NA_QR * g - kr // 2, 0, rows - wr)
    start = pl.multiple_of(ws * GRID_W, GRID_W)
    k = k_ref[pl.ds(start, wr * GRID_W), :]
    v = v_ref[pl.ds(start, wr * GRID_W), :]
    qs = _stack_heads(q_ref[...], NA_HEADS, NA_HEAD_DIM)
    m_rows = q_ref.shape[0]
    halves = [slice(0, 2 * m_rows), slice(2 * m_rows, 4 * m_rows)]
    ss = [_dot_nt(qs[h], k) for h in halves]
    ps, ls = [], []
    for i, s in enumerate(ss):
        s = s + bias_ref[2 * i:2 * i + 2].reshape(s.shape)
        p = jnp.exp2(s - jnp.max(s, -1, keepdims=True))
        ls.append(jnp.sum(p, -1, keepdims=True))
        ps.append(p.astype(BF16))
    o = jnp.concatenate([jnp.dot(p, v, preferred_element_type=F32) / l for p, l in zip(ps, ls)], 0)
    o = _unstack_heads(o, NA_HEADS, NA_HEAD_DIM)
    o_ref[...] = (o * _silu(g_ref[...])).astype(o_ref.dtype)


def _na(hb, fb, bias_tab, *, batch, seq):
    rows = seq // GRID_W
    kr, wr, steps, _ = _na_geometry(rows)
    qb = NA_QR * GRID_W

    def pattern_of(g):
        return jnp.where(g == 0, 0, jnp.where(g == steps - 1, 2, 1))

    return pl.pallas_call(
        functools.partial(_na_kernel, rows=rows),
        grid=(batch, steps),
        in_specs=[
            pl.BlockSpec((None, qb, BW), lambda b, g: (b, g, 0)),
            pl.BlockSpec((None, seq, BW), lambda b, g: (b, 0, 1)),
            pl.BlockSpec((None, seq, BW), lambda b, g: (b, 0, 2)),
            pl.BlockSpec((None, qb, BW), lambda b, g: (b, g, 5)),
            pl.BlockSpec((NA_HEADS, None, qb, wr * GRID_W), lambda b, g: (0, pattern_of(g), 0, 0)),
        ],
        out_specs=pl.BlockSpec((None, qb, BW), lambda b, g: (b, g, 0)),
        out_shape=jax.ShapeDtypeStruct((batch, seq, BW), BF16),
        compiler_params=_cparams(("parallel", "arbitrary")),
        name="na_attn",
    )(hb, hb, hb, fb, bias_tab)


RW_C = 64
RW_FIELDS = 9
RW_INV_BASE = 4


def _mm(a, b, mode="nn", passes=1):
    dims = {"nn": (((1,), (0,)), ((), ())), "nt": (((1,), (1,)), ((), ()))}[mode]
    dg = lambda x, y: lax.dot_general(x, y, dims, preferred_element_type=F32)
    ah, bh = a.astype(BF16), b.astype(BF16)
    out = dg(ah, bh)
    if passes == 3:
        al = (a - ah.astype(F32)).astype(BF16)
        bl = (b - bh.astype(F32)).astype(BF16)
        out = out + (dg(al, bh) + dg(ah, bl))
    return out


def _mm_bd(a, b, mode="nn"):
    halves = [slice(i * LANES, (i + 1) * LANES) for i in range(BW // LANES)]
    return jnp.concatenate([_mm(a[:, h], b[h, h], mode) for h in halves], axis=1)


def _rwprep_kernel(x_ref, prev_ref, next_ref, mu_ref, w0_ref, w2_ref, a0_ref, a2_ref, kk_ref, ka_ref,
                   rk_ref, f_ref, e_ref, *, rc):
    c = pl.program_id(1)
    lane = lax.broadcasted_iota(jnp.int32, (1, 2 * RW_LORA), 1)
    ri = lax.broadcasted_iota(jnp.int32, (rc, 1), 0)
    x = x_ref[...]
    prev_row = jnp.where(c > 0, prev_ref[7:8, :], 0.0)
    next_row = jnp.where(c < pl.num_programs(1) - 1, next_ref[0:1, :], 0.0)
    up = jnp.where(ri == 0, prev_row, pltpu.roll(x, 1, 0))
    dn = jnp.where(ri == rc - 1, next_row, pltpu.roll(x, rc - 1, 0))
    mu = mu_ref[...]
    xs = x * (1.0 - mu) + (0.5 * mu) * (up + dn)
    r, k, v, g = (xs[:, i * BW:(i + 1) * BW] for i in range(4))
    wl = xs[:, 4 * BW:4 * BW + 2 * RW_LORA]
    al = xs[:, 4 * BW + 2 * RW_LORA:4 * BW + 4 * RW_LORA]
    kk = k * kk_ref[...]
    kap = kk * lax.rsqrt(jnp.maximum(_seg_sum(kk * kk, RW_HEAD_DIM), 1e-24))
    tw = jnp.tanh(wl)
    k_scaled = k * ka_ref[...]
    k_fixed = k * (1.0 - ka_ref[...])
    kd_sum = None
    for d in range(2):
        dm = ((lane >= d * RW_LORA) & (lane < (d + 1) * RW_LORA)).astype(F32)
        w_raw = w0_ref[d:d + 1, :] + _mm(tw * dm, w2_ref[...], passes=3)
        lw = _sigmoid(w_raw) * (-math.exp(-0.5))
        a = _sigmoid(a0_ref[d:d + 1, :] + _mm(al * dm, a2_ref[...], passes=3))
        kd = k_fixed + k_scaled * a
        f_ref[:, (3 + 3 * d) * BW:(4 + 3 * d) * BW] = lw
        f_ref[:, (4 + 3 * d) * BW:(5 + 3 * d) * BW] = kd
        f_ref[:, (5 + 3 * d) * BW:(6 + 3 * d) * BW] = kap * a
        kd_sum = kd if kd_sum is None else kd_sum + kd
    f_ref[:, 0:BW] = r
    f_ref[:, BW:2 * BW] = v
    f_ref[:, 2 * BW:3 * BW] = kap
    bonus = _seg_sum(r * kd_sum * rk_ref[...], RW_HEAD_DIM)
    e_ref[:, 0:BW] = bonus * v
    e_ref[:, BW:2 * BW] = _silu(g)


def _rwprep(fb, mu, w0, w2, a0, a2, k_k, k_a, r_k, *, batch, seq, rc=256):
    ucols = 4 * BW + 4 * RW_LORA
    row = lambda z: z.astype(F32).reshape(1, -1)
    c2 = lambda b, c: (0, 0)
    tpb = rc // 8
    return pl.pallas_call(
        functools.partial(_rwprep_kernel, rc=rc),
        grid=(batch, seq // rc),
        in_specs=[
            pl.BlockSpec((None, rc, ucols), lambda b, c: (b, c, 0)),
            pl.BlockSpec((None, 8, ucols), lambda b, c: (b, jnp.maximum(c * tpb - 1, 0), 0)),
            pl.BlockSpec((None, 8, ucols),
                         lambda b, c: (b, jnp.minimum((c + 1) * tpb, seq // 8 - 1), 0)),
            pl.BlockSpec((1, ucols), c2),
            pl.BlockSpec((2, BW), c2),
            pl.BlockSpec((2 * RW_LORA, BW), c2),
            pl.BlockSpec((2, BW), c2),
            pl.BlockSpec((2 * RW_LORA, BW), c2),
            pl.BlockSpec((1, BW), c2),
            pl.BlockSpec((1, BW), c2),
            pl.BlockSpec((1, BW), c2),
        ],
        out_specs=[pl.BlockSpec((None, rc, RW_FIELDS * BW), lambda b, c: (b, c, 0)),
                   pl.BlockSpec((None, rc, 2 * BW), lambda b, c: (b, c, 0))],
        out_shape=[jax.ShapeDtypeStruct((batch, seq, RW_FIELDS * BW), F32),
                   jax.ShapeDtypeStruct((batch, seq, 2 * BW), F32)],
        compiler_params=_cparams(("parallel", "arbitrary")),
        name="rwkv_prep",
    )(fb, fb, fb, row(mu), w0.astype(F32), w2.astype(F32).reshape(2 * RW_LORA, BW), a0.astype(F32),
      a2.astype(F32).reshape(2 * RW_LORA, BW), row(k_k), row(k_a), row(r_k))


def _rw_bd(x):
    lane = lax.broadcasted_iota(jnp.int32, (1, BW), 1)
    xb = x.astype(BF16)
    zero = jnp.zeros_like(xb)
    return jnp.concatenate(
        [jnp.where((lane >= h * RW_C) & (lane < (h + 1) * RW_C), xb, zero) for h in range(RW_HEADS)],
        axis=0)


def _rw_advance(terms, z):
    gm, hv, nt = terms
    both = _mm_bd(gm, z)
    return both[0:RW_C] + hv, _rw_bd(both[RW_C:2 * RW_C] + nt)


def _rw_prepare(specs):
    C, H = RW_C, RW_HEADS
    assert H * C == BW and RW_HEAD_DIM == C
    bd = _rw_bd
    cat = lambda a, b: jnp.concatenate([a, b], axis=0)
    ri = lax.broadcasted_iota(jnp.int32, (C, C), 0)
    ci = lax.broadcasted_iota(jnp.int32, (C, C), 1)
    tris = [(ci <= ri).astype(BF16), (ci >= ri).astype(BF16)]
    row = lax.broadcasted_iota(jnp.int32, (C, BW), 0)
    col = lax.broadcasted_iota(jnp.int32, (C, BW), 1) % C
    eye = col == row
    strict = [col < row, col > row]
    incl = [col <= row, col >= row]
    same_block = lambda s: (row // s) == (col // s)

    def lc_t(x):
        t = bd(x).astype(F32).T
        return t[0:C] + t[C:2 * C] + t[2 * C:3 * C] + t[3 * C:4 * C]

    def load(spec):
        f_ref, r0, d = spec
        fld = lambda i: f_ref[r0:r0 + C, i * BW:(i + 1) * BW]
        c = dict(d=d, r=fld(0), v=fld(1), kap=fld(2), lw=fld(3 + 3 * d), kd=fld(4 + 3 * d),
                 beta=fld(5 + 3 * d))
        cs = None
        for t in _split2(c["lw"]):
            y = jnp.dot(tris[d], t, preferred_element_type=F32)
            cs = y if cs is None else cs + y
        c["cs"] = cs
        return c

    def scale(c):
        cs, d = c["cs"], c["d"]
        tot = cs[0:1] if d == 1 else cs[C - 1:C]
        ginv = jnp.exp(-cs)
        gto = jnp.exp(tot - cs)
        kt = c["kap"] * jnp.exp(cs - c["lw"])
        rt = c["r"] * jnp.exp(cs)
        c.update(gc=jnp.exp(tot), rt=rt, ktrt=cat(kt, rt), kt_b=bd(kt), v_b=bd(c["v"]),
                 kg_b=bd(c["kd"] * ginv), bg_b=bd(c["beta"] * ginv),
                 kh=c["kd"] * gto, bh=c["beta"] * gto)
        return c

    def gram(c):
        d = c["d"]
        gk = _mm_bd(c["ktrt"], c["kg_b"], "nt")
        gb = _mm_bd(c["ktrt"], c["bg_b"], "nt")
        l_bk = jnp.where(strict[d], gb[0:C], 0.0)
        c.update(l_kk=jnp.where(strict[d], gk[0:C], 0.0), l_kr=jnp.where(incl[d], gk[C:2 * C], 0.0),
                 l_br=jnp.where(incl[d], gb[C:2 * C], 0.0), l_bk=l_bk,
                 nbase=-jnp.where(same_block(RW_INV_BASE), l_bk, 0.0))
        return c

    def base_square(c):
        c["nsq"] = _mm_bd(c["nbase"], bd(c["nbase"]))
        return c

    def base_inverse(c):
        w = jnp.where(eye, 1.0, 0.0) + c.pop("nbase")
        c["w"] = w + _mm_bd(w, bd(c.pop("nsq")))
        return c

    def merge_left(c, s):
        off = same_block(2 * s) & jnp.logical_not(same_block(s))
        c["t"] = _mm_bd(c["w"], bd(jnp.where(off, c["l_bk"], 0.0)))
        return c

    def merge_right(c):
        c["w"] = c["w"] - _mm_bd(c.pop("t"), bd(c["w"]))
        return c

    def solve(c):
        c["p1_b"] = bd(_mm_bd(c["w"], c["kt_b"]))
        c["p2"] = _mm_bd(c["w"], bd(c["l_kk"]))
        c["p2_b"] = bd(c["p2"])
        return c

    def outputs(c):
        both = _mm_bd(cat(c["l_br"], c["bh_t"]), c["p1_b"])
        c["gm"] = cat(c["rt"] - both[0:C], jnp.where(eye, c["gc"], 0.0) - both[C:2 * C])
        hm = c["l_kr"] - _mm_bd(c["l_br"], c["p2_b"])
        three = _mm_bd(jnp.concatenate([hm, c["p2"], c["kh_t"]], axis=0), c["v_b"])
        c["hv"], c["p2v_b"], c["khv"] = three[0:C], bd(three[C:2 * C]), three[2 * C:3 * C]
        return c

    def state(c):
        nt = c["khv"] - _mm_bd(c["bh_t"], c["p2v_b"])
        return c["gm"].astype(BF16), c["hv"], nt

    def transposes(c):
        c["kh_t"], c["bh_t"] = lc_t(c.pop("kh")), lc_t(c.pop("bh"))
        return c

    cs = [load(s) for s in specs]
    cs = [scale(c) for c in cs]
    cs = [gram(c) for c in cs]
    cs = [transposes(c) for c in cs]
    cs = [base_square(c) for c in cs]
    cs = [base_inverse(c) for c in cs]
    s = RW_INV_BASE
    while s < C:
        cs = [merge_left(c, s) for c in cs]
        cs = [merge_right(c) for c in cs]
        s *= 2
    cs = [solve(c) for c in cs]
    cs = [outputs(c) for c in cs]
    return [state(c) for c in cs]


def _rwscan_kernel(ff_ref, fr_ref, yf_ref, yr_ref, zf_ref, zr_ref, *, nch):
    @pl.when(pl.program_id(1) == 0)
    def _():
        zf_ref[...] = jnp.zeros_like(zf_ref)
        zr_ref[...] = jnp.zeros_like(zr_ref)

    C = RW_C
    pre = _rw_prepare([(ff_ref, j * C, 0) for j in range(nch)]
                      + [(fr_ref, j * C, 1) for j in range(nch)])
    pre_f, pre_r = pre[:nch], pre[nch:]
    zf, zr = zf_ref[...], zr_ref[...]
    for j in range(nch):
        yf_ref[j * C:(j + 1) * C, :], zf = _rw_advance(pre_f[j], zf)
        jr = nch - 1 - j
        yr_ref[jr * C:(jr + 1) * C, :], zr = _rw_advance(pre_r[jr], zr)
    zf_ref[...] = zf
    zr_ref[...] = zr


def _rwscan(f, *, batch, seq, nch=4):
    rb = nch * RW_C
    nc = seq // rb
    fsp = lambda imap: pl.BlockSpec((None, rb, RW_FIELDS * BW), imap)
    ysp = lambda imap: pl.BlockSpec((None, rb, BW), imap)
    fwd = lambda b, s: (b, s, 0)
    rev = lambda b, s: (b, nc - 1 - s, 0)
    y = jax.ShapeDtypeStruct((batch, seq, BW), F32)
    return pl.pallas_call(
        functools.partial(_rwscan_kernel, nch=nch),
        grid=(batch, nc),
        in_specs=[fsp(fwd), fsp(rev)],
        out_specs=[ysp(fwd), ysp(rev)],
        out_shape=[y, y],
        scratch_shapes=[pltpu.VMEM((BW, BW), BF16), pltpu.VMEM((BW, BW), BF16)],
        compiler_params=_cparams(("parallel", "arbitrary")),
        name="rwkv_scan",
    )(f, f)


def _rwpost_kernel(yf_ref, yr_ref, e_ref, lg_ref, lb_ref, o_ref):
    y = yf_ref[...] + yr_ref[...]
    inv = 1.0 / RW_HEAD_DIM
    yc = y - _seg_sum(y, RW_HEAD_DIM) * inv
    var = _seg_sum(yc * yc, RW_HEAD_DIM) * inv
    yn = yc * lax.rsqrt(var + RW_LNX_EPS) * lg_ref[...] + lb_ref[...]
    o_ref[...] = ((yn + e_ref[:, 0:BW]) * e_ref[:, BW:2 * BW]).astype(o_ref.dtype)


def _rwpost(yf, yr, e, lnx_g, lnx_b, *, tm=512):
    m = yf.shape[0]
    row = lambda z: z.astype(F32).reshape(1, -1)
    ysp = pl.BlockSpec((tm, BW), lambda i: (i, 0))
    return pl.pallas_call(
        _rwpost_kernel,
        grid=(m // tm,),
        in_specs=[ysp, ysp, pl.BlockSpec((tm, 2 * BW), lambda i: (i, 0)),
                  pl.BlockSpec((1, BW), lambda i: (0, 0)), pl.BlockSpec((1, BW), lambda i: (0, 0))],
        out_specs=ysp,
        out_shape=jax.ShapeDtypeStruct((m, BW), BF16),
        compiler_params=_cparams(("parallel",)),
        name="rwkv_post",
    )(yf, yr, e, row(lnx_g), row(lnx_b))


def _df_kernel(lam_ref, sg_ref, q_ref, k_ref, v_ref, g_ref, o_ref, *, lam_init):
    lam = lam_ref[...]
    e1 = jnp.exp(jnp.sum(lam[0:1] * lam[1:2], -1, keepdims=True))
    e2 = jnp.exp(jnp.sum(lam[2:3] * lam[3:4], -1, keepdims=True))
    lam_full = e1 - e2 + lam_init
    d = DF_HEAD_DIM
    tq = q_ref.shape[0]
    qs = _stack_heads(q_ref[...], 2 * DF_HEADS, d)
    k, v = k_ref[...], v_ref[...]
    ss = [_dot_nt(qs[2 * h * tq:2 * (h + 1) * tq], k) for h in range(DF_HEADS)]
    ws, r1s = [], []
    for s in ss:
        e = jnp.exp2(s - jnp.max(s, -1, keepdims=True))
        l = jnp.sum(e, -1, keepdims=True)
        r1 = 1.0 / l[0:tq]
        ws.append((e[0:tq] - e[tq:2 * tq] * (lam_full * l[0:tq] / l[tq:2 * tq])).astype(BF16))
        r1s.append(r1)
    os_ = [jnp.dot(w, v, preferred_element_type=F32) * r1 for w, r1 in zip(ws, r1s)]
    lane = lax.broadcasted_iota(jnp.int32, (1, BW), 1)
    acc = None
    for h, o in enumerate(os_):
        in_head = (lane >= 2 * h * d) & (lane < 2 * (h + 1) * d)
        acc = o if acc is None else jnp.where(in_head, o, acc)
    ms = _seg_sum(acc * acc, 2 * d) * (1.0 / (2 * d))
    o = acc * lax.rsqrt(ms + DF_EPS) * sg_ref[...] * (1.0 - lam_init)
    o_ref[...] = (o * _silu(g_ref[...])).astype(o_ref.dtype)


def _df(cqk, hb, fb, lam, subln_g, lam_init, *, batch, seq, tq=256):
    sg = jnp.tile(subln_g.astype(F32), DF_HEADS).reshape(1, BW)
    return pl.pallas_call(
        functools.partial(_df_kernel, lam_init=lam_init),
        grid=(batch, seq // tq),
        in_specs=[
            pl.BlockSpec((4, DF_HEAD_DIM), lambda b, j: (0, 0)),
            pl.BlockSpec((1, BW), lambda b, j: (0, 0)),
            pl.BlockSpec((None, tq, BW), lambda b, j: (b, j, 0)),
            pl.BlockSpec((None, seq, BW), lambda b, j: (b, 0, 1)),
            pl.BlockSpec((None, seq, BW), lambda b, j: (b, 0, 3)),
            pl.BlockSpec((None, tq, BW), lambda b, j: (b, j, 6)),
        ],
        out_specs=pl.BlockSpec((None, tq, BW), lambda b, j: (b, j, 0)),
        out_shape=jax.ShapeDtypeStruct((batch, seq, BW), BF16),
        compiler_params=_cparams(("parallel", "arbitrary")),
        name="diff_attn",
    )(lam.astype(F32), sg, cqk, cqk, hb, fb)


DL_QB = 2 * DL_HALF
DL_UNROLL = 4


def _dl_group(qkv_ref, kpad, vpad, oacc, lacc, gidx, dil, seq):
    L = seq // dil
    assert L % DL_QB == 0 and (seq // DL_QB) % DL_UNROLL == 0
    nblk = L // DL_QB
    seg = L + 2 * DL_HALF
    zpad = jnp.zeros((DL_HALF, BW), BF16)

    def stage(rho, carry):
        src = pl.ds(pl.multiple_of(rho * L, DL_QB), L)
        krow = pl.multiple_of(rho * seg, DL_HALF)
        for ref in (kpad, vpad):
            ref[pl.ds(krow, DL_HALF), :] = zpad
            ref[pl.ds(krow + DL_HALF + L, DL_HALF), :] = zpad
        for c in range(BW // LANES):
            cols = slice(c * LANES, (c + 1) * LANES)
            kpad[pl.ds(krow + DL_HALF, L), cols] = qkv_ref[2 + c, src, :]
            vpad[pl.ds(krow + DL_HALF, L), cols] = qkv_ref[4 + c, src, :]
        return carry

    lax.fori_loop(0, dil, stage, 0)

    hq = DL_HEADS * DL_QB
    qi = lax.broadcasted_iota(jnp.int32, (hq, 2 * DL_QB), 0) % DL_QB
    ji = lax.broadcasted_iota(jnp.int32, (hq, 2 * DL_QB), 1)
    band = jnp.where(jnp.abs(ji - DL_HALF - qi) <= DL_HALF, 0.0, MASKED)
    jrow = lax.broadcasted_iota(jnp.int32, (1, 2 * DL_QB), 1)
    lane = lax.broadcasted_iota(jnp.int32, (1, BW), 1)

    def blocks(it, carry):
        fs = [it * DL_UNROLL + u for u in range(DL_UNROLL)]
        rhos = [f // nblk for f in fs]
        ns = [f % nblk for f in fs]
        qrows = [pl.ds(pl.multiple_of(f * DL_QB, DL_QB), DL_QB) for f in fs]
        qst = [_stack_heads(jnp.concatenate([qkv_ref[c, r, :] for c in range(BW // LANES)], axis=1),
                            DL_HEADS, DL_HEAD_DIM) for r in qrows]
        krows = [pl.multiple_of(rho * seg + n * DL_QB, DL_HALF) for rho, n in zip(rhos, ns)]
        ss = [_dot_nt(q, kpad[pl.ds(kr, 2 * DL_QB), :]) for q, kr in zip(qst, krows)]
        ps, lses, dens = [], [], []
        for s, n in zip(ss, ns):
            kpos = n * DL_QB - DL_HALF + jrow
            s = s + band + jnp.where((kpos >= 0) & (kpos < L), 0.0, MASKED)
            m = jnp.max(s, -1, keepdims=True)
            p = jnp.exp2(s - m)
            den = jnp.sum(p, -1, keepdims=True)
            ps.append(p.astype(BF16))
            dens.append(den)
            lses.append(m + jnp.log2(den))
        os_ = [jnp.dot(p, vpad[pl.ds(kr, 2 * DL_QB), :], preferred_element_type=F32) / den
               for p, kr, den in zip(ps, krows, dens)]
        for o, lse, rho, n in zip(os_, lses, rhos, ns):
            acc_o = _unstack_heads(o, DL_HEADS, DL_HEAD_DIM)
            acc_l = _unstack_heads(jnp.broadcast_to(lse, (hq, BW)), DL_HEADS, DL_HEAD_DIM)
            if dil > 1:
                dst = pl.ds(rho + dil * n * DL_QB, DL_QB, stride=dil)
            else:
                dst = pl.ds(pl.multiple_of(n * DL_QB, DL_QB), DL_QB)
            for c in range(BW // LANES):
                oacc[gidx, c, dst, :] = acc_o[:, c * LANES:(c + 1) * LANES]
                lacc[gidx, c, dst, :] = acc_l[:, c * LANES:(c + 1) * LANES]
        return carry

    lax.fori_loop(0, seq // DL_QB // DL_UNROLL, blocks, 0)


def _dl_kernel(*refs, seq):
    ng = len(DL_DILATIONS)
    qkv_refs, (g_ref, o_ref, kpad, vpad, oacc, lacc) = refs[:ng], refs[ng:]
    gi = pl.program_id(1)
    for gidx, dil in enumerate(DL_DILATIONS):
        @pl.when(gi == gidx)
        def _(gidx=gidx, dil=dil):
            _dl_group(qkv_refs[gidx], kpad, vpad, oacc, lacc, gidx, dil, seq)

    @pl.when(gi == len(DL_DILATIONS) - 1)
    def _():
        rc = 256

        def chunk(c, carry):
            rows = pl.ds(pl.multiple_of(c * rc, rc), rc)
            for lc in range(BW // LANES):
                cols = slice(lc * LANES, (lc + 1) * LANES)
                ls = [lacc[g, lc, rows, :] for g in range(len(DL_DILATIONS))]
                mx = functools.reduce(jnp.maximum, ls)
                ws = [jnp.exp2(l - mx) for l in ls]
                num = sum(w * oacc[g, lc, rows, :] for g, w in enumerate(ws))
                out = num / sum(ws)
                o_ref[rows, cols] = (out * _silu(g_ref[rows, cols])).astype(o_ref.dtype)
            return carry

        lax.fori_loop(0, seq // rc, chunk, 0)


def _dl(dqkvs, fb, *, batch, seq):
    ng = len(DL_DILATIONS)
    nch = 3 * BW // LANES
    assert all(w // (2 * d) == DL_HALF for w, d in zip(DL_WINDOWS, DL_DILATIONS))
    return pl.pallas_call(
        functools.partial(_dl_kernel, seq=seq),
        grid=(batch, ng),
        in_specs=[pl.BlockSpec((nch, None, seq, LANES), lambda b, g: (0, b, 0, 0))] * ng
        + [pl.BlockSpec((None, seq, BW), lambda b, g: (b, 0, 7))],
        out_specs=pl.BlockSpec((None, seq, BW), lambda b, g: (b, 0, 0)),
        out_shape=jax.ShapeDtypeStruct((batch, seq, BW), BF16),
        scratch_shapes=[
            pltpu.VMEM((seq + 2 * DL_HALF * max(DL_DILATIONS), BW), BF16),
            pltpu.VMEM((seq + 2 * DL_HALF * max(DL_DILATIONS), BW), BF16),
            pltpu.VMEM((ng, BW // LANES, seq, LANES), F32),
            pltpu.VMEM((ng, BW // LANES, seq, LANES), F32),
        ],
        compiler_params=_cparams(("parallel", "arbitrary")),
        name="dilated_attn",
    )(*dqkvs, fb)


_A0 = 0
_B0 = _A0 + 4 * BW
_C0 = _B0 + 4 * BW + 4 * RW_LORA
_D0 = _C0 + 4 * BW
_DG0 = _D0 + 9 * BW
_G0 = _DG0 + BW
PROJ_TM = 2048


def _layer(xf, xb, l, p, tabs, *, batch, seq, last_split):
    w, b = p["w_in"][l], p["b_in"][l]
    cols = lambda lo, n: (w[:, lo:lo + n], b[lo:lo + n])
    cat = lambda parts: (jnp.concatenate([q[0] for q in parts], 1).astype(BF16),
                         jnp.concatenate([q[1] for q in parts], 0).astype(F32))
    scaled = lambda part, s: (part[0] * s, part[1] * s)
    q_scale = NA_HEAD_DIM ** -0.5 * LOG2E
    proj = functools.partial(_proj, xb, seq=seq, tm=PROJ_TM)

    gates = proj(*cat([scaled(cols(_G0, 4 * D_MODEL), 0.5)]), BF16, tn=1024, sigmoid_2x=True,
                 name="proj_gates")
    fb = proj(*cat([cols(_B0, _C0 - _B0), cols(_A0 + 3 * BW, BW), cols(_C0 + 3 * BW, BW),
                    cols(_DG0, BW)]), F32, tn=1024, name="proj_f32")
    hb = proj(*cat([scaled(cols(_A0, BW), q_scale), cols(_A0 + BW, 2 * BW), cols(_C0 + 2 * BW, BW)]),
              BF16, tn=1024, name="proj_bf16")
    cpb = BW // LANES
    cqk = proj(*cat([cols(_C0, 2 * BW)]), BF16, tn=2 * BW, name="proj_rot_c",
               rot=(DF_HEAD_DIM,) + tabs["c"] + ((0,) * cpb + (1,) * cpb,))
    dqs = [proj(*cat([scaled(cols(_D0 + 3 * g * BW, BW), q_scale),
                      cols(_D0 + (3 * g + 1) * BW, 2 * BW)]),
                BF16, tn=3 * BW, residue_dil=dil, name="proj_rot_d",
                rot=(DL_HEAD_DIM,) + tabs["d"] + ((0,) * (2 * cpb) + (None,) * cpb,))
           for g, dil in enumerate(DL_DILATIONS)]

    r3 = lambda z: z.reshape(batch, seq, z.shape[-1])
    hb3, fb3 = r3(hb), r3(fb)
    ya = _na(hb3, fb3, _na_bias_tables(p["na_rpb"][l], seq // GRID_W), batch=batch, seq=seq)
    f, e = _rwprep(fb3, p["rw_mu"][l], p["rw_w0"][l], p["rw_w2"][l], p["rw_a0"][l], p["rw_a2"][l],
                   p["rw_kk"][l], p["rw_ka"][l], p["rw_rk"][l].reshape(-1), batch=batch, seq=seq)
    yf, yr = _rwscan(f, batch=batch, seq=seq)
    m = batch * seq
    yb = _rwpost(yf.reshape(m, BW), yr.reshape(m, BW), e.reshape(m, 2 * BW), p["rw_lnx_g"][l],
                 p["rw_lnx_b"][l])
    lam_init = 0.8 - 0.6 * math.exp(-0.3 * l)
    yc = _df(r3(cqk), hb3, fb3, p["df_lam"][l], p["df_subln_g"][l], lam_init, batch=batch, seq=seq)
    yd = _dl([dq.reshape(dq.shape[0], batch, seq, LANES) for dq in dqs], fb3, batch=batch, seq=seq)

    margs = (ya.reshape(m, BW), yb, yc.reshape(m, BW), yd.reshape(m, BW), gates, xf,
             p["w_branch"][l].astype(BF16), p["w_out"][l].astype(BF16), p["b_out"][l],
             p["ln_g"][l], p["ln_b"][l])
    if last_split is None:
        return _merge(*margs, row0=0, rows=m, with_bf16=True)
    return [_merge(*margs, row0=r0, rows=n, with_bf16=False)[0] for r0, n in last_split]


def kernel(x_prompt, x_sample, ln0_g, ln0_b, w_in, b_in, na_rpb, rw_mu, rw_w0, rw_w2, rw_a0, rw_a2,
           rw_kk, rw_ka, rw_rk, rw_lnx_g, rw_lnx_b, df_lam, df_subln_g, w_branch, w_out, b_out,
           ln_g, ln_b):
    p = dict(w_in=w_in, b_in=b_in, na_rpb=na_rpb, rw_mu=rw_mu, rw_w0=rw_w0, rw_w2=rw_w2,
             rw_a0=rw_a0, rw_a2=rw_a2, rw_kk=rw_kk, rw_ka=rw_ka, rw_rk=rw_rk, rw_lnx_g=rw_lnx_g,
             rw_lnx_b=rw_lnx_b, df_lam=df_lam, df_subln_g=df_subln_g, w_branch=w_branch,
             w_out=w_out, b_out=b_out, ln_g=ln_g, ln_b=ln_b)
    bp, seq, _ = x_prompt.shape
    bs = x_sample.shape[0]
    assert x_sample.shape[1] == seq
    batch = bp + bs
    xf, xb = _ln0(x_prompt.reshape(bp * seq, D_MODEL), x_sample.reshape(bs * seq, D_MODEL),
                  ln0_g, ln0_b)
    tabs = {"c": _rope_tables(seq, DF_HEAD_DIM, (DF_HEAD_DIM ** -0.5 * LOG2E, 1.0)),
            "d": _rope_tables(seq, DL_HEAD_DIM, (1.0,))}
    for l in range(DEPTH - 1):
        xf, xb = _layer(xf, xb, l, p, tabs, batch=batch, seq=seq, last_split=None)
    split = [(0, bp * seq), (bp * seq, bs * seq)]
    yp, ys = _layer(xf, xb, DEPTH - 1, p, tabs, batch=batch, seq=seq, last_split=split)
    return yp.reshape(bp, seq, D_MODEL), ys.reshape(bs, seq, D_MODEL)
```

```python
import functools
import math

import numpy as np
import jax
import jax.numpy as jnp
from jax import lax
from jax.experimental import pallas as pl
from jax.experimental.pallas import tpu as pltpu

F32 = jnp.float32
BF16 = jnp.bfloat16

D_MODEL = 1024
DEPTH = 2
GRID_W = 64
NA_HEADS, NA_HEAD_DIM, NA_ROWS, NA_COLS = 4, 64, 8, 16
RW_HEADS, RW_HEAD_DIM, RW_LORA = 4, 64, 64
RW_LNX_EPS = 64e-5
DF_HEADS, DF_HEAD_DIM, DF_EPS = 4, 32, 1e-5
DL_HEADS, DL_HEAD_DIM = 4, 64
DL_DILATIONS = (1, 4, 16)
DL_WINDOWS = (128, 512, 2048)
DL_HALF = 64
BW = 256
ROPE_THETA = 10000.0
LN_EPS = 1e-5
DEEPNORM_ALPHA = (2 * DEPTH) ** 0.25
LOG2E = math.log2(math.e)
MASKED = -float("inf")

LANES = 128
VMEM_LIMIT = 56 * 1024 * 1024


def _cparams(sem):
    return pltpu.CompilerParams(dimension_semantics=sem, vmem_limit_bytes=VMEM_LIMIT)


def _sigmoid(x):
    return 0.5 * jnp.tanh(0.5 * x) + 0.5


def _silu(x):
    return x * _sigmoid(x)


def _dot_nt(a, b):
    return lax.dot_general(a, b, (((1,), (1,)), ((), ())), preferred_element_type=F32)


def _split2(x):
    hi = x.astype(BF16)
    return hi, (x - hi.astype(F32)).astype(BF16)


def _seg_sum(x, seg):
    n = x.shape[-1]
    r = lax.broadcasted_iota(jnp.int32, (n, n), 0) // seg
    c = lax.broadcasted_iota(jnp.int32, (n, n), 1) // seg
    ones = (r == c).astype(BF16)
    out = None
    for t in _split2(x):
        y = jnp.dot(t, ones, preferred_element_type=F32)
        out = y if out is None else out + y
    return out


def _ln0_kernel(xp_ref, xs_ref, g_ref, b_ref, of_ref, ob_ref, *, n_prompt):
    def norm(x):
        mu = jnp.mean(x, -1, keepdims=True)
        xc = x - mu
        var = jnp.mean(xc * xc, -1, keepdims=True)
        y = xc * lax.rsqrt(var + LN_EPS) * g_ref[...] + b_ref[...]
        of_ref[...] = y
        ob_ref[...] = y.astype(BF16)

    i = pl.program_id(0)

    @pl.when(i < n_prompt)
    def _():
        norm(xp_ref[...])

    @pl.when(i >= n_prompt)
    def _():
        norm(xs_ref[...])


def _ln0(xp, xs, g, b, tm=1024):
    mp, ms = xp.shape[0], xs.shape[0]
    n_p, n_s = mp // tm, ms // tm
    out = jax.ShapeDtypeStruct((mp + ms, D_MODEL), F32)
    outb = jax.ShapeDtypeStruct((mp + ms, D_MODEL), BF16)
    return pl.pallas_call(
        functools.partial(_ln0_kernel, n_prompt=n_p),
        grid=(n_p + n_s,),
        in_specs=[
            pl.BlockSpec((tm, D_MODEL), lambda i: (jnp.minimum(i, n_p - 1), 0)),
            pl.BlockSpec((tm, D_MODEL), lambda i: (jnp.maximum(i - n_p, 0), 0)),
            pl.BlockSpec((1, D_MODEL), lambda i: (0, 0)),
            pl.BlockSpec((1, D_MODEL), lambda i: (0, 0)),
        ],
        out_specs=[pl.BlockSpec((tm, D_MODEL), lambda i: (i, 0)),
                   pl.BlockSpec((tm, D_MODEL), lambda i: (i, 0))],
        out_shape=[out, outb],
        compiler_params=_cparams(("parallel",)),
        name="ln0",
    )(xp, xs, g.reshape(1, -1), b.reshape(1, -1))


def _proj_kernel(x_ref, w_ref, b_ref, *rest, rot_hd, kinds, residue_dil, sigmoid_2x):
    acc = jnp.dot(x_ref[...], w_ref[...], preferred_element_type=F32) + b_ref[...]
    if not rot_hd:
        (o_ref,) = rest
        o_ref[...] = (0.5 * jnp.tanh(acc) + 0.5 if sigmoid_2x else acc).astype(o_ref.dtype)
        return
    cs_ref, sn_ref, o_ref = rest[:3]
    h2 = rot_hd // 2
    lane = lax.broadcasted_iota(jnp.int32, (1, LANES), 1)
    first = (lane % rot_hd) < h2
    for c, kind in enumerate(kinds):
        val = acc[:, c * LANES:(c + 1) * LANES]
        if kind is not None:
            sw = jnp.where(first, pltpu.roll(val, LANES - h2, 1), pltpu.roll(val, h2, 1))
            val = val * cs_ref[kind] + sw * sn_ref[kind]
        if not residue_dil:
            o_ref[:, c * LANES:(c + 1) * LANES] = val.astype(o_ref.dtype)
        elif residue_dil == 1:
            o_ref[c] = val.astype(o_ref.dtype)
        else:
            tmp_ref = rest[3]
            tmp_ref[...] = val
            L = val.shape[0] // residue_dil
            for rho in range(residue_dil):
                o_ref[c, rho * L:(rho + 1) * L, :] = (
                    tmp_ref[pl.ds(rho, L, stride=residue_dil), :].astype(o_ref.dtype))


def _proj(xb, w, b, out_dtype, *, seq, tm, tn, rot=None, residue_dil=0, sigmoid_2x=False,
          name="proj"):
    m, n = xb.shape[0], w.shape[1]
    assert m % tm == 0 and n % tn == 0 and seq % tm == 0
    in_specs = [
        pl.BlockSpec((tm, D_MODEL), lambda i, j: (i, 0)),
        pl.BlockSpec((D_MODEL, tn), lambda i, j: (0, j)),
        pl.BlockSpec((1, tn), lambda i, j: (0, j)),
    ]
    args = [xb, w, b.reshape(1, -1)]
    rot_hd, kinds, scratch = 0, None, []
    if rot is not None:
        rot_hd, cos, sin, kinds = rot
        assert len(kinds) == tn // LANES and LANES % rot_hd == 0
        spt = seq // tm
        tab = pl.BlockSpec((cos.shape[0], tm, LANES), lambda i, j: (0, i % spt, 0))
        in_specs += [tab, tab]
        args += [cos, sin]
    if residue_dil:
        assert rot is not None and tm == seq and seq % residue_dil == 0
        out_spec = pl.BlockSpec((tn // LANES, tm, LANES), lambda i, j: (j, i, 0))
        out_shape = jax.ShapeDtypeStruct((n // LANES, m, LANES), out_dtype)
        if residue_dil > 1:
            scratch = [pltpu.VMEM((tm, LANES), F32)]
    else:
        out_spec = pl.BlockSpec((tm, tn), lambda i, j: (i, j))
        out_shape = jax.ShapeDtypeStruct((m, n), out_dtype)
    return pl.pallas_call(
        functools.partial(_proj_kernel, rot_hd=rot_hd, kinds=kinds, residue_dil=residue_dil,
                          sigmoid_2x=sigmoid_2x),
        grid=(m // tm, n // tn),
        in_specs=in_specs,
        out_specs=out_spec,
        out_shape=out_shape,
        scratch_shapes=scratch,
        compiler_params=_cparams(("parallel", "arbitrary")),
        name=name,
    )(*args)


def _rope_tables(seq, head_dim, scales):
    half = head_dim // 2
    inv_freq = jnp.power(ROPE_THETA, -jnp.arange(half, dtype=F32) / half)
    ang = jnp.arange(seq, dtype=F32)[:, None] * inv_freq[None, :]
    lane = np.arange(LANES)
    f_idx = lane % half
    sign = np.where((lane % head_dim) < half, -1.0, 1.0).astype(np.float32)
    cos = jnp.cos(ang)[:, f_idx]
    sin = jnp.sin(ang)[:, f_idx] * sign[None, :]
    return (jnp.stack([cos * F32(s) for s in scales]), jnp.stack([sin * F32(s) for s in scales]))


def _merge_kernel(ya_ref, yb_ref, yc_ref, yd_ref, g_ref, x_ref, wb_ref, wo_ref, bo_ref, lg_ref,
                  lb_ref, of_ref, *maybe_ob_ref):
    merged = None
    for i, y_ref in enumerate((ya_ref, yb_ref, yc_ref, yd_ref)):
        p = jnp.dot(y_ref[...], wb_ref[i], preferred_element_type=F32)
        gate = g_ref[:, i * D_MODEL:(i + 1) * D_MODEL].astype(F32)
        merged = gate * p if merged is None else merged + gate * p
    y = jnp.dot(merged.astype(BF16), wo_ref[...], preferred_element_type=F32) + bo_ref[...]
    z = DEEPNORM_ALPHA * x_ref[...] + y
    mu = jnp.mean(z, -1, keepdims=True)
    zc = z - mu
    var = jnp.mean(zc * zc, -1, keepdims=True)
    out = zc * lax.rsqrt(var + LN_EPS) * lg_ref[...] + lb_ref[...]
    of_ref[...] = out
    for ob_ref in maybe_ob_ref:
        ob_ref[...] = out.astype(BF16)


def _merge(ya, yb, yc, yd, gates, x, wb, wo, bo, lg, lb, *, row0, rows, with_bf16, tm=512):
    outs = [F32, BF16] if with_bf16 else [F32]
    assert row0 % tm == 0 and rows % tm == 0
    o = row0 // tm
    ysp = pl.BlockSpec((tm, BW), lambda i: (i + o, 0))
    const2 = lambda i: (0, 0)
    return pl.pallas_call(
        _merge_kernel,
        grid=(rows // tm,),
        in_specs=[ysp, ysp, ysp, ysp,
                  pl.BlockSpec((tm, 4 * D_MODEL), lambda i: (i + o, 0)),
                  pl.BlockSpec((tm, D_MODEL), lambda i: (i + o, 0)),
                  pl.BlockSpec((4, BW, D_MODEL), lambda i: (0, 0, 0)),
                  pl.BlockSpec((D_MODEL, D_MODEL), const2),
                  pl.BlockSpec((1, D_MODEL), const2),
                  pl.BlockSpec((1, D_MODEL), const2),
                  pl.BlockSpec((1, D_MODEL), const2)],
        out_specs=[pl.BlockSpec((tm, D_MODEL), lambda i: (i, 0)) for _ in outs],
        out_shape=[jax.ShapeDtypeStruct((rows, D_MODEL), dt) for dt in outs],
        compiler_params=_cparams(("parallel",)),
        name="merge",
    )(ya, yb, yc, yd, gates, x, wb, wo, bo.reshape(1, -1), lg.reshape(1, -1), lb.reshape(1, -1))


NA_QR = 4


def _na_geometry(rows):
    kr, wr = min(NA_ROWS, rows), min(NA_ROWS, rows) + NA_QR - 1
    assert rows % NA_QR == 0 and rows >= wr and kr == NA_ROWS
    steps = rows // NA_QR
    a = np.arange(NA_QR)[:, None]
    j = np.arange(wr)[None, :]
    pats = []
    for g in range(steps):
        r = NA_QR * g + a
        rs = np.clip(r - kr // 2, 0, rows - kr)
        ws = int(np.clip(NA_QR * g - kr // 2, 0, rows - wr))
        assert rs.min() >= ws and rs.max() + kr <= ws + wr
        valid = (ws + j >= rs) & (ws + j < rs + kr)
        dr = np.clip(ws + j - r + NA_ROWS - 1, 0, 2 * NA_ROWS - 2)
        pats.append((valid, np.where(valid, dr, 0)))
    same = lambda x, y: np.array_equal(x[0], y[0]) and np.array_equal(x[1], y[1])
    assert steps >= 3 and all(same(pats[g], pats[1]) for g in range(1, steps - 1))
    return kr, wr, steps, [pats[0], pats[1], pats[steps - 1]]


def _na_bias_tables(rpb, rows):
    kr, wr, steps, pats = _na_geometry(rows)
    qc = np.arange(GRID_W)[:, None]
    kc = np.arange(GRID_W)[None, :]
    c_start = np.clip(qc - NA_COLS // 2, 0, GRID_W - NA_COLS)
    col_ok = (kc >= c_start) & (kc < c_start + NA_COLS)
    dc = np.clip(kc - qc + NA_COLS - 1, 0, 2 * NA_COLS - 2)
    tabs = []
    by_col = rpb.astype(F32)[:, :, dc]
    for valid, dr in pats:
        b = by_col[:, dr] * LOG2E
        ok = valid[:, :, None, None] & col_ok[None, None]
        b = jnp.where(ok[None], b, MASKED)
        tabs.append(b.transpose(0, 1, 3, 2, 4).reshape(NA_HEADS, NA_QR * GRID_W, wr * GRID_W))
    return jnp.stack(tabs, axis=1)


def _stack_heads(q, n_heads, head_dim):
    lane = lax.broadcasted_iota(jnp.int32, (1, q.shape[1]), 1)
    zero = jnp.zeros_like(q)
    return jnp.concatenate(
        [jnp.where((lane >= h * head_dim) & (lane < (h + 1) * head_dim), q, zero)
         for h in range(n_heads)], axis=0)


def _unstack_heads(o, n_heads, head_dim):
    m = o.shape[0] // n_heads
    lane = lax.broadcasted_iota(jnp.int32, (1, o.shape[1]), 1)
    acc = o[0:m]
    for h in range(1, n_heads):
        in_head = (lane >= h * head_dim) & (lane < (h + 1) * head_dim)
        acc = jnp.where(in_head, o[h * m:(h + 1) * m], acc)
    return acc


def _na_kernel(q_ref, k_ref, v_ref, g_ref, bias_ref, o_ref, *, rows):
    kr, wr = min(NA_ROWS, rows), min(NA_ROWS, rows) + NA_QR - 1
    g = pl.program_id(1)
    ws = jnp.clip(NA_QR * g - kr // 2, 0, rows - wr)
    start = pl.multiple_of(ws * GRID_W, GRID_W)
    k = k_ref[pl.ds(start, wr * GRID_W), :]
    v = v_ref[pl.ds(start, wr * GRID_W), :]
    qs = _stack_heads(q_ref[...], NA_HEADS, NA_HEAD_DIM)
    m_rows = q_ref.shape[0]
    parts = [slice(h * m_rows, (h + 1) * m_rows) for h in range(NA_HEADS)]
    ss = [_dot_nt(qs[h], k) for h in parts]
    ps, ls = [], []
    for i, s in enumerate(ss):
        s = s + bias_ref[i]
        p = jnp.exp2(s - jnp.max(s, -1, keepdims=True))
        ls.append(jnp.sum(p, -1, keepdims=True))
        ps.append(p.astype(BF16))
    o = jnp.concatenate([jnp.dot(p, v, preferred_element_type=F32) / l for p, l in zip(ps, ls)], 0)
    o = _unstack_heads(o, NA_HEADS, NA_HEAD_DIM)
    o_ref[...] = (o * _silu(g_ref[...])).astype(o_ref.dtype)


def _na(hb, fb, bias_tab, *, batch, seq):
    rows = seq // GRID_W
    kr, wr, steps, _ = _na_geometry(rows)
    qb = NA_QR * GRID_W

    def pattern_of(g):
        return jnp.where(g == 0, 0, jnp.where(g == steps - 1, 2, 1))

    return pl.pallas_call(
        functools.partial(_na_kernel, rows=rows),
        grid=(batch, steps),
        in_specs=[
            pl.BlockSpec((None, qb, BW), lambda b, g: (b, g, 0)),
            pl.BlockSpec((None, seq, BW), lambda b, g: (b, 0, 1)),
            pl.BlockSpec((None, seq, BW), lambda b, g: (b, 0, 2)),
            pl.BlockSpec((None, qb, BW), lambda b, g: (b, g, 5)),
            pl.BlockSpec((NA_HEADS, None, qb, wr * GRID_W), lambda b, g: (0, pattern_of(g), 0, 0)),
        ],
        out_specs=pl.BlockSpec((None, qb, BW), lambda b, g: (b, g, 0)),
        out_shape=jax.ShapeDtypeStruct((batch, seq, BW), BF16),
        compiler_params=_cparams(("parallel", "arbitrary")),
        name="na_attn",
    )(hb, hb, hb, fb, bias_tab)


RW_C = 64
RW_FIELDS = 9
RW_INV_BASE = 4


def _mm(a, b, mode="nn", passes=1):
    dims = {"nn": (((1,), (0,)), ((), ())), "nt": (((1,), (1,)), ((), ()))}[mode]
    dg = lambda x, y: lax.dot_general(x, y, dims, preferred_element_type=F32)
    ah, bh = a.astype(BF16), b.astype(BF16)
    out = dg(ah, bh)
    if passes == 3:
        al = (a - ah.astype(F32)).astype(BF16)
        bl = (b - bh.astype(F32)).astype(BF16)
        out = out + (dg(al, bh) + dg(ah, bl))
    return out


def _mm_bd(a, b, mode="nn"):
    halves = [slice(i * LANES, (i + 1) * LANES) for i in range(BW // LANES)]
    return jnp.concatenate([_mm(a[:, h], b[h, h], mode) for h in halves], axis=1)


def _rwprep_kernel(x_ref, prev_ref, next_ref, mu_ref, w0_ref, w2_ref, a0_ref, a2_ref, kk_ref, ka_ref,
                   rk_ref, f_ref, e_ref, *, rc):
    c = pl.program_id(1)
    lane = lax.broadcasted_iota(jnp.int32, (1, 2 * RW_LORA), 1)
    ri = lax.broadcasted_iota(jnp.int32, (rc, 1), 0)
    x = x_ref[...]
    prev_row = jnp.where(c > 0, prev_ref[7:8, :], 0.0)
    next_row = jnp.where(c < pl.num_programs(1) - 1, next_ref[0:1, :], 0.0)
    up = jnp.where(ri == 0, prev_row, pltpu.roll(x, 1, 0))
    dn = jnp.where(ri == rc - 1, next_row, pltpu.roll(x, rc - 1, 0))
    mu = mu_ref[...]
    xs = x * (1.0 - mu) + (0.5 * mu) * (up + dn)
    r, k, v, g = (xs[:, i * BW:(i + 1) * BW] for i in range(4))
    wl = xs[:, 4 * BW:4 * BW + 2 * RW_LORA]
    al = xs[:, 4 * BW + 2 * RW_LORA:4 * BW + 4 * RW_LORA]
    kk = k * kk_ref[...]
    kap = kk * lax.rsqrt(jnp.maximum(_seg_sum(kk * kk, RW_HEAD_DIM), 1e-24))
    tw = jnp.tanh(wl)
    k_scaled = k * ka_ref[...]
    k_fixed = k * (1.0 - ka_ref[...])
    kd_sum = None
    for d in range(2):
        dm = ((lane >= d * RW_LORA) & (lane < (d + 1) * RW_LORA)).astype(F32)
        w_raw = w0_ref[d:d + 1, :] + _mm(tw * dm, w2_ref[...], passes=3)
        lw = _sigmoid(w_raw) * (-math.exp(-0.5))
        a = _sigmoid(a0_ref[d:d + 1, :] + _mm(al * dm, a2_ref[...], passes=3))
        kd = k_fixed + k_scaled * a
        f_ref[:, (3 + 3 * d) * BW:(4 + 3 * d) * BW] = lw
        f_ref[:, (4 + 3 * d) * BW:(5 + 3 * d) * BW] = kd
        f_ref[:, (5 + 3 * d) * BW:(6 + 3 * d) * BW] = kap * a
        kd_sum = kd if kd_sum is None else kd_sum + kd
    f_ref[:, 0:BW] = r
    f_ref[:, BW:2 * BW] = v
    f_ref[:, 2 * BW:3 * BW] = kap
    bonus = _seg_sum(r * kd_sum * rk_ref[...], RW_HEAD_DIM)
    e_ref[:, 0:BW] = bonus * v
    e_ref[:, BW:2 * BW] = _silu(g)


def _rwprep(fb, mu, w0, w2, a0, a2, k_k, k_a, r_k, *, batch, seq, rc=256):
    ucols = 4 * BW + 4 * RW_LORA
    row = lambda z: z.astype(F32).reshape(1, -1)
    c2 = lambda b, c: (0, 0)
    tpb = rc // 8
    return pl.pallas_call(
        functools.partial(_rwprep_kernel, rc=rc),
        grid=(batch, seq // rc),
        in_specs=[
            pl.BlockSpec((None, rc, ucols), lambda b, c: (b, c, 0)),
            pl.BlockSpec((None, 8, ucols), lambda b, c: (b, jnp.maximum(c * tpb - 1, 0), 0)),
            pl.BlockSpec((None, 8, ucols),
                         lambda b, c: (b, jnp.minimum((c + 1) * tpb, seq // 8 - 1), 0)),
            pl.BlockSpec((1, ucols), c2),
            pl.BlockSpec((2, BW), c2),
            pl.BlockSpec((2 * RW_LORA, BW), c2),
            pl.BlockSpec((2, BW), c2),
            pl.BlockSpec((2 * RW_LORA, BW), c2),
            pl.BlockSpec((1, BW), c2),
            pl.BlockSpec((1, BW), c2),
            pl.BlockSpec((1, BW), c2),
        ],
        out_specs=[pl.BlockSpec((None, rc, RW_FIELDS * BW), lambda b, c: (b, c, 0)),
                   pl.BlockSpec((None, rc, 2 * BW), lambda b, c: (b, c, 0))],
        out_shape=[jax.ShapeDtypeStruct((batch, seq, RW_FIELDS * BW), F32),
                   jax.ShapeDtypeStruct((batch, seq, 2 * BW), F32)],
        compiler_params=_cparams(("parallel", "arbitrary")),
        name="rwkv_prep",
    )(fb, fb, fb, row(mu), w0.astype(F32), w2.astype(F32).reshape(2 * RW_LORA, BW), a0.astype(F32),
      a2.astype(F32).reshape(2 * RW_LORA, BW), row(k_k), row(k_a), row(r_k))


def _rw_bd(x):
    lane = lax.broadcasted_iota(jnp.int32, (1, BW), 1)
    xb = x.astype(BF16)
    zero = jnp.zeros_like(xb)
    return jnp.concatenate(
        [jnp.where((lane >= h * RW_C) & (lane < (h + 1) * RW_C), xb, zero) for h in range(RW_HEADS)],
        axis=0)


def _rw_advance(terms, z):
    gm, hv, nt = terms
    both = _mm_bd(gm, z)
    return both[0:RW_C] + hv, _rw_bd(both[RW_C:2 * RW_C] + nt)


def _rw_prepare(specs):
    C, H = RW_C, RW_HEADS
    assert H * C == BW and RW_HEAD_DIM == C
    bd = _rw_bd
    cat = lambda a, b: jnp.concatenate([a, b], axis=0)
    ri = lax.broadcasted_iota(jnp.int32, (C, C), 0)
    ci = lax.broadcasted_iota(jnp.int32, (C, C), 1)
    tris = [(ci <= ri).astype(BF16), (ci >= ri).astype(BF16)]
    row = lax.broadcasted_iota(jnp.int32, (C, BW), 0)
    col = lax.broadcasted_iota(jnp.int32, (C, BW), 1) % C
    eye = col == row
    strict = [col < row, col > row]
    incl = [col <= row, col >= row]
    same_block = lambda s: (row // s) == (col // s)

    def lc_t(x):
        t = bd(x).astype(F32).T
        return t[0:C] + t[C:2 * C] + t[2 * C:3 * C] + t[3 * C:4 * C]

    def load(spec):
        f_ref, r0, d = spec
        fld = lambda i: f_ref[r0:r0 + C, i * BW:(i + 1) * BW]
        c = dict(d=d, r=fld(0), v=fld(1), kap=fld(2), lw=fld(3 + 3 * d), kd=fld(4 + 3 * d),
                 beta=fld(5 + 3 * d))
        cs = None
        for t in _split2(c["lw"]):
            y = jnp.dot(tris[d], t, preferred_element_type=F32)
            cs = y if cs is None else cs + y
        c["cs"] = cs
        return c

    def scale(c):
        cs, d = c["cs"], c["d"]
        tot = cs[0:1] if d == 1 else cs[C - 1:C]
        ginv = jnp.exp(-cs)
        gto = jnp.exp(tot - cs)
        kt = c["kap"] * jnp.exp(cs - c["lw"])
        rt = c["r"] * jnp.exp(cs)
        c.update(gc=jnp.exp(tot), rt=rt, ktrt=cat(kt, rt), kt_b=bd(kt), v_b=bd(c["v"]),
                 kg_b=bd(c["kd"] * ginv), bg_b=bd(c["beta"] * ginv),
                 kh=c["kd"] * gto, bh=c["beta"] * gto)
        return c

    def gram(c):
        d = c["d"]
        gk = _mm_bd(c["ktrt"], c["kg_b"], "nt")
        gb = _mm_bd(c["ktrt"], c["bg_b"], "nt")
        l_bk = jnp.where(strict[d], gb[0:C], 0.0)
        c.update(l_kk=jnp.where(strict[d], gk[0:C], 0.0), l_kr=jnp.where(incl[d], gk[C:2 * C], 0.0),
                 l_br=jnp.where(incl[d], gb[C:2 * C], 0.0), l_bk=l_bk,
                 nbase=-jnp.where(same_block(RW_INV_BASE), l_bk, 0.0))
        return c

    def base_square(c):
        c["nsq"] = _mm_bd(c["nbase"], bd(c["nbase"]))
        return c

    def base_inverse(c):
        w = jnp.where(eye, 1.0, 0.0) + c.pop("nbase")
        c["w"] = w + _mm_bd(w, bd(c.pop("nsq")))
        return c

    def merge_left(c, s):
        off = same_block(2 * s) & jnp.logical_not(same_block(s))
        c["t"] = _mm_bd(c["w"], bd(jnp.where(off, c["l_bk"], 0.0)))
        return c

    def merge_right(c):
        c["w"] = c["w"] - _mm_bd(c.pop("t"), bd(c["w"]))
        return c

    def solve(c):
        c["p1_b"] = bd(_mm_bd(c["w"], c["kt_b"]))
        c["p2"] = _mm_bd(c["w"], bd(c["l_kk"]))
        c["p2_b"] = bd(c["p2"])
        return c

    def outputs(c):
        both = _mm_bd(cat(c["l_br"], c["bh_t"]), c["p1_b"])
        c["gm"] = cat(c["rt"] - both[0:C], jnp.where(eye, c["gc"], 0.0) - both[C:2 * C])
        hm = c["l_kr"] - _mm_bd(c["l_br"], c["p2_b"])
        three = _mm_bd(jnp.concatenate([hm, c["p2"], c["kh_t"]], axis=0), c["v_b"])
        c["hv"], c["p2v_b"], c["khv"] = three[0:C], bd(three[C:2 * C]), three[2 * C:3 * C]
        return c

    def state(c):
        nt = c["khv"] - _mm_bd(c["bh_t"], c["p2v_b"])
        return c["gm"].astype(BF16), c["hv"], nt

    def transposes(c):
        c["kh_t"], c["bh_t"] = lc_t(c.pop("kh")), lc_t(c.pop("bh"))
        return c

    cs = [load(s) for s in specs]
    cs = [scale(c) for c in cs]
    cs = [gram(c) for c in cs]
    cs = [transposes(c) for c in cs]
    cs = [base_square(c) for c in cs]
    cs = [base_inverse(c) for c in cs]
    s = RW_INV_BASE
    while s < C:
        cs = [merge_left(c, s) for c in cs]
        cs = [merge_right(c) for c in cs]
        s *= 2
    cs = [solve(c) for c in cs]
    cs = [outputs(c) for c in cs]
    return [state(c) for c in cs]


def _rwscan_kernel(ff_ref, fr_ref, yf_ref, yr_ref, zf_ref, zr_ref, *, nch):
    @pl.when(pl.program_id(1) == 0)
    def _():
        zf_ref[...] = jnp.zeros_like(zf_ref)
        zr_ref[...] = jnp.zeros_like(zr_ref)

    C = RW_C
    pre = _rw_prepare([(ff_ref, j * C, 0) for j in range(nch)]
                      + [(fr_ref, j * C, 1) for j in range(nch)])
    pre_f, pre_r = pre[:nch], pre[nch:]
    zf, zr = zf_ref[...], zr_ref[...]
    for j in range(nch):
        yf_ref[j * C:(j + 1) * C, :], zf = _rw_advance(pre_f[j], zf)
        jr = nch - 1 - j
        yr_ref[jr * C:(jr + 1) * C, :], zr = _rw_advance(pre_r[jr], zr)
    zf_ref[...] = zf
    zr_ref[...] = zr


def _rwscan(f, *, batch, seq, nch=4):
    rb = nch * RW_C
    nc = seq // rb
    fsp = lambda imap: pl.BlockSpec((None, rb, RW_FIELDS * BW), imap)
    ysp = lambda imap: pl.BlockSpec((None, rb, BW), imap)
    fwd = lambda b, s: (b, s, 0)
    rev = lambda b, s: (b, nc - 1 - s, 0)
    y = jax.ShapeDtypeStruct((batch, seq, BW), F32)
    return pl.pallas_call(
        functools.partial(_rwscan_kernel, nch=nch),
        grid=(batch, nc),
        in_specs=[fsp(fwd), fsp(rev)],
        out_specs=[ysp(fwd), ysp(rev)],
        out_shape=[y, y],
        scratch_shapes=[pltpu.VMEM((BW, BW), BF16), pltpu.VMEM((BW, BW), BF16)],
        compiler_params=_cparams(("parallel", "arbitrary")),
        name="rwkv_scan",
    )(f, f)


def _rwpost_kernel(yf_ref, yr_ref, e_ref, lg_ref, lb_ref, o_ref):
    y = yf_ref[...] + yr_ref[...]
    inv = 1.0 / RW_HEAD_DIM
    yc = y - _seg_sum(y, RW_HEAD_DIM) * inv
    var = _seg_sum(yc * yc, RW_HEAD_DIM) * inv
    yn = yc * lax.rsqrt(var + RW_LNX_EPS) * lg_ref[...] + lb_ref[...]
    o_ref[...] = ((yn + e_ref[:, 0:BW]) * e_ref[:, BW:2 * BW]).astype(o_ref.dtype)


def _rwpost(yf, yr, e, lnx_g, lnx_b, *, tm=1024):
    m = yf.shape[0]
    row = lambda z: z.astype(F32).reshape(1, -1)
    ysp = pl.BlockSpec((tm, BW), lambda i: (i, 0))
    return pl.pallas_call(
        _rwpost_kernel,
        grid=(m // tm,),
        in_specs=[ysp, ysp, pl.BlockSpec((tm, 2 * BW), lambda i: (i, 0)),
                  pl.BlockSpec((1, BW), lambda i: (0, 0)), pl.BlockSpec((1, BW), lambda i: (0, 0))],
        out_specs=ysp,
        out_shape=jax.ShapeDtypeStruct((m, BW), BF16),
        compiler_params=_cparams(("parallel",)),
        name="rwkv_post",
    )(yf, yr, e, row(lnx_g), row(lnx_b))


def _df_kernel(lam_ref, sg_ref, q_ref, k_ref, v_ref, g_ref, o_ref, *, lam_init):
    lam = lam_ref[...]
    e1 = jnp.exp(jnp.sum(lam[0:1] * lam[1:2], -1, keepdims=True))
    e2 = jnp.exp(jnp.sum(lam[2:3] * lam[3:4], -1, keepdims=True))
    lam_full = e1 - e2 + lam_init
    d = DF_HEAD_DIM
    tq = q_ref.shape[0]
    qs = _stack_heads(q_ref[...], 2 * DF_HEADS, d)
    k, v = k_ref[...], v_ref[...]
    ss = [_dot_nt(qs[2 * h * tq:2 * (h + 1) * tq], k) for h in range(DF_HEADS)]
    ws, r1s = [], []
    for s in ss:
        e = jnp.exp2(s - jnp.max(s, -1, keepdims=True))
        l = jnp.sum(e, -1, keepdims=True)
        r1 = 1.0 / l[0:tq]
        ws.append((e[0:tq] - e[tq:2 * tq] * (lam_full * l[0:tq] / l[tq:2 * tq])).astype(BF16))
        r1s.append(r1)
    os_ = [jnp.dot(w, v, preferred_element_type=F32) * r1 for w, r1 in zip(ws, r1s)]
    lane = lax.broadcasted_iota(jnp.int32, (1, BW), 1)
    acc = None
    for h, o in enumerate(os_):
        in_head = (lane >= 2 * h * d) & (lane < 2 * (h + 1) * d)
        acc = o if acc is None else jnp.where(in_head, o, acc)
    ms = _seg_sum(acc * acc, 2 * d) * (1.0 / (2 * d))
    o = acc * lax.rsqrt(ms + DF_EPS) * sg_ref[...] * (1.0 - lam_init)
    o_ref[...] = (o * _silu(g_ref[...])).astype(o_ref.dtype)


def _df(cqk, hb, fb, lam, subln_g, lam_init, *, batch, seq, tq=256):
    sg = jnp.tile(subln_g.astype(F32), DF_HEADS).reshape(1, BW)
    return pl.pallas_call(
        functools.partial(_df_kernel, lam_init=lam_init),
        grid=(batch, seq // tq),
        in_specs=[
            pl.BlockSpec((4, DF_HEAD_DIM), lambda b, j: (0, 0)),
            pl.BlockSpec((1, BW), lambda b, j: (0, 0)),
            pl.BlockSpec((None, tq, BW), lambda b, j: (b, j, 0)),
            pl.BlockSpec((None, seq, BW), lambda b, j: (b, 0, 1)),
            pl.BlockSpec((None, seq, BW), lambda b, j: (b, 0, 3)),
            pl.BlockSpec((None, tq, BW), lambda b, j: (b, j, 6)),
        ],
        out_specs=pl.BlockSpec((None, tq, BW), lambda b, j: (b, j, 0)),
        out_shape=jax.ShapeDtypeStruct((batch, seq, BW), BF16),
        compiler_params=_cparams(("parallel", "arbitrary")),
        name="diff_attn",
    )(lam.astype(F32), sg, cqk, cqk, hb, fb)


DL_QB = 2 * DL_HALF
DL_UNROLL = 4


def _dl_group(qkv_ref, kpad, vpad, oacc, lacc, gidx, dil, seq):
    L = seq // dil
    assert L % DL_QB == 0 and (seq // DL_QB) % DL_UNROLL == 0
    nblk = L // DL_QB
    seg = L + 2 * DL_HALF
    zpad = jnp.zeros((DL_HALF, BW), BF16)

    def stage(rho, carry):
        src = pl.ds(pl.multiple_of(rho * L, DL_QB), L)
        krow = pl.multiple_of(rho * seg, DL_HALF)
        for ref in (kpad, vpad):
            ref[pl.ds(krow, DL_HALF), :] = zpad
            ref[pl.ds(krow + DL_HALF + L, DL_HALF), :] = zpad
        for c in range(BW // LANES):
            cols = slice(c * LANES, (c + 1) * LANES)
            kpad[pl.ds(krow + DL_HALF, L), cols] = qkv_ref[2 + c, src, :]
            vpad[pl.ds(krow + DL_HALF, L), cols] = qkv_ref[4 + c, src, :]
        return carry

    lax.fori_loop(0, dil, stage, 0)

    hq = DL_HEADS * DL_QB
    qi = lax.broadcasted_iota(jnp.int32, (hq, 2 * DL_QB), 0) % DL_QB
    ji = lax.broadcasted_iota(jnp.int32, (hq, 2 * DL_QB), 1)
    band = jnp.where(jnp.abs(ji - DL_HALF - qi) <= DL_HALF, 0.0, MASKED)
    jrow = lax.broadcasted_iota(jnp.int32, (1, 2 * DL_QB), 1)
    lane = lax.broadcasted_iota(jnp.int32, (1, BW), 1)

    def blocks(it, carry):
        fs = [it * DL_UNROLL + u for u in range(DL_UNROLL)]
        rhos = [f // nblk for f in fs]
        ns = [f % nblk for f in fs]
        qrows = [pl.ds(pl.multiple_of(f * DL_QB, DL_QB), DL_QB) for f in fs]
        qst = [_stack_heads(jnp.concatenate([qkv_ref[c, r, :] for c in range(BW // LANES)], axis=1),
                            DL_HEADS, DL_HEAD_DIM) for r in qrows]
        krows = [pl.multiple_of(rho * seg + n * DL_QB, DL_HALF) for rho, n in zip(rhos, ns)]
        ss = [_dot_nt(q, kpad[pl.ds(kr, 2 * DL_QB), :]) for q, kr in zip(qst, krows)]
        ps, lses, dens = [], [], []
        for s, n in zip(ss, ns):
            kpos = n * DL_QB - DL_HALF + jrow
            s = s + band + jnp.where((kpos >= 0) & (kpos < L), 0.0, MASKED)
            m = jnp.max(s, -1, keepdims=True)
            p = jnp.exp2(s - m)
            den = jnp.sum(p, -1, keepdims=True)
            ps.append(p.astype(BF16))
            dens.append(den)
            lses.append(m + jnp.log2(den))
        os_ = [jnp.dot(p, vpad[pl.ds(kr, 2 * DL_QB), :], preferred_element_type=F32) / den
               for p, kr, den in zip(ps, krows, dens)]
        for o, lse, rho, n in zip(os_, lses, rhos, ns):
            acc_o = _unstack_heads(o, DL_HEADS, DL_HEAD_DIM)
            acc_l = _unstack_heads(jnp.broadcast_to(lse, (hq, BW)), DL_HEADS, DL_HEAD_DIM)
            if dil > 1:
                dst = pl.ds(rho + dil * n * DL_QB, DL_QB, stride=dil)
            else:
                dst = pl.ds(pl.multiple_of(n * DL_QB, DL_QB), DL_QB)
            for c in range(BW // LANES):
                oacc[gidx, c, dst, :] = acc_o[:, c * LANES:(c + 1) * LANES]
                lacc[gidx, c, dst, :] = acc_l[:, c * LANES:(c + 1) * LANES]
        return carry

    lax.fori_loop(0, seq // DL_QB // DL_UNROLL, blocks, 0)


def _dl_kernel(*refs, seq):
    ng = len(DL_DILATIONS)
    qkv_refs, (g_ref, o_ref, kpad, vpad, oacc, lacc) = refs[:ng], refs[ng:]
    gi = pl.program_id(1)
    for gidx, dil in enumerate(DL_DILATIONS):
        @pl.when(gi == gidx)
        def _(gidx=gidx, dil=dil):
            _dl_group(qkv_refs[gidx], kpad, vpad, oacc, lacc, gidx, dil, seq)

    @pl.when(gi == len(DL_DILATIONS) - 1)
    def _():
        rc = 256

        def chunk(c, carry):
            rows = pl.ds(pl.multiple_of(c * rc, rc), rc)
            for lc in range(BW // LANES):
                cols = slice(lc * LANES, (lc + 1) * LANES)
                ls = [lacc[g, lc, rows, :] for g in range(len(DL_DILATIONS))]
                mx = functools.reduce(jnp.maximum, ls)
                ws = [jnp.exp2(l - mx) for l in ls]
                num = sum(w * oacc[g, lc, rows, :] for g, w in enumerate(ws))
                out = num / sum(ws)
                o_ref[rows, cols] = (out * _silu(g_ref[rows, cols])).astype(o_ref.dtype)
            return carry

        lax.fori_loop(0, seq // rc, chunk, 0)


def _dl(dqkvs, fb, *, batch, seq):
    ng = len(DL_DILATIONS)
    nch = 3 * BW // LANES
    assert all(w // (2 * d) == DL_HALF for w, d in zip(DL_WINDOWS, DL_DILATIONS))
    return pl.pallas_call(
        functools.partial(_dl_kernel, seq=seq),
        grid=(batch, ng),
        in_specs=[pl.BlockSpec((nch, None, seq, LANES), lambda b, g: (0, b, 0, 0))] * ng
        + [pl.BlockSpec((None, seq, BW), lambda b, g: (b, 0, 7))],
        out_specs=pl.BlockSpec((None, seq, BW), lambda b, g: (b, 0, 0)),
        out_shape=jax.ShapeDtypeStruct((batch, seq, BW), BF16),
        scratch_shapes=[
            pltpu.VMEM((seq + 2 * DL_HALF * max(DL_DILATIONS), BW), BF16),
            pltpu.VMEM((seq + 2 * DL_HALF * max(DL_DILATIONS), BW), BF16),
            pltpu.VMEM((ng, BW // LANES, seq, LANES), F32),
            pltpu.VMEM((ng, BW // LANES, seq, LANES), F32),
        ],
        compiler_params=_cparams(("parallel", "arbitrary")),
        name="dilated_attn",
    )(*dqkvs, fb)


_A0 = 0
_B0 = _A0 + 4 * BW
_C0 = _B0 + 4 * BW + 4 * RW_LORA
_D0 = _C0 + 4 * BW
_DG0 = _D0 + 9 * BW
_G0 = _DG0 + BW
PROJ_TM = 2048


def _layer(xf, xb, l, p, tabs, *, batch, seq, last_split):
    w, b = p["w_in"][l], p["b_in"][l]
    cols = lambda lo, n: (w[:, lo:lo + n], b[lo:lo + n])
    cat = lambda parts: (jnp.concatenate([q[0] for q in parts], 1).astype(BF16),
                         jnp.concatenate([q[1] for q in parts], 0).astype(F32))
    scaled = lambda part, s: (part[0] * s, part[1] * s)
    q_scale = NA_HEAD_DIM ** -0.5 * LOG2E
    proj = functools.partial(_proj, xb, seq=seq, tm=PROJ_TM)

    gates = proj(*cat([scaled(cols(_G0, 4 * D_MODEL), 0.5)]), BF16, tn=1024, sigmoid_2x=True,
                 name="proj_gates")
    fb = proj(*cat([cols(_B0, _C0 - _B0), cols(_A0 + 3 * BW, BW), cols(_C0 + 3 * BW, BW),
                    cols(_DG0, BW)]), F32, tn=1024, name="proj_f32")
    hb = proj(*cat([scaled(cols(_A0, BW), q_scale), cols(_A0 + BW, 2 * BW), cols(_C0 + 2 * BW, BW)]),
              BF16, tn=1024, name="proj_bf16")
    cpb = BW // LANES
    cqk = proj(*cat([cols(_C0, 2 * BW)]), BF16, tn=2 * BW, name="proj_rot_c",
               rot=(DF_HEAD_DIM,) + tabs["c"] + ((0,) * cpb + (1,) * cpb,))
    dqs = [proj(*cat([scaled(cols(_D0 + 3 * g * BW, BW), q_scale),
                      cols(_D0 + (3 * g + 1) * BW, 2 * BW)]),
                BF16, tn=3 * BW, residue_dil=dil, name="proj_rot_d",
                rot=(DL_HEAD_DIM,) + tabs["d"] + ((0,) * (2 * cpb) + (None,) * cpb,))
           for g, dil in enumerate(DL_DILATIONS)]

    r3 = lambda z: z.reshape(batch, seq, z.shape[-1])
    hb3, fb3 = r3(hb), r3(fb)
    ya = _na(hb3, fb3, _na_bias_tables(p["na_rpb"][l], seq // GRID_W), batch=batch, seq=seq)
    f, e = _rwprep(fb3, p["rw_mu"][l], p["rw_w0"][l], p["rw_w2"][l], p["rw_a0"][l], p["rw_a2"][l],
                   p["rw_kk"][l], p["rw_ka"][l], p["rw_rk"][l].reshape(-1), batch=batch, seq=seq)
    yf, yr = _rwscan(f, batch=batch, seq=seq)
    m = batch * seq
    yb = _rwpost(yf.reshape(m, BW), yr.reshape(m, BW), e.reshape(m, 2 * BW), p["rw_lnx_g"][l],
                 p["rw_lnx_b"][l])
    lam_init = 0.8 - 0.6 * math.exp(-0.3 * l)
    yc = _df(r3(cqk), hb3, fb3, p["df_lam"][l], p["df_subln_g"][l], lam_init, batch=batch, seq=seq)
    yd = _dl([dq.reshape(dq.shape[0], batch, seq, LANES) for dq in dqs], fb3, batch=batch, seq=seq)

    margs = (ya.reshape(m, BW), yb, yc.reshape(m, BW), yd.reshape(m, BW), gates, xf,
             p["w_branch"][l].astype(BF16), p["w_out"][l].astype(BF16), p["b_out"][l],
             p["ln_g"][l], p["ln_b"][l])
    if last_split is None:
        return _merge(*margs, row0=0, rows=m, with_bf16=True)
    return [_merge(*margs, row0=r0, rows=n, with_bf16=False)[0] for r0, n in last_split]


def kernel(x_prompt, x_sample, ln0_g, ln0_b, w_in, b_in, na_rpb, rw_mu, rw_w0, rw_w2, rw_a0, rw_a2,
           rw_kk, rw_ka, rw_rk, rw_lnx_g, rw_lnx_b, df_lam, df_subln_g, w_branch, w_out, b_out,
           ln_g, ln_b):
    p = dict(w_in=w_in, b_in=b_in, na_rpb=na_rpb, rw_mu=rw_mu, rw_w0=rw_w0, rw_w2=rw_w2,
             rw_a0=rw_a0, rw_a2=rw_a2, rw_kk=rw_kk, rw_ka=rw_ka, rw_rk=rw_rk, rw_lnx_g=rw_lnx_g,
             rw_lnx_b=rw_lnx_b, df_lam=df_lam, df_subln_g=df_subln_g, w_branch=w_branch,
             w_out=w_out, b_out=b_out, ln_g=ln_g, ln_b=ln_b)
    bp, seq, _ = x_prompt.shape
    bs = x_sample.shape[0]
    assert x_sample.shape[1] == seq
    batch = bp + bs
    xf, xb = _ln0(x_prompt.reshape(bp * seq, D_MODEL), x_sample.reshape(bs * seq, D_MODEL),
                  ln0_g, ln0_b)
    tabs = {"c": _rope_tables(seq, DF_HEAD_DIM, (DF_HEAD_DIM ** -0.5 * LOG2E, 1.0)),
            "d": _rope_tables(seq, DL_HEAD_DIM, (1.0,))}
    for l in range(DEPTH - 1):
        xf, xb = _layer(xf, xb, l, p, tabs, batch=batch, seq=seq, last_split=None)
    split = [(0, bp * seq), (bp * seq, bs * seq)]
    yp, ys = _layer(xf, xb, DEPTH - 1, p, tabs, batch=batch, seq=seq, last_split=split)
    return yp.reshape(bp, seq, D_MODEL), ys.reshape(bs, seq, D_MODEL)
```

```python
import functools
import math

import numpy as np
import jax
import jax.numpy as jnp
from jax import lax
from jax.experimental import pallas as pl
from jax.experimental.pallas import tpu as pltpu

F32 = jnp.float32
BF16 = jnp.bfloat16

D_MODEL = 1024
DEPTH = 2
GRID_W = 64
NA_HEADS, NA_HEAD_DIM, NA_ROWS, NA_COLS = 4, 64, 8, 16
RW_HEADS, RW_HEAD_DIM, RW_LORA = 4, 64, 64
RW_LNX_EPS = 64e-5
DF_HEADS, DF_HEAD_DIM, DF_EPS = 4, 32, 1e-5
DL_HEADS, DL_HEAD_DIM = 4, 64
DL_DILATIONS = (1, 4, 16)
DL_WINDOWS = (128, 512, 2048)
DL_HALF = 64
BW = 256
ROPE_THETA = 10000.0
LN_EPS = 1e-5
DEEPNORM_ALPHA = (2 * DEPTH) ** 0.25
LOG2E = math.log2(math.e)
MASKED = -float("inf")

LANES = 128
VMEM_LIMIT = 56 * 1024 * 1024


def _cparams(sem):
    return pltpu.CompilerParams(dimension_semantics=sem, vmem_limit_bytes=VMEM_LIMIT)


def _sigmoid(x):
    return 0.5 * jnp.tanh(0.5 * x) + 0.5


def _silu(x):
    return x * _sigmoid(x)


def _dot_nt(a, b):
    return lax.dot_general(a, b, (((1,), (1,)), ((), ())), preferred_element_type=F32)


def _split2(x):
    hi = x.astype(BF16)
    return hi, (x - hi.astype(F32)).astype(BF16)


def _seg_sum(x, seg):
    n = x.shape[-1]
    r = lax.broadcasted_iota(jnp.int32, (n, n), 0) // seg
    c = lax.broadcasted_iota(jnp.int32, (n, n), 1) // seg
    ones = (r == c).astype(BF16)
    out = None
    for t in _split2(x):
        y = jnp.dot(t, ones, preferred_element_type=F32)
        out = y if out is None else out + y
    return out


def _ln0_kernel(xp_ref, xs_ref, g_ref, b_ref, of_ref, ob_ref, *, n_prompt):
    def norm(x):
        mu = jnp.mean(x, -1, keepdims=True)
        xc = x - mu
        var = jnp.mean(xc * xc, -1, keepdims=True)
        y = xc * lax.rsqrt(var + LN_EPS) * g_ref[...] + b_ref[...]
        of_ref[...] = y
        ob_ref[...] = y.astype(BF16)

    i = pl.program_id(0)

    @pl.when(i < n_prompt)
    def _():
        norm(xp_ref[...])

    @pl.when(i >= n_prompt)
    def _():
        norm(xs_ref[...])


def _ln0(xp, xs, g, b, tm=1024):
    mp, ms = xp.shape[0], xs.shape[0]
    n_p, n_s = mp // tm, ms // tm
    out = jax.ShapeDtypeStruct((mp + ms, D_MODEL), F32)
    outb = jax.ShapeDtypeStruct((mp + ms, D_MODEL), BF16)
    return pl.pallas_call(
        functools.partial(_ln0_kernel, n_prompt=n_p),
        grid=(n_p + n_s,),
        in_specs=[
            pl.BlockSpec((tm, D_MODEL), lambda i: (jnp.minimum(i, n_p - 1), 0)),
            pl.BlockSpec((tm, D_MODEL), lambda i: (jnp.maximum(i - n_p, 0), 0)),
            pl.BlockSpec((1, D_MODEL), lambda i: (0, 0)),
            pl.BlockSpec((1, D_MODEL), lambda i: (0, 0)),
        ],
        out_specs=[pl.BlockSpec((tm, D_MODEL), lambda i: (i, 0)),
                   pl.BlockSpec((tm, D_MODEL), lambda i: (i, 0))],
        out_shape=[out, outb],
        compiler_params=_cparams(("parallel",)),
        name="ln0",
    )(xp, xs, g.reshape(1, -1), b.reshape(1, -1))


def _proj_kernel(x_ref, w_ref, b_ref, *rest, rot_hd, kinds, residue_dil, sigmoid_2x):
    acc = jnp.dot(x_ref[...], w_ref[...], preferred_element_type=F32) + b_ref[...]
    if not rot_hd:
        (o_ref,) = rest
        o_ref[...] = (0.5 * jnp.tanh(acc) + 0.5 if sigmoid_2x else acc).astype(o_ref.dtype)
        return
    cs_ref, sn_ref, o_ref = rest[:3]
    h2 = rot_hd // 2
    lane = lax.broadcasted_iota(jnp.int32, (1, LANES), 1)
    first = (lane % rot_hd) < h2
    for c, kind in enumerate(kinds):
        val = acc[:, c * LANES:(c + 1) * LANES]
        if kind is not None:
            sw = jnp.where(first, pltpu.roll(val, LANES - h2, 1), pltpu.roll(val, h2, 1))
            val = val * cs_ref[kind] + sw * sn_ref[kind]
        if not residue_dil:
            o_ref[:, c * LANES:(c + 1) * LANES] = val.astype(o_ref.dtype)
        elif residue_dil == 1:
            o_ref[c] = val.astype(o_ref.dtype)
        else:
            tmp_ref = rest[3]
            tmp_ref[...] = val
            L = val.shape[0] // residue_dil
            for rho in range(residue_dil):
                o_ref[c, rho * L:(rho + 1) * L, :] = (
                    tmp_ref[pl.ds(rho, L, stride=residue_dil), :].astype(o_ref.dtype))


def _proj(xb, w, b, out_dtype, *, seq, tm, tn, rot=None, residue_dil=0, sigmoid_2x=False,
          name="proj"):
    m, n = xb.shape[0], w.shape[1]
    assert m % tm == 0 and n % tn == 0 and seq % tm == 0
    in_specs = [
        pl.BlockSpec((tm, D_MODEL), lambda i, j: (i, 0)),
        pl.BlockSpec((D_MODEL, tn), lambda i, j: (0, j)),
        pl.BlockSpec((1, tn), lambda i, j: (0, j)),
    ]
    args = [xb, w, b.reshape(1, -1)]
    rot_hd, kinds, scratch = 0, None, []
    if rot is not None:
        rot_hd, cos, sin, kinds = rot
        assert len(kinds) == tn // LANES and LANES % rot_hd == 0
        spt = seq // tm
        tab = pl.BlockSpec((cos.shape[0], tm, LANES), lambda i, j: (0, i % spt, 0))
        in_specs += [tab, tab]
        args += [cos, sin]
    if residue_dil:
        assert rot is not None and tm == seq and seq % residue_dil == 0
        out_spec = pl.BlockSpec((tn // LANES, tm, LANES), lambda i, j: (j, i, 0))
        out_shape = jax.ShapeDtypeStruct((n // LANES, m, LANES), out_dtype)
        if residue_dil > 1:
            scratch = [pltpu.VMEM((tm, LANES), F32)]
    else:
        out_spec = pl.BlockSpec((tm, tn), lambda i, j: (i, j))
        out_shape = jax.ShapeDtypeStruct((m, n), out_dtype)
    return pl.pallas_call(
        functools.partial(_proj_kernel, rot_hd=rot_hd, kinds=kinds, residue_dil=residue_dil,
                          sigmoid_2x=sigmoid_2x),
        grid=(m // tm, n // tn),
        in_specs=in_specs,
        out_specs=out_spec,
        out_shape=out_shape,
        scratch_shapes=scratch,
        compiler_params=_cparams(("parallel", "arbitrary")),
        name=name,
    )(*args)


def _rope_tables(seq, head_dim, scales):
    half = head_dim // 2
    inv_freq = jnp.power(ROPE_THETA, -jnp.arange(half, dtype=F32) / half)
    ang = jnp.arange(seq, dtype=F32)[:, None] * inv_freq[None, :]
    lane = np.arange(LANES)
    f_idx = lane % half
    sign = np.where((lane % head_dim) < half, -1.0, 1.0).astype(np.float32)
    cos = jnp.cos(ang)[:, f_idx]
    sin = jnp.sin(ang)[:, f_idx] * sign[None, :]
    return (jnp.stack([cos * F32(s) for s in scales]), jnp.stack([sin * F32(s) for s in scales]))


MERGE_SLOTS = 3


def _merge_kernel(ya_ref, yb_ref, yc_ref, yd_ref, g_hbm, x_ref, wb_ref, wo_ref, bo_ref, lg_ref,
                  lb_ref, of_ref, *rest, first_block, tm):
    *maybe_ob_ref, gbuf, gsem = rest
    step, n_steps = pl.program_id(0), pl.num_programs(0)

    def gate_copy(s):
        slot = s % MERGE_SLOTS
        rows = pl.ds(pl.multiple_of((s + first_block) * tm, tm), tm)
        return pltpu.make_async_copy(g_hbm.at[rows, :], gbuf.at[slot], gsem.at[slot])

    @pl.when(step == 0)
    def _():
        gate_copy(step).start()

        @pl.when(n_steps > 1)
        def _():
            gate_copy(step + 1).start()

    @pl.when(step + 2 < n_steps)
    def _():
        gate_copy(step + 2).start()

    gate_copy(step).wait()
    slot = step % MERGE_SLOTS
    merged = None
    for i, y_ref in enumerate((ya_ref, yb_ref, yc_ref, yd_ref)):
        p = jnp.dot(y_ref[...], wb_ref[i], preferred_element_type=F32)
        gate = gbuf[slot, :, i * D_MODEL:(i + 1) * D_MODEL].astype(F32)
        merged = gate * p if merged is None else merged + gate * p
    y = jnp.dot(merged.astype(BF16), wo_ref[...], preferred_element_type=F32) + bo_ref[...]
    z = DEEPNORM_ALPHA * x_ref[...] + y
    mu = jnp.mean(z, -1, keepdims=True)
    zc = z - mu
    var = jnp.mean(zc * zc, -1, keepdims=True)
    out = zc * lax.rsqrt(var + LN_EPS) * lg_ref[...] + lb_ref[...]
    of_ref[...] = out
    for ob_ref in maybe_ob_ref:
        ob_ref[...] = out.astype(BF16)


def _merge(ya, yb, yc, yd, gates, x, wb, wo, bo, lg, lb, *, row0, rows, with_bf16, tm=512):
    outs = [F32, BF16] if with_bf16 else [F32]
    assert row0 % tm == 0 and rows % tm == 0
    o = row0 // tm
    ysp = pl.BlockSpec((tm, BW), lambda i: (i + o, 0))
    const2 = lambda i: (0, 0)
    return pl.pallas_call(
        functools.partial(_merge_kernel, first_block=o, tm=tm),
        grid=(rows // tm,),
        in_specs=[ysp, ysp, ysp, ysp,
                  pl.BlockSpec(memory_space=pl.ANY),
                  pl.BlockSpec((tm, D_MODEL), lambda i: (i + o, 0)),
                  pl.BlockSpec((4, BW, D_MODEL), lambda i: (0, 0, 0)),
                  pl.BlockSpec((D_MODEL, D_MODEL), const2),
                  pl.BlockSpec((1, D_MODEL), const2),
                  pl.BlockSpec((1, D_MODEL), const2),
                  pl.BlockSpec((1, D_MODEL), const2)],
        out_specs=[pl.BlockSpec((tm, D_MODEL), lambda i: (i, 0)) for _ in outs],
        out_shape=[jax.ShapeDtypeStruct((rows, D_MODEL), dt) for dt in outs],
        scratch_shapes=[pltpu.VMEM((MERGE_SLOTS, tm, 4 * D_MODEL), BF16),
                        pltpu.SemaphoreType.DMA((MERGE_SLOTS,))],
        compiler_params=_cparams(("arbitrary",)),
        name="merge",
    )(ya, yb, yc, yd, gates, x, wb, wo, bo.reshape(1, -1), lg.reshape(1, -1), lb.reshape(1, -1))


NA_QR = 4


def _na_geometry(rows):
    kr, wr = min(NA_ROWS, rows), min(NA_ROWS, rows) + NA_QR - 1
    assert rows % NA_QR == 0 and rows >= wr and kr == NA_ROWS
    steps = rows // NA_QR
    a = np.arange(NA_QR)[:, None]
    j = np.arange(wr)[None, :]
    pats = []
    for g in range(steps):
        r = NA_QR * g + a
        rs = np.clip(r - kr // 2, 0, rows - kr)
        ws = int(np.clip(NA_QR * g - kr // 2, 0, rows - wr))
        assert rs.min() >= ws and rs.max() + kr <= ws + wr
        valid = (ws + j >= rs) & (ws + j < rs + kr)
        dr = np.clip(ws + j - r + NA_ROWS - 1, 0, 2 * NA_ROWS - 2)
        pats.append((valid, np.where(valid, dr, 0)))
    same = lambda x, y: np.array_equal(x[0], y[0]) and np.array_equal(x[1], y[1])
    assert steps >= 3 and all(same(pats[g], pats[1]) for g in range(1, steps - 1))
    return kr, wr, steps, [pats[0], pats[1], pats[steps - 1]]


def _na_bias_tables(rpb, rows):
    kr, wr, steps, pats = _na_geometry(rows)
    qc = np.arange(GRID_W)[:, None]
    kc = np.arange(GRID_W)[None, :]
    c_start = np.clip(qc - NA_COLS // 2, 0, GRID_W - NA_COLS)
    col_ok = (kc >= c_start) & (kc < c_start + NA_COLS)
    dc = np.clip(kc - qc + NA_COLS - 1, 0, 2 * NA_COLS - 2)
    tabs = []
    by_col = rpb.astype(F32)[:, :, dc]
    for valid, dr in pats:
        b = by_col[:, dr] * LOG2E
        ok = valid[:, :, None, None] & col_ok[None, None]
        b = jnp.where(ok[None], b, MASKED)
        tabs.append(b.transpose(0, 1, 3, 2, 4).reshape(NA_HEADS, NA_QR * GRID_W, wr * GRID_W))
    return jnp.stack(tabs, axis=1)


def _stack_heads(q, n_heads, head_dim):
    lane = lax.broadcasted_iota(jnp.int32, (1, q.shape[1]), 1)
    zero = jnp.zeros_like(q)
    return jnp.concatenate(
        [jnp.where((lane >= h * head_dim) & (lane < (h + 1) * head_dim), q, zero)
         for h in range(n_heads)], axis=0)


def _unstack_heads(o, n_heads, head_dim):
    m = o.shape[0] // n_heads
    lane = lax.broadcasted_iota(jnp.int32, (1, o.shape[1]), 1)
    acc = o[0:m]
    for h in range(1, n_heads):
        in_head = (lane >= h * head_dim) & (lane < (h + 1) * head_dim)
        acc = jnp.where(in_head, o[h * m:(h + 1) * m], acc)
    return acc


def _na_kernel(q_ref, k_ref, v_ref, g_ref, bias_ref, o_ref, *, rows):
    kr, wr = min(NA_ROWS, rows), min(NA_ROWS, rows) + NA_QR - 1
    g = pl.program_id(1)
    ws = jnp.clip(NA_QR * g - kr // 2, 0, rows - wr)
    start = pl.multiple_of(ws * GRID_W, GRID_W)
    k = k_ref[pl.ds(start, wr * GRID_W), :]
    v = v_ref[pl.ds(start, wr * GRID_W), :]
    qs = _stack_heads(q_ref[...], NA_HEADS, NA_HEAD_DIM)
    m_rows = q_ref.shape[0]
    parts = [slice(h * m_rows, (h + 1) * m_rows) for h in range(NA_HEADS)]
    ss = [_dot_nt(qs[h], k) for h in parts]
    ps, ls = [], []
    for i, s in enumerate(ss):
        s = s + bias_ref[i]
        p = jnp.exp2(s - jnp.max(s, -1, keepdims=True))
        ls.append(jnp.sum(p, -1, keepdims=True))
        ps.append(p.astype(BF16))
    o = jnp.concatenate([jnp.dot(p, v, preferred_element_type=F32) / l for p, l in zip(ps, ls)], 0)
    o = _unstack_heads(o, NA_HEADS, NA_HEAD_DIM)
    o_ref[...] = (o * _silu(g_ref[...])).astype(o_ref.dtype)


def _na(hb, fb, bias_tab, *, batch, seq):
    rows = seq // GRID_W
    kr, wr, steps, _ = _na_geometry(rows)
    qb = NA_QR * GRID_W

    def pattern_of(g):
        return jnp.where(g == 0, 0, jnp.where(g == steps - 1, 2, 1))

    return pl.pallas_call(
        functools.partial(_na_kernel, rows=rows),
        grid=(batch, steps),
        in_specs=[
            pl.BlockSpec((None, qb, BW), lambda b, g: (b, g, 0)),
            pl.BlockSpec((None, seq, BW), lambda b, g: (b, 0, 1)),
            pl.BlockSpec((None, seq, BW), lambda b, g: (b, 0, 2)),
            pl.BlockSpec((None, qb, BW), lambda b, g: (b, g, 5)),
            pl.BlockSpec((NA_HEADS, None, qb, wr * GRID_W), lambda b, g: (0, pattern_of(g), 0, 0)),
        ],
        out_specs=pl.BlockSpec((None, qb, BW), lambda b, g: (b, g, 0)),
        out_shape=jax.ShapeDtypeStruct((batch, seq, BW), BF16),
        compiler_params=_cparams(("parallel", "arbitrary")),
        name="na_attn",
    )(hb, hb, hb, fb, bias_tab)


RW_C = 64
RW_FIELDS = 9
RW_INV_BASE = 4


def _mm(a, b, mode="nn", passes=1):
    dims = {"nn": (((1,), (0,)), ((), ())), "nt": (((1,), (1,)), ((), ()))}[mode]
    dg = lambda x, y: lax.dot_general(x, y, dims, preferred_element_type=F32)
    ah, bh = a.astype(BF16), b.astype(BF16)
    out = dg(ah, bh)
    if passes == 3:
        al = (a - ah.astype(F32)).astype(BF16)
        bl = (b - bh.astype(F32)).astype(BF16)
        out = out + (dg(al, bh) + dg(ah, bl))
    return out


def _mm_bd(a, b, mode="nn"):
    halves = [slice(i * LANES, (i + 1) * LANES) for i in range(BW // LANES)]
    return jnp.concatenate([_mm(a[:, h], b[h, h], mode) for h in halves], axis=1)


def _rwprep_kernel(x_ref, prev_ref, next_ref, mu_ref, w0_ref, w2_ref, a0_ref, a2_ref, kk_ref, ka_ref,
                   rk_ref, f_ref, e_ref, *, rc):
    c = pl.program_id(1)
    lane = lax.broadcasted_iota(jnp.int32, (1, 2 * RW_LORA), 1)
    ri = lax.broadcasted_iota(jnp.int32, (rc, 1), 0)
    x = x_ref[...]
    prev_row = jnp.where(c > 0, prev_ref[7:8, :], 0.0)
    next_row = jnp.where(c < pl.num_programs(1) - 1, next_ref[0:1, :], 0.0)
    up = jnp.where(ri == 0, prev_row, pltpu.roll(x, 1, 0))
    dn = jnp.where(ri == rc - 1, next_row, pltpu.roll(x, rc - 1, 0))
    mu = mu_ref[...]
    xs = x * (1.0 - mu) + (0.5 * mu) * (up + dn)
    r, k, v, g = (xs[:, i * BW:(i + 1) * BW] for i in range(4))
    wl = xs[:, 4 * BW:4 * BW + 2 * RW_LORA]
    al = xs[:, 4 * BW + 2 * RW_LORA:4 * BW + 4 * RW_LORA]
    kk = k * kk_ref[...]
    kap = kk * lax.rsqrt(jnp.maximum(_seg_sum(kk * kk, RW_HEAD_DIM), 1e-24))
    tw = jnp.tanh(wl)
    k_scaled = k * ka_ref[...]
    k_fixed = k * (1.0 - ka_ref[...])
    kd_sum = None
    for d in range(2):
        dm = ((lane >= d * RW_LORA) & (lane < (d + 1) * RW_LORA)).astype(F32)
        w_raw = w0_ref[d:d + 1, :] + _mm(tw * dm, w2_ref[...], passes=3)
        lw = _sigmoid(w_raw) * (-math.exp(-0.5))
        a = _sigmoid(a0_ref[d:d + 1, :] + _mm(al * dm, a2_ref[...], passes=3))
        kd = k_fixed + k_scaled * a
        f_ref[:, (3 + 3 * d) * BW:(4 + 3 * d) * BW] = lw
        f_ref[:, (4 + 3 * d) * BW:(5 + 3 * d) * BW] = kd
        f_ref[:, (5 + 3 * d) * BW:(6 + 3 * d) * BW] = kap * a
        kd_sum = kd if kd_sum is None else kd_sum + kd
    f_ref[:, 0:BW] = r
    f_ref[:, BW:2 * BW] = v
    f_ref[:, 2 * BW:3 * BW] = kap
    bonus = _seg_sum(r * kd_sum * rk_ref[...], RW_HEAD_DIM)
    e_ref[:, 0:BW] = bonus * v
    e_ref[:, BW:2 * BW] = _silu(g)


def _rwprep(fb, mu, w0, w2, a0, a2, k_k, k_a, r_k, *, batch, seq, rc=256):
    ucols = 4 * BW + 4 * RW_LORA
    row = lambda z: z.astype(F32).reshape(1, -1)
    c2 = lambda b, c: (0, 0)
    tpb = rc // 8
    return pl.pallas_call(
        functools.partial(_rwprep_kernel, rc=rc),
        grid=(batch, seq // rc),
        in_specs=[
            pl.BlockSpec((None, rc, ucols), lambda b, c: (b, c, 0)),
            pl.BlockSpec((None, 8, ucols), lambda b, c: (b, jnp.maximum(c * tpb - 1, 0), 0)),
            pl.BlockSpec((None, 8, ucols),
                         lambda b, c: (b, jnp.minimum((c + 1) * tpb, seq // 8 - 1), 0)),
            pl.BlockSpec((1, ucols), c2),
            pl.BlockSpec((2, BW), c2),
            pl.BlockSpec((2 * RW_LORA, BW), c2),
            pl.BlockSpec((2, BW), c2),
            pl.BlockSpec((2 * RW_LORA, BW), c2),
            pl.BlockSpec((1, BW), c2),
            pl.BlockSpec((1, BW), c2),
            pl.BlockSpec((1, BW), c2),
        ],
        out_specs=[pl.BlockSpec((None, rc, RW_FIELDS * BW), lambda b, c: (b, c, 0)),
                   pl.BlockSpec((None, rc, 2 * BW), lambda b, c: (b, c, 0))],
        out_shape=[jax.ShapeDtypeStruct((batch, seq, RW_FIELDS * BW), F32),
                   jax.ShapeDtypeStruct((batch, seq, 2 * BW), F32)],
        compiler_params=_cparams(("parallel", "arbitrary")),
        name="rwkv_prep",
    )(fb, fb, fb, row(mu), w0.astype(F32), w2.astype(F32).reshape(2 * RW_LORA, BW), a0.astype(F32),
      a2.astype(F32).reshape(2 * RW_LORA, BW), row(k_k), row(k_a), row(r_k))


def _rw_bd(x):
    lane = lax.broadcasted_iota(jnp.int32, (1, BW), 1)
    xb = x.astype(BF16)
    zero = jnp.zeros_like(xb)
    return jnp.concatenate(
        [jnp.where((lane >= h * RW_C) & (lane < (h + 1) * RW_C), xb, zero) for h in range(RW_HEADS)],
        axis=0)


def _rw_advance(terms, z):
    gm, hv, nt = terms
    both = _mm_bd(gm, z)
    return both[0:RW_C] + hv, _rw_bd(both[RW_C:2 * RW_C] + nt)


def _rw_prepare(specs):
    C, H = RW_C, RW_HEADS
    assert H * C == BW and RW_HEAD_DIM == C
    bd = _rw_bd
    cat = lambda a, b: jnp.concatenate([a, b], axis=0)
    ri = lax.broadcasted_iota(jnp.int32, (C, C), 0)
    ci = lax.broadcasted_iota(jnp.int32, (C, C), 1)
    tris = [(ci <= ri).astype(BF16), (ci >= ri).astype(BF16)]
    row = lax.broadcasted_iota(jnp.int32, (C, BW), 0)
    col = lax.broadcasted_iota(jnp.int32, (C, BW), 1) % C
    eye = col == row
    strict = [col < row, col > row]
    incl = [col <= row, col >= row]
    same_block = lambda s: (row // s) == (col // s)

    def lc_t(x):
        t = bd(x).astype(F32).T
        return t[0:C] + t[C:2 * C] + t[2 * C:3 * C] + t[3 * C:4 * C]

    def load(spec):
        f_ref, r0, d = spec
        fld = lambda i: f_ref[r0:r0 + C, i * BW:(i + 1) * BW]
        c = dict(d=d, r=fld(0), v=fld(1), kap=fld(2), lw=fld(3 + 3 * d), kd=fld(4 + 3 * d),
                 beta=fld(5 + 3 * d))
        cs = None
        for t in _split2(c["lw"]):
            y = jnp.dot(tris[d], t, preferred_element_type=F32)
            cs = y if cs is None else cs + y
        c["cs"] = cs
        return c

    def scale(c):
        cs, d = c["cs"], c["d"]
        tot = cs[0:1] if d == 1 else cs[C - 1:C]
        ginv = jnp.exp(-cs)
        gto = jnp.exp(tot - cs)
        kt = c["kap"] * jnp.exp(cs - c["lw"])
        rt = c["r"] * jnp.exp(cs)
        c.update(gc=jnp.exp(tot), rt=rt, ktrt=cat(kt, rt), kt_b=bd(kt), v_b=bd(c["v"]),
                 kg_b=bd(c["kd"] * ginv), bg_b=bd(c["beta"] * ginv),
                 kh=c["kd"] * gto, bh=c["beta"] * gto)
        return c

    def gram(c):
        d = c["d"]
        gk = _mm_bd(c["ktrt"], c["kg_b"], "nt")
        gb = _mm_bd(c["ktrt"], c["bg_b"], "nt")
        l_bk = jnp.where(strict[d], gb[0:C], 0.0)
        c.update(l_kk=jnp.where(strict[d], gk[0:C], 0.0), l_kr=jnp.where(incl[d], gk[C:2 * C], 0.0),
                 l_br=jnp.where(incl[d], gb[C:2 * C], 0.0), l_bk=l_bk,
                 nbase=-jnp.where(same_block(RW_INV_BASE), l_bk, 0.0))
        return c

    def base_square(c):
        c["nsq"] = _mm_bd(c["nbase"], bd(c["nbase"]))
        return c

    def base_inverse(c):
        w = jnp.where(eye, 1.0, 0.0) + c.pop("nbase")
        c["w"] = w + _mm_bd(w, bd(c.pop("nsq")))
        return c

    def merge_left(c, s):
        off = same_block(2 * s) & jnp.logical_not(same_block(s))
        c["t"] = _mm_bd(c["w"], bd(jnp.where(off, c["l_bk"], 0.0)))
        return c

    def merge_right(c):
        c["w"] = c["w"] - _mm_bd(c.pop("t"), bd(c["w"]))
        return c

    def solve(c):
        c["p1_b"] = bd(_mm_bd(c["w"], c["kt_b"]))
        c["p2"] = _mm_bd(c["w"], bd(c["l_kk"]))
        c["p2_b"] = bd(c["p2"])
        return c

    def outputs(c):
        both = _mm_bd(cat(c["l_br"], c["bh_t"]), c["p1_b"])
        c["gm"] = cat(c["rt"] - both[0:C], jnp.where(eye, c["gc"], 0.0) - both[C:2 * C])
        hm = c["l_kr"] - _mm_bd(c["l_br"], c["p2_b"])
        three = _mm_bd(jnp.concatenate([hm, c["p2"], c["kh_t"]], axis=0), c["v_b"])
        c["hv"], c["p2v_b"], c["khv"] = three[0:C], bd(three[C:2 * C]), three[2 * C:3 * C]
        return c

    def state(c):
        nt = c["khv"] - _mm_bd(c["bh_t"], c["p2v_b"])
        return c["gm"].astype(BF16), c["hv"], nt

    def transposes(c):
        c["kh_t"], c["bh_t"] = lc_t(c.pop("kh")), lc_t(c.pop("bh"))
        return c

    cs = [load(s) for s in specs]
    cs = [scale(c) for c in cs]
    cs = [gram(c) for c in cs]
    cs = [transposes(c) for c in cs]
    cs = [base_square(c) for c in cs]
    cs = [base_inverse(c) for c in cs]
    s = RW_INV_BASE
    while s < C:
        cs = [merge_left(c, s) for c in cs]
        cs = [merge_right(c) for c in cs]
        s *= 2
    cs = [solve(c) for c in cs]
    cs = [outputs(c) for c in cs]
    return [state(c) for c in cs]


def _rwscan_kernel(ff_ref, fr_ref, yf_ref, yr_ref, zf_ref, zr_ref, *, nch):
    @pl.when(pl.program_id(1) == 0)
    def _():
        zf_ref[...] = jnp.zeros_like(zf_ref)
        zr_ref[...] = jnp.zeros_like(zr_ref)

    C = RW_C
    pre = _rw_prepare([(ff_ref, j * C, 0) for j in range(nch)]
                      + [(fr_ref, j * C, 1) for j in range(nch)])
    pre_f, pre_r = pre[:nch], pre[nch:]
    zf, zr = zf_ref[...], zr_ref[...]
    for j in range(nch):
        yf_ref[j * C:(j + 1) * C, :], zf = _rw_advance(pre_f[j], zf)
        jr = nch - 1 - j
        yr_ref[jr * C:(jr + 1) * C, :], zr = _rw_advance(pre_r[jr], zr)
    zf_ref[...] = zf
    zr_ref[...] = zr


def _rwscan(f, *, batch, seq, nch=4):
    rb = nch * RW_C
    nc = seq // rb
    fsp = lambda imap: pl.BlockSpec((None, rb, RW_FIELDS * BW), imap)
    ysp = lambda imap: pl.BlockSpec((None, rb, BW), imap)
    fwd = lambda b, s: (b, s, 0)
    rev = lambda b, s: (b, nc - 1 - s, 0)
    y = jax.ShapeDtypeStruct((batch, seq, BW), F32)
    return pl.pallas_call(
        functools.partial(_rwscan_kernel, nch=nch),
        grid=(batch, nc),
        in_specs=[fsp(fwd), fsp(rev)],
        out_specs=[ysp(fwd), ysp(rev)],
        out_shape=[y, y],
        scratch_shapes=[pltpu.VMEM((BW, BW), BF16), pltpu.VMEM((BW, BW), BF16)],
        compiler_params=_cparams(("parallel", "arbitrary")),
        name="rwkv_scan",
    )(f, f)


def _rwpost_kernel(yf_ref, yr_ref, e_ref, lg_ref, lb_ref, o_ref):
    y = yf_ref[...] + yr_ref[...]
    inv = 1.0 / RW_HEAD_DIM
    yc = y - _seg_sum(y, RW_HEAD_DIM) * inv
    var = _seg_sum(yc * yc, RW_HEAD_DIM) * inv
    yn = yc * lax.rsqrt(var + RW_LNX_EPS) * lg_ref[...] + lb_ref[...]
    o_ref[...] = ((yn + e_ref[:, 0:BW]) * e_ref[:, BW:2 * BW]).astype(o_ref.dtype)


def _rwpost(yf, yr, e, lnx_g, lnx_b, *, tm=1024):
    m = yf.shape[0]
    row = lambda z: z.astype(F32).reshape(1, -1)
    ysp = pl.BlockSpec((tm, BW), lambda i: (i, 0))
    return pl.pallas_call(
        _rwpost_kernel,
        grid=(m // tm,),
        in_specs=[ysp, ysp, pl.BlockSpec((tm, 2 * BW), lambda i: (i, 0)),
                  pl.BlockSpec((1, BW), lambda i: (0, 0)), pl.BlockSpec((1, BW), lambda i: (0, 0))],
        out_specs=ysp,
        out_shape=jax.ShapeDtypeStruct((m, BW), BF16),
        compiler_params=_cparams(("parallel",)),
        name="rwkv_post",
    )(yf, yr, e, row(lnx_g), row(lnx_b))


def _df_kernel(lam_ref, sg_ref, q_ref, k_ref, v_ref, g_ref, o_ref, *, lam_init):
    lam = lam_ref[...]
    e1 = jnp.exp(jnp.sum(lam[0:1] * lam[1:2], -1, keepdims=True))
    e2 = jnp.exp(jnp.sum(lam[2:3] * lam[3:4], -1, keepdims=True))
    lam_full = e1 - e2 + lam_init
    d = DF_HEAD_DIM
    tq = q_ref.shape[0]
    qs = _stack_heads(q_ref[...], 2 * DF_HEADS, d)
    k, v = k_ref[...], v_ref[...]
    ss = [_dot_nt(qs[2 * h * tq:2 * (h + 1) * tq], k) for h in range(DF_HEADS)]
    ws, r1s = [], []
    for s in ss:
        e = jnp.exp2(s - jnp.max(s, -1, keepdims=True))
        l = jnp.sum(e, -1, keepdims=True)
        r1 = 1.0 / l[0:tq]
        ws.append((e[0:tq] - e[tq:2 * tq] * (lam_full * l[0:tq] / l[tq:2 * tq])).astype(BF16))
        r1s.append(r1)
    os_ = [jnp.dot(w, v, preferred_element_type=F32) * r1 for w, r1 in zip(ws, r1s)]
    lane = lax.broadcasted_iota(jnp.int32, (1, BW), 1)
    acc = None
    for h, o in enumerate(os_):
        in_head = (lane >= 2 * h * d) & (lane < 2 * (h + 1) * d)
        acc = o if acc is None else jnp.where(in_head, o, acc)
    ms = _seg_sum(acc * acc, 2 * d) * (1.0 / (2 * d))
    o = acc * lax.rsqrt(ms + DF_EPS) * sg_ref[...] * (1.0 - lam_init)
    o_ref[...] = (o * _silu(g_ref[...])).astype(o_ref.dtype)


def _df(cqk, hb, fb, lam, subln_g, lam_init, *, batch, seq, tq=256):
    sg = jnp.tile(subln_g.astype(F32), DF_HEADS).reshape(1, BW)
    return pl.pallas_call(
        functools.partial(_df_kernel, lam_init=lam_init),
        grid=(batch, seq // tq),
        in_specs=[
            pl.BlockSpec((4, DF_HEAD_DIM), lambda b, j: (0, 0)),
            pl.BlockSpec((1, BW), lambda b, j: (0, 0)),
            pl.BlockSpec((None, tq, BW), lambda b, j: (b, j, 0)),
            pl.BlockSpec((None, seq, BW), lambda b, j: (b, 0, 1)),
            pl.BlockSpec((None, seq, BW), lambda b, j: (b, 0, 3)),
            pl.BlockSpec((None, tq, BW), lambda b, j: (b, j, 6)),
        ],
        out_specs=pl.BlockSpec((None, tq, BW), lambda b, j: (b, j, 0)),
        out_shape=jax.ShapeDtypeStruct((batch, seq, BW), BF16),
        compiler_params=_cparams(("parallel", "arbitrary")),
        name="diff_attn",
    )(lam.astype(F32), sg, cqk, cqk, hb, fb)


DL_QB = 2 * DL_HALF
DL_UNROLL = 4


def _dl_group(qkv_ref, kpad, vpad, oacc, lacc, gidx, dil, seq):
    L = seq // dil
    assert L % DL_QB == 0 and (seq // DL_QB) % DL_UNROLL == 0
    nblk = L // DL_QB
    seg = L + 2 * DL_HALF
    zpad = jnp.zeros((DL_HALF, BW), BF16)

    def stage(rho, carry):
        src = pl.ds(pl.multiple_of(rho * L, DL_QB), L)
        krow = pl.multiple_of(rho * seg, DL_HALF)
        for ref in (kpad, vpad):
            ref[pl.ds(krow, DL_HALF), :] = zpad
            ref[pl.ds(krow + DL_HALF + L, DL_HALF), :] = zpad
        for c in range(BW // LANES):
            cols = slice(c * LANES, (c + 1) * LANES)
            kpad[pl.ds(krow + DL_HALF, L), cols] = qkv_ref[2 + c, src, :]
            vpad[pl.ds(krow + DL_HALF, L), cols] = qkv_ref[4 + c, src, :]
        return carry

    lax.fori_loop(0, dil, stage, 0)

    hq = DL_HEADS * DL_QB
    qi = lax.broadcasted_iota(jnp.int32, (hq, 2 * DL_QB), 0) % DL_QB
    ji = lax.broadcasted_iota(jnp.int32, (hq, 2 * DL_QB), 1)
    band = jnp.where(jnp.abs(ji - DL_HALF - qi) <= DL_HALF, 0.0, MASKED)
    jrow = lax.broadcasted_iota(jnp.int32, (1, 2 * DL_QB), 1)
    lane = lax.broadcasted_iota(jnp.int32, (1, BW), 1)

    def blocks(it, carry):
        fs = [it * DL_UNROLL + u for u in range(DL_UNROLL)]
        rhos = [f // nblk for f in fs]
        ns = [f % nblk for f in fs]
        qrows = [pl.ds(pl.multiple_of(f * DL_QB, DL_QB), DL_QB) for f in fs]
        qst = [_stack_heads(jnp.concatenate([qkv_ref[c, r, :] for c in range(BW // LANES)], axis=1),
                            DL_HEADS, DL_HEAD_DIM) for r in qrows]
        krows = [pl.multiple_of(rho * seg + n * DL_QB, DL_HALF) for rho, n in zip(rhos, ns)]
        ss = [_dot_nt(q, kpad[pl.ds(kr, 2 * DL_QB), :]) for q, kr in zip(qst, krows)]
        ps, lses, dens = [], [], []
        for s, n in zip(ss, ns):
            kpos = n * DL_QB - DL_HALF + jrow
            s = s + band + jnp.where((kpos >= 0) & (kpos < L), 0.0, MASKED)
            m = jnp.max(s, -1, keepdims=True)
            p = jnp.exp2(s - m)
            den = jnp.sum(p, -1, keepdims=True)
            ps.append(p.astype(BF16))
            dens.append(den)
            lses.append(m + jnp.log2(den))
        os_ = [jnp.dot(p, vpad[pl.ds(kr, 2 * DL_QB), :], preferred_element_type=F32) / den
               for p, kr, den in zip(ps, krows, dens)]
        for o, lse, rho, n in zip(os_, lses, rhos, ns):
            acc_o = _unstack_heads(o, DL_HEADS, DL_HEAD_DIM)
            acc_l = _unstack_heads(jnp.broadcast_to(lse, (hq, BW)), DL_HEADS, DL_HEAD_DIM)
            if dil > 1:
                dst = pl.ds(rho + dil * n * DL_QB, DL_QB, stride=dil)
            else:
                dst = pl.ds(pl.multiple_of(n * DL_QB, DL_QB), DL_QB)
            for c in range(BW // LANES):
                oacc[gidx, c, dst, :] = acc_o[:, c * LANES:(c + 1) * LANES]
                lacc[gidx, c, dst, :] = acc_l[:, c * LANES:(c + 1) * LANES]
        return carry

    lax.fori_loop(0, seq // DL_QB // DL_UNROLL, blocks, 0)


def _dl_kernel(*refs, seq):
    ng = len(DL_DILATIONS)
    qkv_refs, (g_ref, o_ref, kpad, vpad, oacc, lacc) = refs[:ng], refs[ng:]
    gi = pl.program_id(1)
    for gidx, dil in enumerate(DL_DILATIONS):
        @pl.when(gi == gidx)
        def _(gidx=gidx, dil=dil):
            _dl_group(qkv_refs[gidx], kpad, vpad, oacc, lacc, gidx, dil, seq)

    @pl.when(gi == len(DL_DILATIONS) - 1)
    def _():
        rc = 256

        def chunk(c, carry):
            rows = pl.ds(pl.multiple_of(c * rc, rc), rc)
            for lc in range(BW // LANES):
                cols = slice(lc * LANES, (lc + 1) * LANES)
                ls = [lacc[g, lc, rows, :] for g in range(len(DL_DILATIONS))]
                mx = functools.reduce(jnp.maximum, ls)
                ws = [jnp.exp2(l - mx) for l in ls]
                num = sum(w * oacc[g, lc, rows, :] for g, w in enumerate(ws))
                out = num / sum(ws)
                o_ref[rows, cols] = (out * _silu(g_ref[rows, cols])).astype(o_ref.dtype)
            return carry

        lax.fori_loop(0, seq // rc, chunk, 0)


def _dl(dqkvs, fb, *, batch, seq):
    ng = len(DL_DILATIONS)
    nch = 3 * BW // LANES
    assert all(w // (2 * d) == DL_HALF for w, d in zip(DL_WINDOWS, DL_DILATIONS))
    return pl.pallas_call(
        functools.partial(_dl_kernel, seq=seq),
        grid=(batch, ng),
        in_specs=[pl.BlockSpec((nch, None, seq, LANES), lambda b, g: (0, b, 0, 0))] * ng
        + [pl.BlockSpec((None, seq, BW), lambda b, g: (b, 0, 7))],
        out_specs=pl.BlockSpec((None, seq, BW), lambda b, g: (b, 0, 0)),
        out_shape=jax.ShapeDtypeStruct((batch, seq, BW), BF16),
        scratch_shapes=[
            pltpu.VMEM((seq + 2 * DL_HALF * max(DL_DILATIONS), BW), BF16),
            pltpu.VMEM((seq + 2 * DL_HALF * max(DL_DILATIONS), BW), BF16),
            pltpu.VMEM((ng, BW // LANES, seq, LANES), F32),
            pltpu.VMEM((ng, BW // LANES, seq, LANES), F32),
        ],
        compiler_params=_cparams(("parallel", "arbitrary")),
        name="dilated_attn",
    )(*dqkvs, fb)


_A0 = 0
_B0 = _A0 + 4 * BW
_C0 = _B0 + 4 * BW + 4 * RW_LORA
_D0 = _C0 + 4 * BW
_DG0 = _D0 + 9 * BW
_G0 = _DG0 + BW
PROJ_TM = 2048


def _layer(xf, xb, l, p, tabs, *, batch, seq, last_split):
    w, b = p["w_in"][l], p["b_in"][l]
    cols = lambda lo, n: (w[:, lo:lo + n], b[lo:lo + n])
    cat = lambda parts: (jnp.concatenate([q[0] for q in parts], 1).astype(BF16),
                         jnp.concatenate([q[1] for q in parts], 0).astype(F32))
    scaled = lambda part, s: (part[0] * s, part[1] * s)
    q_scale = NA_HEAD_DIM ** -0.5 * LOG2E
    proj = functools.partial(_proj, xb, seq=seq, tm=PROJ_TM)

    gates = proj(*cat([scaled(cols(_G0, 4 * D_MODEL), 0.5)]), BF16, tn=1024, sigmoid_2x=True,
                 name="proj_gates")
    fb = proj(*cat([cols(_B0, _C0 - _B0), cols(_A0 + 3 * BW, BW), cols(_C0 + 3 * BW, BW),
                    cols(_DG0, BW)]), F32, tn=1024, name="proj_f32")
    hb = proj(*cat([scaled(cols(_A0, BW), q_scale), cols(_A0 + BW, 2 * BW), cols(_C0 + 2 * BW, BW)]),
              BF16, tn=1024, name="proj_bf16")
    cpb = BW // LANES
    cqk = proj(*cat([cols(_C0, 2 * BW)]), BF16, tn=2 * BW, name="proj_rot_c",
               rot=(DF_HEAD_DIM,) + tabs["c"] + ((0,) * cpb + (1,) * cpb,))
    dqs = [proj(*cat([scaled(cols(_D0 + 3 * g * BW, BW), q_scale),
                      cols(_D0 + (3 * g + 1) * BW, 2 * BW)]),
                BF16, tn=3 * BW, residue_dil=dil, name="proj_rot_d",
                rot=(DL_HEAD_DIM,) + tabs["d"] + ((0,) * (2 * cpb) + (None,) * cpb,))
           for g, dil in enumerate(DL_DILATIONS)]

    r3 = lambda z: z.reshape(batch, seq, z.shape[-1])
    hb3, fb3 = r3(hb), r3(fb)
    ya = _na(hb3, fb3, _na_bias_tables(p["na_rpb"][l], seq // GRID_W), batch=batch, seq=seq)
    f, e = _rwprep(fb3, p["rw_mu"][l], p["rw_w0"][l], p["rw_w2"][l], p["rw_a0"][l], p["rw_a2"][l],
                   p["rw_kk"][l], p["rw_ka"][l], p["rw_rk"][l].reshape(-1), batch=batch, seq=seq)
    yf, yr = _rwscan(f, batch=batch, seq=seq)
    m = batch * seq
    yb = _rwpost(yf.reshape(m, BW), yr.reshape(m, BW), e.reshape(m, 2 * BW), p["rw_lnx_g"][l],
                 p["rw_lnx_b"][l])
    lam_init = 0.8 - 0.6 * math.exp(-0.3 * l)
    yc = _df(r3(cqk), hb3, fb3, p["df_lam"][l], p["df_subln_g"][l], lam_init, batch=batch, seq=seq)
    yd = _dl([dq.reshape(dq.shape[0], batch, seq, LANES) for dq in dqs], fb3, batch=batch, seq=seq)

    margs = (ya.reshape(m, BW), yb, yc.reshape(m, BW), yd.reshape(m, BW), gates, xf,
             p["w_branch"][l].astype(BF16), p["w_out"][l].astype(BF16), p["b_out"][l],
             p["ln_g"][l], p["ln_b"][l])
    if last_split is None:
        return _merge(*margs, row0=0, rows=m, with_bf16=True)
    return [_merge(*margs, row0=r0, rows=n, with_bf16=False)[0] for r0, n in last_split]


def kernel(x_prompt, x_sample, ln0_g, ln0_b, w_in, b_in, na_rpb, rw_mu, rw_w0, rw_w2, rw_a0, rw_a2,
           rw_kk, rw_ka, rw_rk, rw_lnx_g, rw_lnx_b, df_lam, df_subln_g, w_branch, w_out, b_out,
           ln_g, ln_b):
    p = dict(w_in=w_in, b_in=b_in, na_rpb=na_rpb, rw_mu=rw_mu, rw_w0=rw_w0, rw_w2=rw_w2,
             rw_a0=rw_a0, rw_a2=rw_a2, rw_kk=rw_kk, rw_ka=rw_ka, rw_rk=rw_rk, rw_lnx_g=rw_lnx_g,
             rw_lnx_b=rw_lnx_b, df_lam=df_lam, df_subln_g=df_subln_g, w_branch=w_branch,
             w_out=w_out, b_out=b_out, ln_g=ln_g, ln_b=ln_b)
    bp, seq, _ = x_prompt.shape
    bs = x_sample.shape[0]
    assert x_sample.shape[1] == seq
    batch = bp + bs
    xf, xb = _ln0(x_prompt.reshape(bp * seq, D_MODEL), x_sample.reshape(bs * seq, D_MODEL),
                  ln0_g, ln0_b)
    tabs = {"c": _rope_tables(seq, DF_HEAD_DIM, (DF_HEAD_DIM ** -0.5 * LOG2E, 1.0)),
            "d": _rope_tables(seq, DL_HEAD_DIM, (1.0,))}
    for l in range(DEPTH - 1):
        xf, xb = _layer(xf, xb, l, p, tabs, batch=batch, seq=seq, last_split=None)
    split = [(0, bp * seq), (bp * seq, bs * seq)]
    yp, ys = _layer(xf, xb, DEPTH - 1, p, tabs, batch=batch, seq=seq, last_split=split)
    return yp.reshape(bp, seq, D_MODEL), ys.reshape(bs, seq, D_MODEL)
```
